```python
import jax, jax.numpy as jnp
from jax import lax
import numpy as np

D_MODEL = 1024
BATCH = 4
SEQ = 8192
DEPTH = 2

HEAD_DIM = 64
NSA_HEADS = 8
NSA_KV_HEADS = 2
CMP_LEN = 32
CMP_STRIDE = 16
SEL_LEN = 64
SEL_TOPK = 16
NSA_WINDOW = 512
SWA_HEADS = 8
SWA_KV_HEADS = 2
SWA_WINDOW = 128
Q_BLOCK = 128
MLSTM_HEADS = 4
MLSTM_QK_DIM = 128
MLSTM_V_DIM = 256
MLSTM_CHUNK = 64
CONV_WIDTH = 4
FFN_HIDDEN = -(-8 * D_MODEL // (3 * 256)) * 256
RMS_EPS = 1e-6
NEG_INF = -1e30
FORCE_SCORE = 1e9

NSA_REP = NSA_HEADS // NSA_KV_HEADS
SWA_REP = SWA_HEADS // SWA_KV_HEADS
AB_SPLITS = (NSA_HEADS * HEAD_DIM,) + (NSA_KV_HEADS * HEAD_DIM,) * 6 + (NSA_HEADS * 3, SWA_HEADS * HEAD_DIM, SWA_KV_HEADS * HEAD_DIM, SWA_KV_HEADS * HEAD_DIM)
AB_IN = sum(AB_SPLITS)
AB_MIX = (NSA_HEADS + SWA_HEADS) * HEAD_DIM
C_SPLITS = (2 * MLSTM_HEADS * MLSTM_QK_DIM, MLSTM_HEADS * MLSTM_V_DIM, MLSTM_HEADS * MLSTM_V_DIM, MLSTM_HEADS, MLSTM_HEADS)
C_IN = sum(C_SPLITS)
C_MIX = MLSTM_HEADS * MLSTM_V_DIM

kernel_name = 'nsa_swasink_mlstm_hybrid'


def _split(z, sizes):
    idx = np.cumsum(np.array(sizes))[:-1].tolist()
    return jnp.split(z, idx, axis=-1)


def rmsnorm(x, g):
    xf = x.astype(jnp.float32)
    y = xf * lax.rsqrt(jnp.mean(xf * xf, axis=-1, keepdims=True) + RMS_EPS)
    return (y * g.astype(jnp.float32)).astype(x.dtype)


def swiglu(x, wg, wu, wd):
    return (jax.nn.silu(x @ wg) * (x @ wu)) @ wd


def masked_softmax(logits, mask):
    p = jax.nn.softmax(jnp.where(mask, logits, NEG_INF), axis=-1)
    return jnp.where(mask, p, 0.0)


def selection_map(seq):
    nc = (seq - CMP_LEN) // CMP_STRIDE + 1
    ns = seq // SEL_LEN
    cs = np.arange(nc, dtype=np.int32)[:, None] * CMP_STRIDE
    ss = np.arange(ns, dtype=np.int32)[None, :] * SEL_LEN
    return ((cs < ss + SEL_LEN) & (cs + CMP_LEN > ss)).astype(np.float32)


def compress_kv(kv, pos, w1, w2):
    b, t = kv.shape[:2]
    nc = (t - CMP_LEN) // CMP_STRIDE + 1
    idx = np.arange(nc, dtype=np.int32)[:, None] * CMP_STRIDE + np.arange(CMP_LEN, dtype=np.int32)[None, :]
    blk = kv[:, idx] + pos[None, None, :, None, :]
    blk = blk.transpose(0, 3, 1, 2, 4).reshape(b, kv.shape[2], nc, CMP_LEN * HEAD_DIM)
    return jax.nn.silu(blk @ w1) @ w2


def nsa_attention(q, kc, vc, ksel, vsel, kwin, vwin, gates):
    b, t = q.shape[:2]
    nc = kc.shape[2]
    nb = t // Q_BLOCK
    ns = t // SEL_LEN
    topk = min(SEL_TOPK, ns)
    scale = HEAD_DIM ** -0.5
    sel_map = jnp.asarray(selection_map(t))
    cmp_end = jnp.asarray(np.arange(nc, dtype=np.int32) * CMP_STRIDE + CMP_LEN - 1)
    ksb = ksel.reshape(b, ns, SEL_LEN, NSA_KV_HEADS, HEAD_DIM).transpose(0, 3, 1, 2, 4)
    vsb = vsel.reshape(b, ns, SEL_LEN, NSA_KV_HEADS, HEAD_DIM).transpose(0, 3, 1, 2, 4)
    pad = ((0, 0), (NSA_WINDOW, 0), (0, 0), (0, 0))
    kwp = jnp.pad(kwin, pad)
    vwp = jnp.pad(vwin, pad)
    span = NSA_WINDOW + Q_BLOCK
    gather = jax.vmap(jax.vmap(lambda kb, ix: kb[ix]))
    blocks = jnp.arange(ns, dtype=jnp.int32)

    def block(n):
        start = n * Q_BLOCK
        qb = lax.dynamic_slice_in_dim(q, start, Q_BLOCK, axis=1).transpose(0, 2, 3, 1, 4)
        gb = lax.dynamic_slice_in_dim(gates, start, Q_BLOCK, axis=1).transpose(0, 2, 3, 1, 4)
        tq = start + jnp.arange(Q_BLOCK, dtype=jnp.int32)
        s = jnp.einsum('bgrtd,bgcd->bgrtc', qb, kc).astype(jnp.float32) * scale
        p_c = masked_softmax(s, cmp_end[None, :] <= tq[:, None])
        o_c = jnp.einsum('bgrtc,bgcd->bgrtd', p_c.astype(vc.dtype), vc)
        imp = jnp.einsum('bgrtc,cs->bgts', p_c, sel_map)
        cur = tq // SEL_LEN
        valid = blocks[None, :] * SEL_LEN <= tq[:, None]
        forced = (blocks[None, :] == 0) | (blocks[None, :] == cur[:, None]) | (blocks[None, :] == cur[:, None] - 1)
        score = jnp.where(forced, FORCE_SCORE, jnp.where(valid, imp, NEG_INF))
        top_s, top_i = lax.top_k(score, topk)
        kg = gather(ksb, top_i)
        vg = gather(vsb, top_i)
        kpos = top_i[..., None] * SEL_LEN + jnp.arange(SEL_LEN, dtype=jnp.int32)
        m_s = (top_s > NEG_INF * 0.5)[..., None] & (kpos <= tq[:, None, None])
        m_s = m_s.reshape(b, NSA_KV_HEADS, 1, Q_BLOCK, topk * SEL_LEN)
        s = jnp.einsum('bgrtd,bgtkjd->bgrtkj', qb, kg).astype(jnp.float32) * scale
        p_s = masked_softmax(s.reshape(b, NSA_KV_HEADS, NSA_REP, Q_BLOCK, topk * SEL_LEN), m_s)
        o_s = jnp.einsum('bgrtm,bgtmd->bgrtd', p_s.astype(vg.dtype), vg.reshape(b, NSA_KV_HEADS, Q_BLOCK, topk * SEL_LEN, HEAD_DIM))
        kw = lax.dynamic_slice_in_dim(kwp, start, span, axis=1)
        vw = lax.dynamic_slice_in_dim(vwp, start, span, axis=1)
        sp = start - NSA_WINDOW + jnp.arange(span, dtype=jnp.int32)
        m_w = (sp[None, :] >= 0) & (sp[None, :] <= tq[:, None]) & (tq[:, None] - sp[None, :] < NSA_WINDOW)
        s = jnp.einsum('bgrtd,bsgd->bgrts', qb, kw).astype(jnp.float32) * scale
        p_w = masked_softmax(s, m_w)
        o_w = jnp.einsum('bgrts,bsgd->bgrtd', p_w.astype(vw.dtype), vw)
        o = gb[..., 0:1] * o_c + gb[..., 1:2] * o_s + gb[..., 2:3] * o_w
        return o.transpose(0, 3, 1, 2, 4).reshape(b, Q_BLOCK, NSA_HEADS * HEAD_DIM)

    out = lax.map(block, jnp.arange(nb, dtype=jnp.int32))
    return out.transpose(1, 0, 2, 3).reshape(b, t, NSA_HEADS * HEAD_DIM)


def swa_sink_attention(q, k, v, sinks):
    b, t = q.shape[:2]
    nb = t // Q_BLOCK
    span = SWA_WINDOW + Q_BLOCK
    scale = HEAD_DIM ** -0.5
    pad = ((0, 0), (SWA_WINDOW, 0), (0, 0), (0, 0))
    kp = jnp.pad(k, pad)
    vp = jnp.pad(v, pad)
    sink = sinks.astype(jnp.float32).reshape(1, SWA_KV_HEADS, SWA_REP, 1, 1)

    def block(n):
        start = n * Q_BLOCK
        qb = lax.dynamic_slice_in_dim(q, start, Q_BLOCK, axis=1).transpose(0, 2, 3, 1, 4)
        kb = lax.dynamic_slice_in_dim(kp, start, span, axis=1)
        vb = lax.dynamic_slice_in_dim(vp, start, span, axis=1)
        tq = start + jnp.arange(Q_BLOCK, dtype=jnp.int32)
        sp = start - SWA_WINDOW + jnp.arange(span, dtype=jnp.int32)
        mask = (sp[None, :] >= 0) & (sp[None, :] <= tq[:, None]) & (tq[:, None] - sp[None, :] < SWA_WINDOW)
        logits = jnp.where(mask, jnp.einsum('bgrtd,bsgd->bgrts', qb, kb).astype(jnp.float32) * scale, NEG_INF)
        mx = jnp.maximum(logits.max(axis=-1, keepdims=True), sink)
        e = jnp.exp(logits - mx)
        p = e / (e.sum(axis=-1, keepdims=True) + jnp.exp(sink - mx))
        o = jnp.einsum('bgrts,bsgd->bgrtd', p.astype(vb.dtype), vb)
        return o.transpose(0, 3, 1, 2, 4).reshape(b, Q_BLOCK, SWA_HEADS * HEAD_DIM)

    out = lax.map(block, jnp.arange(nb, dtype=jnp.int32))
    return out.transpose(1, 0, 2, 3).reshape(b, t, SWA_HEADS * HEAD_DIM)


def nsa_swa_mixer(h, w_in, gate_bias, pos_k, pos_v, ck_w1, ck_w2, cv_w1, cv_w2, sinks, w_out):
    b, t, _ = h.shape
    qa, kc, vc, ks, vs, kw, vw, g, qb, kb, vb = _split(h @ w_in, AB_SPLITS)
    kv = lambda a: a.reshape(b, t, NSA_KV_HEADS, HEAD_DIM)
    qa = qa.reshape(b, t, NSA_KV_HEADS, NSA_REP, HEAD_DIM)
    kc = compress_kv(kv(kc), pos_k, ck_w1, ck_w2)
    vc = compress_kv(kv(vc), pos_v, cv_w1, cv_w2)
    gates = jax.nn.sigmoid(g + gate_bias).reshape(b, t, NSA_KV_HEADS, NSA_REP, 3)
    o_a = nsa_attention(qa, kc, vc, kv(ks), kv(vs), kv(kw), kv(vw), gates)
    qb = qb.reshape(b, t, SWA_KV_HEADS, SWA_REP, HEAD_DIM)
    kb = kb.reshape(b, t, SWA_KV_HEADS, HEAD_DIM)
    vb = vb.reshape(b, t, SWA_KV_HEADS, HEAD_DIM)
    o_b = swa_sink_attention(qb, kb, vb, sinks)
    return jnp.concatenate([o_a, o_b], axis=-1) @ w_out


def causal_depthwise_conv(x, w, bias):
    ch = x.shape[-1]
    y = lax.conv_general_dilated(x, w[:, None, :].astype(x.dtype), window_strides=(1,), padding=((CONV_WIDTH - 1, 0),), dimension_numbers=('NWC', 'WIO', 'NWC'), feature_group_count=ch)
    return y + bias


def mlstm_chunkwise(q, k, v, logi, logf):
    b, nh, t, dk = q.shape
    dv = v.shape[-1]
    L = MLSTM_CHUNK
    nc = t // L

    def chunks(a):
        return jnp.moveaxis(a.reshape(a.shape[:2] + (nc, L) + a.shape[3:]), 2, 0)

    causal = jnp.tril(jnp.ones((L, L), dtype=bool))

    def step(carry, inp):
        C, n, m = carry
        qc, kc, vc, ic, fc = inp
        bcum = jnp.cumsum(fc, axis=-1)
        a = bcum + m[..., None]
        D = jnp.where(causal, bcum[..., :, None] - bcum[..., None, :] + ic[..., None, :], -jnp.inf)
        mt = jnp.maximum(a, D.max(axis=-1))
        w_inter = jnp.exp(a - mt)
        S = jnp.einsum('bhtd,bhsd->bhts', qc, kc) * jnp.exp(D - mt[..., None])
        num = w_inter[..., None] * jnp.einsum('bhtd,bhdv->bhtv', qc, C) + jnp.einsum('bhts,bhsv->bhtv', S, vc)
        den = w_inter * jnp.einsum('bhtd,bhd->bht', qc, n) + S.sum(axis=-1)
        hc = num / jnp.maximum(jnp.abs(den), jnp.exp(-mt))[..., None]
        bl = bcum[..., -1]
        wl = bl[..., None] - bcum + ic
        m_new = jnp.maximum(bl + m, wl.max(axis=-1))
        decay = jnp.exp(bl + m - m_new)
        w = jnp.exp(wl - m_new[..., None])
        C_new = decay[..., None, None] * C + jnp.einsum('bhs,bhsd,bhsv->bhdv', w, kc, vc)
        n_new = decay[..., None] * n + jnp.einsum('bhs,bhsd->bhd', w, kc)
        return (C_new, n_new, m_new), hc

    init = (jnp.zeros((b, nh, dk, dv), jnp.float32), jnp.zeros((b, nh, dk), jnp.float32), jnp.zeros((b, nh), jnp.float32))
    _, hs = lax.scan(step, init, (chunks(q), chunks(k), chunks(v), chunks(logi), chunks(logf)))
    return jnp.moveaxis(hs, 0, 2).reshape(b, nh, t, dv)


def mlstm_mixer(h, w_in, conv_w, conv_b, igate_bias, fgate_bias, w_out):
    b, t, _ = h.shape
    qk, v, og, ig, fg = _split(h @ w_in, C_SPLITS)
    qk = jax.nn.silu(causal_depthwise_conv(qk, conv_w, conv_b))
    q, k = jnp.split(qk, 2, axis=-1)

    def heads(a, d):
        return a.reshape(b, t, MLSTM_HEADS, d).transpose(0, 2, 1, 3).astype(jnp.float32)

    q = heads(q, MLSTM_QK_DIM)
    k = heads(k, MLSTM_QK_DIM) * (MLSTM_QK_DIM ** -0.5)
    v = heads(v, MLSTM_V_DIM)
    logi = (ig + igate_bias).astype(jnp.float32).transpose(0, 2, 1)
    logf = jax.nn.log_sigmoid((fg + fgate_bias).astype(jnp.float32)).transpose(0, 2, 1)
    hh = mlstm_chunkwise(q, k, v, logi, logf)
    hh = hh.transpose(0, 2, 1, 3).reshape(b, t, C_MIX).astype(h.dtype) * jax.nn.sigmoid(og)
    return hh @ w_out


def setup_inputs(seed: int = 0) -> dict:
    key = jax.random.key(seed)
    ks = jax.random.split(key, 24)
    ne = (DEPTH + 1) // 2
    no = DEPTH // 2

    def nrm(k, shape, scale):
        return jax.random.normal(k, shape, jnp.float32) * scale

    qk_ch = 2 * MLSTM_HEADS * MLSTM_QK_DIM
    return {
        'x': nrm(ks[0], (BATCH, SEQ, D_MODEL), 1.0),
        'norm_mix': 1.0 + nrm(ks[1], (DEPTH, D_MODEL), 0.02),
        'norm_ffn': 1.0 + nrm(ks[2], (DEPTH, D_MODEL), 0.02),
        'ffn_w_gate': nrm(ks[3], (DEPTH, D_MODEL, FFN_HIDDEN), D_MODEL ** -0.5),
        'ffn_w_up': nrm(ks[4], (DEPTH, D_MODEL, FFN_HIDDEN), D_MODEL ** -0.5),
        'ffn_w_down': nrm(ks[5], (DEPTH, FFN_HIDDEN, D_MODEL), FFN_HIDDEN ** -0.5),
        'ab_w_in': nrm(ks[6], (ne, D_MODEL, AB_IN), D_MODEL ** -0.5),
        'ab_gate_bias': nrm(ks[7], (ne, NSA_HEADS * 3), 0.1),
        'nsa_pos_k': nrm(ks[8], (ne, CMP_LEN, HEAD_DIM), 0.1),
        'nsa_pos_v': nrm(ks[9], (ne, CMP_LEN, HEAD_DIM), 0.1),
        'nsa_cmp_k_w1': nrm(ks[10], (ne, CMP_LEN * HEAD_DIM, HEAD_DIM), (CMP_LEN * HEAD_DIM) ** -0.5),
        'nsa_cmp_k_w2': nrm(ks[11], (ne, HEAD_DIM, HEAD_DIM), HEAD_DIM ** -0.5),
        'nsa_cmp_v_w1': nrm(ks[12], (ne, CMP_LEN * HEAD_DIM, HEAD_DIM), (CMP_LEN * HEAD_DIM) ** -0.5),
        'nsa_cmp_v_w2': nrm(ks[13], (ne, HEAD_DIM, HEAD_DIM), HEAD_DIM ** -0.5),
        'swa_sinks': nrm(ks[14], (ne, SWA_HEADS), 0.5),
        'ab_w_out': nrm(ks[15], (ne, AB_MIX, D_MODEL), AB_MIX ** -0.5),
        'c_w_in': nrm(ks[16], (no, D_MODEL, C_IN), D_MODEL ** -0.5),
        'c_conv_w': nrm(ks[17], (no, CONV_WIDTH, qk_ch), CONV_WIDTH ** -0.5),
        'c_conv_b': nrm(ks[18], (no, qk_ch), 0.02),
        'c_igate_bias': nrm(ks[19], (no, MLSTM_HEADS), 0.1),
        'c_fgate_bias': jnp.linspace(3.0, 6.0, MLSTM_HEADS, dtype=jnp.float32)[None, :] + nrm(ks[20], (no, MLSTM_HEADS), 0.1),
        'c_w_out': nrm(ks[21], (no, C_MIX, D_MODEL), C_MIX ** -0.5),
        'final_norm': 1.0 + nrm(ks[22], (D_MODEL,), 0.02),
    }


def reference(x, norm_mix, norm_ffn, ffn_w_gate, ffn_w_up, ffn_w_down, ab_w_in, ab_gate_bias, nsa_pos_k, nsa_pos_v, nsa_cmp_k_w1, nsa_cmp_k_w2, nsa_cmp_v_w1, nsa_cmp_v_w2, swa_sinks, ab_w_out, c_w_in, c_conv_w, c_conv_b, c_igate_bias, c_fgate_bias, c_w_out, final_norm):
    h = x
    for layer in range(DEPTH):
        j = layer // 2
        hn = rmsnorm(h, norm_mix[layer])
        if layer % 2 == 0:
            h = h + nsa_swa_mixer(hn, ab_w_in[j], ab_gate_bias[j], nsa_pos_k[j], nsa_pos_v[j], nsa_cmp_k_w1[j], nsa_cmp_k_w2[j], nsa_cmp_v_w1[j], nsa_cmp_v_w2[j], swa_sinks[j], ab_w_out[j])
        else:
            h = h + mlstm_mixer(hn, c_w_in[j], c_conv_w[j], c_conv_b[j], c_igate_bias[j], c_fgate_bias[j], c_w_out[j])
        h = h + swiglu(rmsnorm(h, norm_ffn[layer]), ffn_w_gate[layer], ffn_w_up[layer], ffn_w_down[layer])
    return rmsnorm(h, final_norm)
```

```python
import functools

import numpy as np
import jax
import jax.numpy as jnp
from jax import lax
from jax.experimental import pallas as pl
from jax.experimental.pallas import tpu as pltpu

HEAD_DIM = 64
NSA_HEADS = 8
NSA_KV_HEADS = 2
NSA_REP = NSA_HEADS // NSA_KV_HEADS
CMP_LEN = 32
CMP_STRIDE = 16
SEL_LEN = 64
SEL_TOPK = 16
NSA_WINDOW = 512
SWA_HEADS = 8
SWA_KV_HEADS = 2
SWA_REP = SWA_HEADS // SWA_KV_HEADS
SWA_WINDOW = 128
Q_BLOCK = 128
MLSTM_HEADS = 4
MLSTM_QK_DIM = 128
MLSTM_V_DIM = 256
MLSTM_CHUNK = 64
CONV_WIDTH = 4
RMS_EPS = 1e-6
NEG_INF = -1e30
FORCE_SCORE = 1e9
BELOW_NEG_INF = -3e38
SEL_SHIFT = SEL_LEN.bit_length() - 1
assert 1 << SEL_SHIFT == SEL_LEN

LANES = 128
SUBLANES = 8
VMEM_LIMIT_BYTES = 56 * 1024 * 1024

ROW_TILE = 512
SEL_KEY_TILE = 512
MLSTM_CHUNKS_PER_STEP = 8

F32 = jnp.float32
BF16 = jnp.bfloat16


def _cparams(n_axes):
    return pltpu.CompilerParams(
        dimension_semantics=("arbitrary",) * n_axes,
        vmem_limit_bytes=VMEM_LIMIT_BYTES,
    )


def _resident(shape):
    nd = len(shape)
    return pl.BlockSpec(shape, lambda *_: (0,) * nd)


def _dot(a, b):
    return jnp.dot(a, b, preferred_element_type=F32)


def _sigmoid(x):
    return 1.0 / (1.0 + jnp.exp(-x))


def _silu(x):
    return x * _sigmoid(x)


def _rmsnorm(x, g):
    y = x * lax.rsqrt(jnp.mean(x * x, axis=-1, keepdims=True) + RMS_EPS)
    return y * g


def _norm_proj_kernel(n_out, x_ref, g_ref, *refs):
    w_refs, o_refs = refs[:n_out], refs[n_out:]
    xn = _rmsnorm(x_ref[...], g_ref[...]).astype(BF16)
    for w_ref, o_ref in zip(w_refs, o_refs):
        o_ref[...] = _dot(xn, w_ref[...]).astype(o_ref.dtype)


def _norm_proj(x, g, ws, out_dtypes):
    m, d = x.shape
    n_out = len(ws)
    return pl.pallas_call(
        functools.partial(_norm_proj_kernel, n_out),
        grid=(m // ROW_TILE,),
        in_specs=[pl.BlockSpec((ROW_TILE, d), lambda i: (i, 0)), _resident((1, d))]
        + [_resident(w.shape) for w in ws],
        out_specs=[pl.BlockSpec((ROW_TILE, w.shape[1]), lambda i: (i, 0)) for w in ws],
        out_shape=[jax.ShapeDtypeStruct((m, w.shape[1]), dt) for w, dt in zip(ws, out_dtypes)],
        compiler_params=_cparams(1),
        name="norm_proj",
    )(x, g.reshape(1, d), *ws)


def _out_proj_kernel(n_in, gated, res_ref, *refs):
    a_refs = refs[:n_in]
    w_refs = refs[n_in:2 * n_in]
    rest = refs[2 * n_in:]
    o_ref = rest[-1]
    acc = res_ref[...]
    for idx, (a_ref, w_ref) in enumerate(zip(a_refs, w_refs)):
        a = a_ref[...]
        if gated and idx == 0:
            a = (a.astype(F32) * _sigmoid(rest[0][...])).astype(BF16)
        acc = acc + _dot(a.astype(BF16), w_ref[...])
    o_ref[...] = acc


def _out_proj(res, a_list, w_list, gate=None):
    m, d = res.shape
    n_in = len(a_list)
    row = lambda w: pl.BlockSpec((ROW_TILE, w), lambda i: (i, 0))
    in_specs = [row(d)] + [row(a.shape[1]) for a in a_list] + [_resident(w.shape) for w in w_list]
    args = [res, *a_list, *w_list]
    if gate is not None:
        in_specs.append(row(gate.shape[1]))
        args.append(gate)
    return pl.pallas_call(
        functools.partial(_out_proj_kernel, n_in, gate is not None),
        grid=(m // ROW_TILE,),
        in_specs=in_specs,
        out_specs=row(d),
        out_shape=jax.ShapeDtypeStruct((m, d), F32),
        compiler_params=_cparams(1),
        name="out_proj",
    )(*args)


def _ffn_kernel(n_chunks, chunk, final, x_ref, g_ref, wg_ref, wu_ref, wd_ref, *rest):
    o_ref = rest[-1]
    x = x_ref[...]
    xn = _rmsnorm(x, g_ref[...]).astype(BF16)
    acc = x
    for c in range(n_chunks):
        cols = slice(c * chunk, (c + 1) * chunk)
        gate = _dot(xn, wg_ref[:, cols])
        up = _dot(xn, wu_ref[:, cols])
        h = (_silu(gate) * up).astype(BF16)
        acc = acc + _dot(h, wd_ref[cols, :])
    if final:
        acc = _rmsnorm(acc, rest[0][...])
    o_ref[...] = acc


def _ffn_chunk(hidden):
    return hidden


def _ffn(x, g, wg, wu, wd, final_g=None):
    m, d = x.shape
    hidden = wg.shape[1]
    chunk = _ffn_chunk(hidden)
    row = pl.BlockSpec((ROW_TILE, d), lambda i: (i, 0))
    in_specs = [row, _resident((1, d)), _resident(wg.shape), _resident(wu.shape), _resident(wd.shape)]
    args = [x, g.reshape(1, d), wg, wu, wd]
    if final_g is not None:
        in_specs.append(_resident((1, d)))
        args.append(final_g.reshape(1, d))
    return pl.pallas_call(
        functools.partial(_ffn_kernel, hidden // chunk, chunk, final_g is not None),
        grid=(m // ROW_TILE,),
        in_specs=in_specs,
        out_specs=row,
        out_shape=jax.ShapeDtypeStruct((m, d), F32),
        compiler_params=_cparams(1),
        name="ffn",
    )(*args)


def _compress_kernel(x_ref, p_ref, w1_ref, w2_ref, o_ref):
    x = x_ref[0, 0, 0]
    rows = x.shape[0]
    half = x.shape[1]
    a = _dot((x + p_ref[0, 0]).astype(BF16), w1_ref[0, :half, :])
    b = _dot((x + p_ref[0, 1]).astype(BF16), w1_ref[0, half:, :])
    h = a + pltpu.roll(b, rows - 1, 0)
    out = _dot(_silu(h).astype(BF16), w2_ref[0])
    live = lax.broadcasted_iota(jnp.int32, out.shape, 0) < rows - 1
    o_ref[0, 0, 0] = jnp.where(live, out, 0.0).astype(o_ref.dtype)


def _compress(x, pos, w1, w2):
    two, b, g, rows, width = x.shape
    return pl.pallas_call(
        _compress_kernel,
        grid=(two, b, g),
        in_specs=[
            pl.BlockSpec((1, 1, 1, rows, width), lambda s, i, j: (s, i, j, 0, 0)),
            pl.BlockSpec((1, 2, 1, width), lambda s, i, j: (s, 0, 0, 0)),
            pl.BlockSpec((1,) + w1.shape[1:], lambda s, i, j: (s, 0, 0)),
            pl.BlockSpec((1,) + w2.shape[1:], lambda s, i, j: (s, 0, 0)),
        ],
        out_specs=pl.BlockSpec((1, 1, 1, rows, HEAD_DIM), lambda s, i, j: (s, i, j, 0, 0)),
        out_shape=jax.ShapeDtypeStruct((two, b, g, rows, HEAD_DIM), BF16),
        compiler_params=_cparams(3),
        name="compress_kv",
    )(x, pos, w1, w2)


def _stack_heads(q_ref, rep):
    q = q_ref[0].astype(F32) * (HEAD_DIM ** -0.5)
    return jnp.concatenate([q[:, r * HEAD_DIM:(r + 1) * HEAD_DIM] for r in range(rep)], axis=0).astype(BF16)


def _per_head(x, rep):
    return x.reshape(rep, x.shape[0] // rep, x.shape[1])


def _banded_softmax_pv(qs, k_tiles, v_tiles, first_pos, tq, window, rep, sinks=None):
    kt = jnp.concatenate(k_tiles, axis=1)
    v = jnp.concatenate(v_tiles, axis=0)
    s = _per_head(_dot(qs, kt), rep)
    sp = first_pos + lax.broadcasted_iota(jnp.int32, (tq.shape[0], kt.shape[1]), 1)
    allowed = (sp >= 0) & (sp <= tq) & (tq - sp < window)
    s = jnp.where(allowed[None], s, NEG_INF)
    mx = jnp.max(s, axis=-1, keepdims=True)
    sink = None
    if sinks is not None:
        sink = jnp.concatenate([jnp.full((1, tq.shape[0], 1), sk, F32) for sk in sinks], axis=0)
        mx = jnp.maximum(mx, sink)
    e = jnp.exp(s - mx)
    den = jnp.sum(e, axis=-1, keepdims=True)
    if sink is not None:
        den = den + jnp.exp(sink - mx)
    p = (e * (1.0 / den)).astype(BF16)
    return _dot(p.reshape(qs.shape[0], kt.shape[1]), v)


def _swa_kernel(sink_ref, q_ref, kp_ref, kc_ref, vp_ref, vc_ref, o_ref):
    g, n = pl.program_id(1), pl.program_id(2)
    qs = _stack_heads(q_ref, SWA_REP)
    tq = n * Q_BLOCK + lax.broadcasted_iota(jnp.int32, (Q_BLOCK, 1), 0)
    sinks = [sink_ref[g * SWA_REP + r] for r in range(SWA_REP)]
    o = _banded_softmax_pv(qs, [kp_ref[0, 0, 0], kc_ref[0, 0, 0]], [vp_ref[0, 0, 0], vc_ref[0, 0, 0]],
                           (n - 1) * Q_BLOCK, tq, SWA_WINDOW, SWA_REP, sinks)
    for r in range(SWA_REP):
        o_ref[0, :, r * HEAD_DIM:(r + 1) * HEAD_DIM] = o[r * Q_BLOCK:(r + 1) * Q_BLOCK].astype(o_ref.dtype)


def _swa_attention(q, kt_tiles, v_tiles, sinks, q_col_block):
    b, t = q.shape[:2]
    g, nq = kt_tiles.shape[1], kt_tiles.shape[2]
    width = SWA_REP * HEAD_DIM
    prev = lambda i, j, n: (i, j, jnp.maximum(n - 1, 0), 0, 0)
    cur = lambda i, j, n: (i, j, n, 0, 0)
    return pl.pallas_call(
        _swa_kernel,
        grid=(b, g, nq),
        in_specs=[
            pl.BlockSpec(memory_space=pltpu.SMEM),
            pl.BlockSpec((1, Q_BLOCK, width), lambda i, j, n: (i, n, q_col_block + j)),
            pl.BlockSpec((1, 1, 1, HEAD_DIM, Q_BLOCK), prev),
            pl.BlockSpec((1, 1, 1, HEAD_DIM, Q_BLOCK), cur),
            pl.BlockSpec((1, 1, 1, Q_BLOCK, HEAD_DIM), prev),
            pl.BlockSpec((1, 1, 1, Q_BLOCK, HEAD_DIM), cur),
        ],
        out_specs=pl.BlockSpec((1, Q_BLOCK, width), lambda i, j, n: (i, n, j)),
        out_shape=jax.ShapeDtypeStruct((b, t, g * width), BF16),
        compiler_params=_cparams(3),
        name="swa_sink_attention",
    )(sinks, q, kt_tiles, kt_tiles, v_tiles, v_tiles)


def _nsa_kernel(n_sel_blocks, q_ref, gl_ref, gb_ref, kct_ref, vcs_ref, kst_ref, vs_ref, kwt_ref, vw_ref,
                expand_ref, o_ref, m_sc, l_sc, acc_sc):
    n = pl.program_id(2)
    rep = NSA_REP
    qs = _stack_heads(q_ref, rep)
    tq = n * Q_BLOCK + lax.broadcasted_iota(jnp.int32, (Q_BLOCK, 1), 0)

    kct = kct_ref[0, 0]
    n_cmp = kct.shape[1]
    s = _per_head(_dot(qs, kct), rep)
    cmp_end = lax.broadcasted_iota(jnp.int32, (Q_BLOCK, n_cmp), 1) * CMP_STRIDE + (CMP_LEN - 1)
    vis = (cmp_end <= tq)[None]
    s = jnp.where(vis, s, NEG_INF)
    e = jnp.where(vis, jnp.exp(s - jnp.max(s, axis=-1, keepdims=True)), 0.0)
    den = jnp.sum(e, axis=-1, keepdims=True)
    p = (e * (1.0 / jnp.where(den > 0.0, den, 1.0))).astype(BF16)
    oc_imp = _dot(p.reshape(rep * Q_BLOCK, n_cmp), vcs_ref[0, 0])
    o_c = oc_imp[:, :HEAD_DIM]
    imp = jnp.sum(_per_head(oc_imp[:, HEAD_DIM:], rep), axis=0)

    n_win_tiles = NSA_WINDOW // Q_BLOCK + 1
    tiles = [jnp.maximum(n - (n_win_tiles - 1) + i, 0) for i in range(n_win_tiles)]
    o_w = _banded_softmax_pv(qs, [kwt_ref[0, 0, j] for j in tiles], [vw_ref[0, 0, j] for j in tiles],
                             (n - (n_win_tiles - 1)) * Q_BLOCK, tq, NSA_WINDOW, rep)

    ns_pad = imp.shape[1]
    blk = lax.broadcasted_iota(jnp.int32, (Q_BLOCK, ns_pad), 1)
    blk_f = blk.astype(F32)
    cur = jnp.right_shift(tq, SEL_SHIFT)
    forced = (blk == 0) | (blk == cur) | (blk == cur - 1)
    valid = jnp.left_shift(blk, SEL_SHIFT) <= tq
    score = jnp.where(forced, FORCE_SCORE, jnp.where(valid, imp, NEG_INF))
    score = jnp.where(blk < n_sel_blocks, score, BELOW_NEG_INF)
    sel = jnp.zeros((Q_BLOCK, ns_pad), F32)
    for _ in range(min(SEL_TOPK, n_sel_blocks)):
        top = jnp.max(score, axis=-1, keepdims=True)
        first = jnp.min(jnp.where(score == top, blk_f, float(ns_pad)), axis=-1, keepdims=True)
        pick = blk_f == first
        sel = jnp.where(pick, 1.0, sel)
        score = jnp.where(pick, BELOW_NEG_INF, score)
    sel = sel.astype(BF16)

    m_sc[...] = jnp.full(m_sc.shape, NEG_INF, F32)
    l_sc[...] = jnp.zeros(l_sc.shape, F32)
    acc_sc[...] = jnp.zeros(acc_sc.shape, F32)
    tk = kst_ref.shape[4]

    def key_tile(kt, carry):
        s = _per_head(_dot(qs, kst_ref[0, 0, kt]), rep)
        chosen = _dot(sel, expand_ref[kt])
        kpos = kt * tk + lax.broadcasted_iota(jnp.int32, (Q_BLOCK, tk), 1)
        allowed = ((chosen > 0.5) & (kpos <= tq))[None]
        s = jnp.where(allowed, s, NEG_INF)
        m_old = m_sc[...]
        m_new = jnp.maximum(m_old, jnp.max(s, axis=-1, keepdims=True))
        alpha = jnp.exp(m_old - m_new)
        e = jnp.where(allowed, jnp.exp(s - m_new), 0.0)
        l_sc[...] = alpha * l_sc[...] + jnp.sum(e, axis=-1, keepdims=True)
        pv = _dot(e.astype(BF16).reshape(rep * Q_BLOCK, tk), vs_ref[0, 0, kt])
        acc_sc[...] = alpha * acc_sc[...] + _per_head(pv, rep)
        m_sc[...] = m_new
        return carry

    n_key_tiles = lax.div((n + 1) * Q_BLOCK + tk - 1, tk)
    lax.fori_loop(0, n_key_tiles, key_tile, 0)
    o_s = (acc_sc[...] * (1.0 / l_sc[...])).reshape(rep * Q_BLOCK, HEAD_DIM)

    gates = _sigmoid(gl_ref[0, 0] + gb_ref[0])
    for r in range(rep):
        rows = slice(r * Q_BLOCK, (r + 1) * Q_BLOCK)
        o = (gates[:, 3 * r:3 * r + 1] * o_c[rows] + gates[:, 3 * r + 1:3 * r + 2] * o_s[rows]
             + gates[:, 3 * r + 2:3 * r + 3] * o_w[rows])
        o_ref[0, :, r * HEAD_DIM:(r + 1) * HEAD_DIM] = o.astype(o_ref.dtype)


def _selection_map(t, nc_pad, ns_pad):
    nc = (t - CMP_LEN) // CMP_STRIDE + 1
    ns = t // SEL_LEN
    cs = np.arange(nc, dtype=np.int64)[:, None] * CMP_STRIDE
    ss = np.arange(ns, dtype=np.int64)[None, :] * SEL_LEN
    out = np.zeros((nc_pad, ns_pad), np.float32)
    out[:nc, :ns] = (cs < ss + SEL_LEN) & (cs + CMP_LEN > ss)
    return out


def _block_expansion(t, ns_pad, tk):
    key_block = (np.arange(t, dtype=np.int64) // SEL_LEN).reshape(t // tk, 1, tk)
    return (np.arange(ns_pad, dtype=np.int64)[None, :, None] == key_block).astype(np.float32)


def _nsa_attention(q, gate_logits, gate_bias, kct, vcs, kst, vs, kwt, vw, expand, n_sel_blocks):
    b, t = q.shape[:2]
    g = kct.shape[1]
    width = NSA_REP * HEAD_DIM
    ng = gate_logits.shape[-1]
    per_group = lambda a: pl.BlockSpec((1, 1) + a.shape[2:], lambda i, j, n: (i, j) + (0,) * (a.ndim - 2))
    return pl.pallas_call(
        functools.partial(_nsa_kernel, n_sel_blocks),
        grid=(b, g, t // Q_BLOCK),
        in_specs=[
            pl.BlockSpec((1, Q_BLOCK, width), lambda i, j, n: (i, n, j)),
            pl.BlockSpec((1, 1, Q_BLOCK, ng), lambda i, j, n: (i, j, n, 0)),
            pl.BlockSpec((1, 1, ng), lambda i, j, n: (j, 0, 0)),
            per_group(kct), per_group(vcs), per_group(kst), per_group(vs), per_group(kwt), per_group(vw),
            _resident(expand.shape),
        ],
        out_specs=pl.BlockSpec((1, Q_BLOCK, width), lambda i, j, n: (i, n, j)),
        out_shape=jax.ShapeDtypeStruct((b, t, g * width), BF16),
        scratch_shapes=[
            pltpu.VMEM((NSA_REP, Q_BLOCK, 1), F32),
            pltpu.VMEM((NSA_REP, Q_BLOCK, 1), F32),
            pltpu.VMEM((NSA_REP, Q_BLOCK, HEAD_DIM), F32),
        ],
        compiler_params=_cparams(3),
        name="nsa_attention",
    )(q, gate_logits, gate_bias, kct, vcs, kst, vs, kwt, vw, expand)


def _conv_silu_kernel(n_q_cols, x_ref, prev_ref, w_ref, b_ref, o_ref, buf):
    i = pl.program_id(1)
    rows = x_ref.shape[1]
    halo = prev_ref.shape[1]
    buf[:halo, :] = jnp.where(i > 0, prev_ref[0], 0.0)
    buf[halo:, :] = x_ref[0]
    y = b_ref[...]
    for j in range(CONV_WIDTH):
        start = halo - (CONV_WIDTH - 1) + j
        y = y + w_ref[j:j + 1, :] * buf[start:start + rows, :]
    y = _silu(y)
    col = lax.broadcasted_iota(jnp.int32, y.shape, 1)
    o_ref[0] = jnp.where(col >= n_q_cols, y * (MLSTM_QK_DIM ** -0.5), y).astype(o_ref.dtype)


def _conv_silu(x, w, bias):
    b, t, c = x.shape
    halo = SUBLANES
    ratio = ROW_TILE // halo
    return pl.pallas_call(
        functools.partial(_conv_silu_kernel, c // 2),
        grid=(b, t // ROW_TILE),
        in_specs=[
            pl.BlockSpec((1, ROW_TILE, c), lambda i, j: (i, j, 0)),
            pl.BlockSpec((1, halo, c), lambda i, j: (i, jnp.maximum(j * ratio - 1, 0), 0)),
            _resident(w.shape),
            _resident((1, c)),
        ],
        out_specs=pl.BlockSpec((1, ROW_TILE, c), lambda i, j: (i, j, 0)),
        out_shape=jax.ShapeDtypeStruct((b, t, c), BF16),
        scratch_shapes=[pltpu.VMEM((ROW_TILE + halo, c), F32)],
        compiler_params=_cparams(2),
        name="conv_silu",
    )(x, x, w, bias.reshape(1, c))


def _mlstm_kernel(bias_ref, q_ref, k_ref, v_ref, ig_ref, fg_ref, o_ref, state, m_state):
    h, c = pl.program_id(1), pl.program_id(2)
    L = MLSTM_CHUNK
    dv = MLSTM_V_DIM

    @pl.when(c == 0)
    def _():
        state[...] = jnp.zeros(state.shape, F32)
        m_state[...] = jnp.zeros(m_state.shape, F32)

    t_idx = lax.broadcasted_iota(jnp.int32, (L, L), 0)
    s_idx = lax.broadcasted_iota(jnp.int32, (L, L), 1)
    causal = s_idx <= t_idx
    eye = s_idx == t_idx
    ones_col = (lax.broadcasted_iota(jnp.int32, (L, LANES), 1) == 0).astype(BF16)
    ib = bias_ref[0, h]
    fb = bias_ref[1, h]

    def to_col(row):
        return jnp.sum(jnp.where(eye, row, 0.0), axis=1, keepdims=True)

    def chunk(j, carry):
        rows = pl.ds(pl.multiple_of(j * L, L), L)
        qc = q_ref[0, rows, :]
        kc = k_ref[0, rows, :]
        va = jnp.concatenate([v_ref[0, rows, :], ones_col], axis=1)
        i_row = ig_ref[0, 0, pl.ds(j, 1), :] + ib
        f_pre = fg_ref[0, 0, pl.ds(j, 1), :] + fb
        f_row = jnp.minimum(f_pre, 0.0) - jnp.log(1.0 + jnp.exp(-jnp.abs(f_pre)))
        f_col = to_col(f_row)
        b_col = jnp.sum(jnp.where(causal, f_row, 0.0), axis=1, keepdims=True)
        b_row = jnp.sum(jnp.where(t_idx <= s_idx, f_col, 0.0), axis=0, keepdims=True)
        m_old = m_state[...]
        a = b_col + m_old
        d = jnp.where(causal, b_col - b_row + i_row, -jnp.inf)
        mt = jnp.maximum(a, jnp.max(d, axis=1, keepdims=True))
        w_inter = jnp.exp(a - mt)
        sm = lax.dot_general(qc, kc, (((1,), (1,)), ((), ())), preferred_element_type=F32) * jnp.exp(d - mt)
        st = state[...]
        nd = w_inter * _dot(qc, st.astype(BF16)) + _dot(sm.astype(BF16), va)
        num = nd[:, :dv]
        den = nd[:, dv:dv + 1]
        hc = num * (1.0 / jnp.maximum(jnp.abs(den), jnp.exp(-mt)))
        o_ref[0, rows, :] = hc.astype(o_ref.dtype)
        b_last = b_row[:, L - 1:L]
        wl = b_last - b_row + i_row
        m_new = jnp.maximum(b_last + m_old, jnp.max(wl, axis=1, keepdims=True))
        decay = jnp.exp(b_last + m_old - m_new)
        w_col = to_col(jnp.exp(wl - m_new))
        kw = (kc.astype(F32) * w_col).astype(BF16)
        upd = lax.dot_general(kw, va, (((0,), (0,)), ((), ())), preferred_element_type=F32)
        state[...] = decay * st + upd
        m_state[...] = m_new
        return carry

    lax.fori_loop(0, MLSTM_CHUNKS_PER_STEP, chunk, 0)


def _mlstm(qk, v, ig, fg, gate_bias):
    b, t = qk.shape[:2]
    nh = MLSTM_HEADS
    rows = MLSTM_CHUNKS_PER_STEP * MLSTM_CHUNK
    gate_spec = pl.BlockSpec((1, 1, MLSTM_CHUNKS_PER_STEP, MLSTM_CHUNK), lambda i, h, c: (i, h, c, 0))
    return pl.pallas_call(
        _mlstm_kernel,
        grid=(b, nh, t // rows),
        in_specs=[
            pl.BlockSpec(memory_space=pltpu.SMEM),
            pl.BlockSpec((1, rows, MLSTM_QK_DIM), lambda i, h, c: (i, c, h)),
            pl.BlockSpec((1, rows, MLSTM_QK_DIM), lambda i, h, c: (i, c, nh + h)),
            pl.BlockSpec((1, rows, MLSTM_V_DIM), lambda i, h, c: (i, c, h)),
            gate_spec, gate_spec,
        ],
        out_specs=pl.BlockSpec((1, rows, MLSTM_V_DIM), lambda i, h, c: (i, c, h)),
        out_shape=jax.ShapeDtypeStruct((b, t, nh * MLSTM_V_DIM), BF16),
        scratch_shapes=[
            pltpu.VMEM((MLSTM_QK_DIM, MLSTM_V_DIM + LANES), F32),
            pltpu.VMEM((1, 1), F32),
        ],
        compiler_params=_cparams(3),
        name="mlstm_scan",
    )(gate_bias, qk, qk, v, ig, fg)


def _key_tiles(a, tk):
    b, t, _ = a.shape
    a = a.reshape(b, t // tk, tk, -1, HEAD_DIM)
    return a.transpose(0, 3, 1, 4, 2)


def _value_tiles(a, tk):
    b, t, _ = a.shape
    a = a.reshape(b, t // tk, tk, -1, HEAD_DIM)
    return a.transpose(0, 3, 1, 2, 4)


def _nsa_swa_layer(h, norm_g, w_in, gate_bias, pos_k, pos_v, ck_w1, ck_w2, cv_w1, cv_w2, sinks, w_out):
    b, t, d = h.shape
    m = b * t
    gk = NSA_KV_HEADS * HEAD_DIM
    nq_a, nq_b = NSA_HEADS * HEAD_DIM, SWA_HEADS * HEAD_DIM
    ngate = NSA_HEADS * 3
    o_kc, o_vc, o_ks, o_vs, o_kw, o_vw = (nq_a + i * gk for i in range(6))
    o_g = nq_a + 6 * gk
    o_qb = o_g + ngate
    o_kb, o_vb = o_qb + nq_b, o_qb + nq_b + gk
    cols = lambda o, n: w_in[:, o:o + n]
    w_q = jnp.concatenate([cols(0, nq_a), cols(o_qb, nq_b)], axis=1).astype(BF16)
    w_kv = jnp.concatenate([cols(o_ks, gk), cols(o_vs, gk), cols(o_kw, gk), cols(o_vw, gk),
                            cols(o_kb, gk), cols(o_vb, gk)], axis=1).astype(BF16)
    w_f32 = jnp.concatenate([cols(o_kc, gk), cols(o_vc, gk), cols(o_g, ngate),
                             jnp.zeros((d, LANES - ngate), w_in.dtype)], axis=1).astype(BF16)
    q_all, kv_all, misc = _norm_proj(h.reshape(m, d), norm_g, [w_q, w_kv, w_f32], [BF16, BF16, F32])
    q_all = q_all.reshape(b, t, nq_a + nq_b)
    kv_all = kv_all.reshape(b, t, 6 * gk)
    misc = misc.reshape(b, t, -1)

    rows = t // CMP_STRIDE
    flat = CMP_STRIDE * HEAD_DIM
    raw = misc[..., :2 * gk].reshape(b, rows, CMP_STRIDE, 2, NSA_KV_HEADS, HEAD_DIM)
    raw = raw.transpose(3, 0, 4, 1, 2, 5).reshape(2, b, NSA_KV_HEADS, rows, flat)
    pos = jnp.stack([pos_k, pos_v]).reshape(2, 2, 1, flat)
    w1 = jnp.stack([ck_w1, cv_w1]).astype(BF16)
    w2 = jnp.stack([ck_w2, cv_w2]).astype(BF16)
    cmp_kv = _compress(raw, pos, w1, w2)
    n_sel = t // SEL_LEN
    ns_pad = -(-n_sel // LANES) * LANES
    kct = cmp_kv[0].transpose(0, 1, 3, 2)
    sel_map = jnp.asarray(_selection_map(t, rows, ns_pad), BF16)
    vcs = jnp.concatenate([cmp_kv[1], jnp.broadcast_to(sel_map, (b, NSA_KV_HEADS) + sel_map.shape)], axis=-1)

    part = lambda i: kv_all[..., i * gk:(i + 1) * gk]
    tk = min(SEL_KEY_TILE, t)
    gl = misc[..., 2 * gk:2 * gk + ngate].reshape(b, t, NSA_KV_HEADS, NSA_REP * 3).transpose(0, 2, 1, 3)
    gb = gate_bias.reshape(NSA_KV_HEADS, 1, NSA_REP * 3)
    expand = jnp.asarray(_block_expansion(t, ns_pad, tk), BF16)
    o_a = _nsa_attention(q_all, gl, gb, kct, vcs, _key_tiles(part(0), tk), _value_tiles(part(1), tk),
                         _key_tiles(part(2), Q_BLOCK), _value_tiles(part(3), Q_BLOCK), expand, n_sel)

    o_b = _swa_attention(q_all, _key_tiles(part(4), Q_BLOCK), _value_tiles(part(5), Q_BLOCK), sinks,
                         nq_a // (SWA_REP * HEAD_DIM))

    w_o = w_out.astype(BF16)
    out = _out_proj(h.reshape(m, d), [o_a.reshape(m, nq_a), o_b.reshape(m, nq_b)], [w_o[:nq_a], w_o[nq_a:]])
    return out.reshape(b, t, d)


def _mlstm_layer(h, norm_g, w_in, conv_w, conv_b, igate_bias, fgate_bias, w_out):
    b, t, d = h.shape
    m = b * t
    nqk = 2 * MLSTM_HEADS * MLSTM_QK_DIM
    nv = MLSTM_HEADS * MLSTM_V_DIM
    w = w_in.astype(BF16)
    w_gates = jnp.concatenate([w[:, nqk + 2 * nv:], jnp.zeros((d, LANES - 2 * MLSTM_HEADS), BF16)], axis=1)
    qk_raw, v, og, gates = _norm_proj(
        h.reshape(m, d), norm_g,
        [w[:, :nqk], w[:, nqk:nqk + nv], w[:, nqk + nv:nqk + 2 * nv], w_gates],
        [F32, BF16, F32, F32])
    qk = _conv_silu(qk_raw.reshape(b, t, nqk), conv_w, conv_b)
    gates = gates.reshape(b, t, -1)
    chunked = lambda a: a.transpose(0, 2, 1).reshape(b, MLSTM_HEADS, t // MLSTM_CHUNK, MLSTM_CHUNK)
    ig = chunked(gates[..., :MLSTM_HEADS])
    fg = chunked(gates[..., MLSTM_HEADS:2 * MLSTM_HEADS])
    hh = _mlstm(qk, v.reshape(b, t, nv), ig, fg, jnp.stack([igate_bias, fgate_bias]))
    out = _out_proj(h.reshape(m, d), [hh.reshape(m, nv)], [w_out.astype(BF16)], gate=og)
    return out.reshape(b, t, d)


def kernel(x, norm_mix, norm_ffn, ffn_w_gate, ffn_w_up, ffn_w_down, ab_w_in, ab_gate_bias, nsa_pos_k, nsa_pos_v, nsa_cmp_k_w1, nsa_cmp_k_w2, nsa_cmp_v_w1, nsa_cmp_v_w2, swa_sinks, ab_w_out, c_w_in, c_conv_w, c_conv_b, c_igate_bias, c_fgate_bias, c_w_out, final_norm):
    depth = norm_mix.shape[0]
    b, t, d = x.shape
    h = x
    for layer in range(depth):
        j = layer // 2
        if layer % 2 == 0:
            h = _nsa_swa_layer(h, norm_mix[layer], ab_w_in[j], ab_gate_bias[j], nsa_pos_k[j], nsa_pos_v[j],
                               nsa_cmp_k_w1[j], nsa_cmp_k_w2[j], nsa_cmp_v_w1[j], nsa_cmp_v_w2[j],
                               swa_sinks[j], ab_w_out[j])
        else:
            h = _mlstm_layer(h, norm_mix[layer], c_w_in[j], c_conv_w[j], c_conv_b[j], c_igate_bias[j],
                             c_fgate_bias[j], c_w_out[j])
        last = layer == depth - 1
        h = _ffn(h.reshape(b * t, d), norm_ffn[layer], ffn_w_gate[layer].astype(BF16),
                 ffn_w_up[layer].astype(BF16), ffn_w_down[layer].astype(BF16),
                 final_norm if last else None).reshape(b, t, d)
    if depth == 0:
        raise ValueError("depth must be positive")
    return h
```

```python
import functools

import numpy as np
import jax
import jax.numpy as jnp
from jax import lax
from jax.experimental import pallas as pl
from jax.experimental.pallas import tpu as pltpu

HEAD_DIM = 64
NSA_HEADS = 8
NSA_KV_HEADS = 2
NSA_REP = NSA_HEADS // NSA_KV_HEADS
CMP_LEN = 32
CMP_STRIDE = 16
SEL_LEN = 64
SEL_TOPK = 16
NSA_WINDOW = 512
SWA_HEADS = 8
SWA_KV_HEADS = 2
SWA_REP = SWA_HEADS // SWA_KV_HEADS
SWA_WINDOW = 128
Q_BLOCK = 128
MLSTM_HEADS = 4
MLSTM_QK_DIM = 128
MLSTM_V_DIM = 256
MLSTM_CHUNK = 64
CONV_WIDTH = 4
RMS_EPS = 1e-6
NEG_INF = -1e30
FORCE_SCORE = 1e9
BELOW_NEG_INF = -3e38
SEL_SHIFT = SEL_LEN.bit_length() - 1
assert 1 << SEL_SHIFT == SEL_LEN

LANES = 128
SUBLANES = 8
VMEM_LIMIT_BYTES = 56 * 1024 * 1024

ROW_TILE = 512
SEL_KEY_TILE = 512
SEL_ROW_SPLIT = 512
MLSTM_CHUNKS_PER_STEP = 8

F32 = jnp.float32
BF16 = jnp.bfloat16


def _cparams(n_axes):
    return pltpu.CompilerParams(
        dimension_semantics=("arbitrary",) * n_axes,
        vmem_limit_bytes=VMEM_LIMIT_BYTES,
    )


def _resident(shape):
    nd = len(shape)
    return pl.BlockSpec(shape, lambda *_: (0,) * nd)


def _dot(a, b):
    return jnp.dot(a, b, preferred_element_type=F32)


def _sigmoid(x):
    return 1.0 / (1.0 + jnp.exp(-x))


def _silu(x):
    return x * _sigmoid(x)


def _rmsnorm(x, g):
    y = x * lax.rsqrt(jnp.mean(x * x, axis=-1, keepdims=True) + RMS_EPS)
    return y * g


def _norm_proj_kernel(n_out, x_ref, g_ref, *refs):
    w_refs, o_refs = refs[:n_out], refs[n_out:]
    xn = _rmsnorm(x_ref[...], g_ref[...]).astype(BF16)
    for w_ref, o_ref in zip(w_refs, o_refs):
        o_ref[...] = _dot(xn, w_ref[...]).astype(o_ref.dtype)


def _norm_proj(x, g, ws, out_dtypes):
    m, d = x.shape
    n_out = len(ws)
    return pl.pallas_call(
        functools.partial(_norm_proj_kernel, n_out),
        grid=(m // ROW_TILE,),
        in_specs=[pl.BlockSpec((ROW_TILE, d), lambda i: (i, 0)), _resident((1, d))]
        + [_resident(w.shape) for w in ws],
        out_specs=[pl.BlockSpec((ROW_TILE, w.shape[1]), lambda i: (i, 0)) for w in ws],
        out_shape=[jax.ShapeDtypeStruct((m, w.shape[1]), dt) for w, dt in zip(ws, out_dtypes)],
        compiler_params=_cparams(1),
        name="norm_proj",
    )(x, g.reshape(1, d), *ws)


def _out_proj_kernel(n_in, gated, res_ref, *refs):
    a_refs = refs[:n_in]
    w_refs = refs[n_in:2 * n_in]
    rest = refs[2 * n_in:]
    o_ref = rest[-1]
    acc = res_ref[...]
    for idx, (a_ref, w_ref) in enumerate(zip(a_refs, w_refs)):
        a = a_ref[...]
        if gated and idx == 0:
            a = (a.astype(F32) * _sigmoid(rest[0][...])).astype(BF16)
        acc = acc + _dot(a.astype(BF16), w_ref[...])
    o_ref[...] = acc


def _out_proj(res, a_list, w_list, gate=None):
    m, d = res.shape
    n_in = len(a_list)
    row = lambda w: pl.BlockSpec((ROW_TILE, w), lambda i: (i, 0))
    in_specs = [row(d)] + [row(a.shape[1]) for a in a_list] + [_resident(w.shape) for w in w_list]
    args = [res, *a_list, *w_list]
    if gate is not None:
        in_specs.append(row(gate.shape[1]))
        args.append(gate)
    return pl.pallas_call(
        functools.partial(_out_proj_kernel, n_in, gate is not None),
        grid=(m // ROW_TILE,),
        in_specs=in_specs,
        out_specs=row(d),
        out_shape=jax.ShapeDtypeStruct((m, d), F32),
        compiler_params=_cparams(1),
        name="out_proj",
    )(*args)


def _ffn_kernel(n_chunks, chunk, final, x_ref, g_ref, wg_ref, wu_ref, wd_ref, *rest):
    o_ref = rest[-1]
    x = x_ref[...]
    xn = _rmsnorm(x, g_ref[...]).astype(BF16)
    acc = x
    for c in range(n_chunks):
        cols = slice(c * chunk, (c + 1) * chunk)
        gate = _dot(xn, wg_ref[:, cols])
        up = _dot(xn, wu_ref[:, cols])
        h = (_silu(gate) * up).astype(BF16)
        acc = acc + _dot(h, wd_ref[cols, :])
    if final:
        acc = _rmsnorm(acc, rest[0][...])
    o_ref[...] = acc


def _ffn_chunk(hidden):
    return hidden


def _ffn(x, g, wg, wu, wd, final_g=None):
    m, d = x.shape
    hidden = wg.shape[1]
    chunk = _ffn_chunk(hidden)
    row = pl.BlockSpec((ROW_TILE, d), lambda i: (i, 0))
    in_specs = [row, _resident((1, d)), _resident(wg.shape), _resident(wu.shape), _resident(wd.shape)]
    args = [x, g.reshape(1, d), wg, wu, wd]
    if final_g is not None:
        in_specs.append(_resident((1, d)))
        args.append(final_g.reshape(1, d))
    return pl.pallas_call(
        functools.partial(_ffn_kernel, hidden // chunk, chunk, final_g is not None),
        grid=(m // ROW_TILE,),
        in_specs=in_specs,
        out_specs=row,
        out_shape=jax.ShapeDtypeStruct((m, d), F32),
        compiler_params=_cparams(1),
        name="ffn",
    )(*args)


def _compress_kernel(x_ref, p_ref, w1_ref, w2_ref, o_ref):
    x = x_ref[0, 0, 0]
    rows = x.shape[0]
    half = x.shape[1]
    a = _dot((x + p_ref[0, 0]).astype(BF16), w1_ref[0, :half, :])
    b = _dot((x + p_ref[0, 1]).astype(BF16), w1_ref[0, half:, :])
    h = a + pltpu.roll(b, rows - 1, 0)
    out = _dot(_silu(h).astype(BF16), w2_ref[0])
    live = lax.broadcasted_iota(jnp.int32, out.shape, 0) < rows - 1
    o_ref[0, 0, 0] = jnp.where(live, out, 0.0).astype(o_ref.dtype)


def _compress(x, pos, w1, w2):
    two, b, g, rows, width = x.shape
    return pl.pallas_call(
        _compress_kernel,
        grid=(two, b, g),
        in_specs=[
            pl.BlockSpec((1, 1, 1, rows, width), lambda s, i, j: (s, i, j, 0, 0)),
            pl.BlockSpec((1, 2, 1, width), lambda s, i, j: (s, 0, 0, 0)),
            pl.BlockSpec((1,) + w1.shape[1:], lambda s, i, j: (s, 0, 0)),
            pl.BlockSpec((1,) + w2.shape[1:], lambda s, i, j: (s, 0, 0)),
        ],
        out_specs=pl.BlockSpec((1, 1, 1, rows, HEAD_DIM), lambda s, i, j: (s, i, j, 0, 0)),
        out_shape=jax.ShapeDtypeStruct((two, b, g, rows, HEAD_DIM), BF16),
        compiler_params=_cparams(3),
        name="compress_kv",
    )(x, pos, w1, w2)


def _stack_heads(q_ref, rep):
    q = q_ref[0].astype(F32) * (HEAD_DIM ** -0.5)
    return jnp.concatenate([q[:, r * HEAD_DIM:(r + 1) * HEAD_DIM] for r in range(rep)], axis=0).astype(BF16)


def _per_head(x, rep):
    return x.reshape(rep, x.shape[0] // rep, x.shape[1])


def _banded_softmax_pv(qs, k_tiles, v_tiles, first_pos, tq, window, rep, sinks=None):
    kt = jnp.concatenate(k_tiles, axis=1)
    v = jnp.concatenate(v_tiles, axis=0)
    s = _per_head(_dot(qs, kt), rep)
    sp = first_pos + lax.broadcasted_iota(jnp.int32, (tq.shape[0], kt.shape[1]), 1)
    allowed = (sp >= 0) & (sp <= tq) & (tq - sp < window)
    s = jnp.where(allowed[None], s, NEG_INF)
    mx = jnp.max(s, axis=-1, keepdims=True)
    sink = None
    if sinks is not None:
        sink = jnp.concatenate([jnp.full((1, tq.shape[0], 1), sk, F32) for sk in sinks], axis=0)
        mx = jnp.maximum(mx, sink)
    e = jnp.exp(s - mx)
    den = jnp.sum(e, axis=-1, keepdims=True)
    if sink is not None:
        den = den + jnp.exp(sink - mx)
    p = (e * (1.0 / den)).astype(BF16)
    return _dot(p.reshape(qs.shape[0], kt.shape[1]), v)


def _attn_kernel(n_sel_blocks, sink_ref, qa_ref, qb_ref, gl_ref, gb_ref, kct_ref, vcs_ref, ksd_ref, ksa_ref,
                 vsa_ref, kwt_ref, vw_ref, kbp_ref, kbc_ref, vbp_ref, vbc_ref, oa_ref, ob_ref, m_sc, acc_sc):
    g, n = pl.program_id(1), pl.program_id(2)
    rep = NSA_REP
    rows = rep * Q_BLOCK
    tq = n * Q_BLOCK + lax.broadcasted_iota(jnp.int32, (Q_BLOCK, 1), 0)

    qs_b = _stack_heads(qb_ref, SWA_REP)
    sinks = [sink_ref[g * SWA_REP + r] for r in range(SWA_REP)]
    o_b = _banded_softmax_pv(qs_b, [kbp_ref[0, 0, 0], kbc_ref[0, 0, 0]], [vbp_ref[0, 0, 0], vbc_ref[0, 0, 0]],
                             (n - 1) * Q_BLOCK, tq, SWA_WINDOW, SWA_REP, sinks)
    for r in range(SWA_REP):
        ob_ref[0, :, r * HEAD_DIM:(r + 1) * HEAD_DIM] = o_b[r * Q_BLOCK:(r + 1) * Q_BLOCK].astype(ob_ref.dtype)

    qs = _stack_heads(qa_ref, rep)

    kct = kct_ref[0, 0]
    n_cmp = kct.shape[1]
    s = _per_head(_dot(qs, kct), rep)
    cmp_end = lax.broadcasted_iota(jnp.int32, (Q_BLOCK, n_cmp), 1) * CMP_STRIDE + (CMP_LEN - 1)
    vis = (cmp_end <= tq)[None]
    s = jnp.where(vis, s, NEG_INF)
    e = jnp.where(vis, jnp.exp(s - jnp.max(s, axis=-1, keepdims=True)), 0.0)
    den = jnp.sum(e, axis=-1, keepdims=True)
    p = (e * (1.0 / jnp.where(den > 0.0, den, 1.0))).astype(BF16)
    oc_imp = _dot(p.reshape(rows, n_cmp), vcs_ref[0, 0])
    o_c = oc_imp[:, :HEAD_DIM]
    imp = jnp.sum(_per_head(oc_imp[:, HEAD_DIM:], rep), axis=0)

    n_win_tiles = NSA_WINDOW // Q_BLOCK + 1
    tiles = [jnp.maximum(n - (n_win_tiles - 1) + i, 0) for i in range(n_win_tiles)]
    o_w = _banded_softmax_pv(qs, [kwt_ref[0, 0, j] for j in tiles], [vw_ref[0, 0, j] for j in tiles],
                             (n - (n_win_tiles - 1)) * Q_BLOCK, tq, NSA_WINDOW, rep)

    ns_pad = imp.shape[1]
    imp_t = imp.T
    blk = lax.broadcasted_iota(jnp.int32, (ns_pad, Q_BLOCK), 0)
    blk_f = blk.astype(F32)
    tq_t = n * Q_BLOCK + lax.broadcasted_iota(jnp.int32, (1, Q_BLOCK), 1)
    cur = jnp.right_shift(tq_t, SEL_SHIFT)
    forced = (blk == 0) | (blk == cur) | (blk == cur - 1)
    valid = jnp.left_shift(blk, SEL_SHIFT) <= tq_t
    score = jnp.where(valid, imp_t, NEG_INF)
    score = jnp.where(forced | (blk >= n_sel_blocks), BELOW_NEG_INF, score)
    sel = forced
    for _ in range(min(SEL_TOPK, n_sel_blocks) - 3):
        top = jnp.max(score, axis=0, keepdims=True)
        first = jnp.min(jnp.where(score == top, blk_f, float(ns_pad)), axis=0, keepdims=True)
        pick = blk_f == first
        sel = sel | pick
        score = jnp.where(pick, BELOW_NEG_INF, score)
    n_diag_blk = n * (Q_BLOCK // SEL_LEN)
    unselected = jnp.where(sel & (blk < n_diag_blk), 0.0, 1.0).T.astype(BF16)

    s = _per_head(_dot(qs, ksd_ref[0, 0, 0]), rep)
    tri = (lax.broadcasted_iota(jnp.int32, (Q_BLOCK, Q_BLOCK), 1)
           <= lax.broadcasted_iota(jnp.int32, (Q_BLOCK, Q_BLOCK), 0))[None]
    s = jnp.where(tri, s, NEG_INF).reshape(rows, Q_BLOCK)
    m0 = jnp.max(s, axis=-1, keepdims=True)
    tk = ksa_ref.shape[4]
    per_tile = tk // Q_BLOCK
    kt_diag = lax.div(n, per_tile)
    off = pl.multiple_of((n - kt_diag * per_tile) * Q_BLOCK, Q_BLOCK)
    v_diag = vsa_ref[0, 0, kt_diag, pl.ds(off, Q_BLOCK), :]
    m_sc[...] = m0
    acc_sc[...] = _dot(jnp.exp(s - m0).astype(BF16), v_diag)

    q_aug = jnp.concatenate([jnp.concatenate([unselected] * rep, axis=0), qs], axis=1)

    q_parts = [q_aug[i * SEL_ROW_SPLIT:(i + 1) * SEL_ROW_SPLIT] for i in range(rows // SEL_ROW_SPLIT)]

    def key_tile(kt, carry):
        k_tile = ksa_ref[0, 0, kt]
        v_tile = vsa_ref[0, 0, kt]
        for i, q_part in enumerate(q_parts):
            rs = slice(i * SEL_ROW_SPLIT, (i + 1) * SEL_ROW_SPLIT)
            s = _dot(q_part, k_tile)
            m_old = m_sc[rs]
            m_new = jnp.maximum(m_old, jnp.max(s, axis=-1, keepdims=True))
            alpha = jnp.exp(m_old - m_new)
            pv = _dot(jnp.exp(s - m_new).astype(BF16), v_tile)
            acc_sc[rs] = alpha * acc_sc[rs] + pv
            m_sc[rs] = m_new
        return carry

    lax.fori_loop(0, lax.div(n * Q_BLOCK + tk - 1, tk), key_tile, 0)
    acc = acc_sc[...]
    o_s = acc[:, :HEAD_DIM] * (1.0 / acc[:, HEAD_DIM:HEAD_DIM + 1])

    gates = _sigmoid(gl_ref[0, 0] + gb_ref[0])
    for r in range(rep):
        rr = slice(r * Q_BLOCK, (r + 1) * Q_BLOCK)
        o = (gates[:, 3 * r:3 * r + 1] * o_c[rr] + gates[:, 3 * r + 1:3 * r + 2] * o_s[rr]
             + gates[:, 3 * r + 2:3 * r + 3] * o_w[rr])
        oa_ref[0, :, r * HEAD_DIM:(r + 1) * HEAD_DIM] = o.astype(oa_ref.dtype)


def _selection_map(t, nc_pad, ns_pad):
    nc = (t - CMP_LEN) // CMP_STRIDE + 1
    ns = t // SEL_LEN
    cs = np.arange(nc, dtype=np.int64)[:, None] * CMP_STRIDE
    ss = np.arange(ns, dtype=np.int64)[None, :] * SEL_LEN
    out = np.zeros((nc_pad, ns_pad), np.float32)
    out[:nc, :ns] = (cs < ss + SEL_LEN) & (cs + CMP_LEN > ss)
    return out


def _block_mask_rows(t, ns_pad, tk):
    key_block = (np.arange(t, dtype=np.int64) // SEL_LEN).reshape(t // tk, 1, tk)
    hit = np.arange(ns_pad, dtype=np.int64)[None, :, None] == key_block
    return np.where(hit, NEG_INF, 0.0).astype(np.float32)


def _attention(q, gate_logits, gate_bias, kct, vcs, ksd, ksa, vsa, kwt, vw, kbt, vb, sinks, n_sel_blocks):
    b, t = q.shape[:2]
    g = kct.shape[1]
    width = NSA_REP * HEAD_DIM
    ng = gate_logits.shape[-1]
    per_group = lambda a: pl.BlockSpec((1, 1) + a.shape[2:], lambda i, j, n: (i, j) + (0,) * (a.ndim - 2))
    prev = lambda i, j, n: (i, j, jnp.maximum(n - 1, 0), 0, 0)
    cur = lambda i, j, n: (i, j, n, 0, 0)
    k_tile = lambda idx: pl.BlockSpec((1, 1, 1, HEAD_DIM, Q_BLOCK), idx)
    v_tile = lambda idx: pl.BlockSpec((1, 1, 1, Q_BLOCK, HEAD_DIM), idx)
    out_spec = pl.BlockSpec((1, Q_BLOCK, width), lambda i, j, n: (i, n, j))
    out_shape = jax.ShapeDtypeStruct((b, t, g * width), BF16)
    return pl.pallas_call(
        functools.partial(_attn_kernel, n_sel_blocks),
        grid=(b, g, t // Q_BLOCK),
        in_specs=[
            pl.BlockSpec(memory_space=pltpu.SMEM),
            pl.BlockSpec((1, Q_BLOCK, width), lambda i, j, n: (i, n, j)),
            pl.BlockSpec((1, Q_BLOCK, width), lambda i, j, n: (i, n, g + j)),
            pl.BlockSpec((1, 1, Q_BLOCK, ng), lambda i, j, n: (i, j, n, 0)),
            pl.BlockSpec((1, 1, ng), lambda i, j, n: (j, 0, 0)),
            per_group(kct), per_group(vcs), k_tile(cur), per_group(ksa), per_group(vsa),
            per_group(kwt), per_group(vw),
            k_tile(prev), k_tile(cur), v_tile(prev), v_tile(cur),
        ],
        out_specs=[out_spec, out_spec],
        out_shape=[out_shape, out_shape],
        scratch_shapes=[
            pltpu.VMEM((NSA_REP * Q_BLOCK, 1), F32),
            pltpu.VMEM((NSA_REP * Q_BLOCK, 2 * HEAD_DIM), F32),
        ],
        compiler_params=_cparams(3),
        name="nsa_swa_attention",
    )(sinks, q, q, gate_logits, gate_bias, kct, vcs, ksd, ksa, vsa, kwt, vw, kbt, kbt, vb, vb)


def _conv_silu_kernel(n_q_cols, x_ref, prev_ref, w_ref, b_ref, o_ref, buf):
    i = pl.program_id(1)
    rows = x_ref.shape[1]
    halo = prev_ref.shape[1]
    buf[:halo, :] = jnp.where(i > 0, prev_ref[0], 0.0)
    buf[halo:, :] = x_ref[0]
    y = b_ref[...]
    for j in range(CONV_WIDTH):
        start = halo - (CONV_WIDTH - 1) + j
        y = y + w_ref[j:j + 1, :] * buf[start:start + rows, :]
    y = _silu(y)
    col = lax.broadcasted_iota(jnp.int32, y.shape, 1)
    o_ref[0] = jnp.where(col >= n_q_cols, y * (MLSTM_QK_DIM ** -0.5), y).astype(o_ref.dtype)


def _conv_silu(x, w, bias):
    b, t, c = x.shape
    halo = SUBLANES
    ratio = ROW_TILE // halo
    return pl.pallas_call(
        functools.partial(_conv_silu_kernel, c // 2),
        grid=(b, t // ROW_TILE),
        in_specs=[
            pl.BlockSpec((1, ROW_TILE, c), lambda i, j: (i, j, 0)),
            pl.BlockSpec((1, halo, c), lambda i, j: (i, jnp.maximum(j * ratio - 1, 0), 0)),
            _resident(w.shape),
            _resident((1, c)),
        ],
        out_specs=pl.BlockSpec((1, ROW_TILE, c), lambda i, j: (i, j, 0)),
        out_shape=jax.ShapeDtypeStruct((b, t, c), BF16),
        scratch_shapes=[pltpu.VMEM((ROW_TILE + halo, c), F32)],
        compiler_params=_cparams(2),
        name="conv_silu",
    )(x, x, w, bias.reshape(1, c))


def _mlstm_kernel(bias_ref, q_ref, k_ref, v_ref, ig_ref, fg_ref, o_ref, state, m_state):
    c = pl.program_id(1)
    L = MLSTM_CHUNK
    dqk, dv = MLSTM_QK_DIM, MLSTM_V_DIM

    @pl.when(c == 0)
    def _():
        state[...] = jnp.zeros(state.shape, F32)
        m_state[...] = jnp.zeros(m_state.shape, F32)

    t_idx = lax.broadcasted_iota(jnp.int32, (L, L), 0)
    s_idx = lax.broadcasted_iota(jnp.int32, (L, L), 1)
    causal = s_idx <= t_idx
    eye = s_idx == t_idx
    ones_col = (lax.broadcasted_iota(jnp.int32, (L, LANES), 1) == 0).astype(BF16)

    def to_col(row):
        return jnp.sum(jnp.where(eye, row, 0.0), axis=1, keepdims=True)

    for j in range(MLSTM_CHUNKS_PER_STEP):
        rows = slice(j * L, (j + 1) * L)
        for h in range(MLSTM_HEADS):
            qc = q_ref[0, rows, h * dqk:(h + 1) * dqk]
            kc = k_ref[0, rows, h * dqk:(h + 1) * dqk]
            va = jnp.concatenate([v_ref[0, rows, h * dv:(h + 1) * dv], ones_col], axis=1)
            i_row = ig_ref[0, h, j:j + 1, :] + bias_ref[0, h]
            f_pre = fg_ref[0, h, j:j + 1, :] + bias_ref[1, h]
            f_row = jnp.minimum(f_pre, 0.0) - jnp.log(1.0 + jnp.exp(-jnp.abs(f_pre)))
            f_col = to_col(f_row)
            b_col = jnp.sum(jnp.where(causal, f_row, 0.0), axis=1, keepdims=True)
            b_row = jnp.sum(jnp.where(t_idx <= s_idx, f_col, 0.0), axis=0, keepdims=True)
            m_old = m_state[h:h + 1, 0:1]
            a = b_col + m_old
            d = jnp.where(causal, b_col - b_row + i_row, -jnp.inf)
            mt = jnp.maximum(a, jnp.max(d, axis=1, keepdims=True))
            w_inter = jnp.exp(a - mt)
            sm = lax.dot_general(qc, kc, (((1,), (1,)), ((), ())), preferred_element_type=F32) * jnp.exp(d - mt)
            st = state[h]
            nd = w_inter * _dot(qc, st.astype(BF16)) + _dot(sm.astype(BF16), va)
            hc = nd[:, :dv] * (1.0 / jnp.maximum(jnp.abs(nd[:, dv:dv + 1]), jnp.exp(-mt)))
            o_ref[0, rows, h * dv:(h + 1) * dv] = hc.astype(o_ref.dtype)
            b_last = b_row[:, L - 1:L]
            wl = b_last - b_row + i_row
            m_new = jnp.maximum(b_last + m_old, jnp.max(wl, axis=1, keepdims=True))
            decay = jnp.exp(b_last + m_old - m_new)
            w_col = to_col(jnp.exp(wl - m_new))
            kw = (kc.astype(F32) * w_col).astype(BF16)
            upd = lax.dot_general(kw, va, (((0,), (0,)), ((), ())), preferred_element_type=F32)
            state[h] = decay * st + upd
            m_state[h:h + 1, 0:1] = m_new


def _mlstm(qk, v, ig, fg, gate_bias):
    b, t = qk.shape[:2]
    nh = MLSTM_HEADS
    rows = MLSTM_CHUNKS_PER_STEP * MLSTM_CHUNK
    gate_spec = pl.BlockSpec((1, nh, MLSTM_CHUNKS_PER_STEP, MLSTM_CHUNK), lambda i, c: (i, 0, c, 0))
    return pl.pallas_call(
        _mlstm_kernel,
        grid=(b, t // rows),
        in_specs=[
            pl.BlockSpec(memory_space=pltpu.SMEM),
            pl.BlockSpec((1, rows, nh * MLSTM_QK_DIM), lambda i, c: (i, c, 0)),
            pl.BlockSpec((1, rows, nh * MLSTM_QK_DIM), lambda i, c: (i, c, 1)),
            pl.BlockSpec((1, rows, nh * MLSTM_V_DIM), lambda i, c: (i, c, 0)),
            gate_spec, gate_spec,
        ],
        out_specs=pl.BlockSpec((1, rows, nh * MLSTM_V_DIM), lambda i, c: (i, c, 0)),
        out_shape=jax.ShapeDtypeStruct((b, t, nh * MLSTM_V_DIM), BF16),
        scratch_shapes=[
            pltpu.VMEM((nh, MLSTM_QK_DIM, MLSTM_V_DIM + LANES), F32),
            pltpu.VMEM((SUBLANES, LANES), F32),
        ],
        compiler_params=_cparams(2),
        name="mlstm_scan",
    )(gate_bias, qk, qk, v, ig, fg)


def _key_tiles(a, tk):
    b, t, _ = a.shape
    a = a.reshape(b, t // tk, tk, -1, HEAD_DIM)
    return a.transpose(0, 3, 1, 4, 2)


def _value_tiles(a, tk):
    b, t, _ = a.shape
    a = a.reshape(b, t // tk, tk, -1, HEAD_DIM)
    return a.transpose(0, 3, 1, 2, 4)


def _nsa_swa_layer(h, norm_g, w_in, gate_bias, pos_k, pos_v, ck_w1, ck_w2, cv_w1, cv_w2, sinks, w_out):
    b, t, d = h.shape
    m = b * t
    gk = NSA_KV_HEADS * HEAD_DIM
    nq_a, nq_b = NSA_HEADS * HEAD_DIM, SWA_HEADS * HEAD_DIM
    ngate = NSA_HEADS * 3
    o_kc, o_vc, o_ks, o_vs, o_kw, o_vw = (nq_a + i * gk for i in range(6))
    o_g = nq_a + 6 * gk
    o_qb = o_g + ngate
    o_kb, o_vb = o_qb + nq_b, o_qb + nq_b + gk
    cols = lambda o, n: w_in[:, o:o + n]
    w_q = jnp.concatenate([cols(0, nq_a), cols(o_qb, nq_b)], axis=1).astype(BF16)
    w_kv = jnp.concatenate([cols(o_ks, gk), cols(o_vs, gk), cols(o_kw, gk), cols(o_vw, gk),
                            cols(o_kb, gk), cols(o_vb, gk)], axis=1).astype(BF16)
    w_f32 = jnp.concatenate([cols(o_kc, gk), cols(o_vc, gk), cols(o_g, ngate),
                             jnp.zeros((d, LANES - ngate), w_in.dtype)], axis=1).astype(BF16)
    q_all, kv_all, misc = _norm_proj(h.reshape(m, d), norm_g, [w_q, w_kv, w_f32], [BF16, BF16, F32])
    q_all = q_all.reshape(b, t, nq_a + nq_b)
    kv_all = kv_all.reshape(b, t, 6 * gk)
    misc = misc.reshape(b, t, -1)

    rows = t // CMP_STRIDE
    flat = CMP_STRIDE * HEAD_DIM
    raw = misc[..., :2 * gk].reshape(b, rows, CMP_STRIDE, 2, NSA_KV_HEADS, HEAD_DIM)
    raw = raw.transpose(3, 0, 4, 1, 2, 5).reshape(2, b, NSA_KV_HEADS, rows, flat)
    pos = jnp.stack([pos_k, pos_v]).reshape(2, 2, 1, flat)
    w1 = jnp.stack([ck_w1, cv_w1]).astype(BF16)
    w2 = jnp.stack([ck_w2, cv_w2]).astype(BF16)
    cmp_kv = _compress(raw, pos, w1, w2)
    n_sel = t // SEL_LEN
    ns_pad = -(-n_sel // LANES) * LANES
    kct = cmp_kv[0].transpose(0, 1, 3, 2)
    sel_map = jnp.asarray(_selection_map(t, rows, ns_pad), BF16)
    vcs = jnp.concatenate([cmp_kv[1], jnp.broadcast_to(sel_map, (b, NSA_KV_HEADS) + sel_map.shape)], axis=-1)

    part = lambda i: kv_all[..., i * gk:(i + 1) * gk]
    tk = min(SEL_KEY_TILE, t)
    gl = misc[..., 2 * gk:2 * gk + ngate].reshape(b, t, NSA_KV_HEADS, NSA_REP * 3).transpose(0, 2, 1, 3)
    gb = gate_bias.reshape(NSA_KV_HEADS, 1, NSA_REP * 3)
    mask_rows = jnp.asarray(_block_mask_rows(t, ns_pad, tk), BF16)
    kst = _key_tiles(part(0), tk)
    ksa = jnp.concatenate([jnp.broadcast_to(mask_rows, kst.shape[:2] + mask_rows.shape), kst], axis=3)
    vst = _value_tiles(part(1), tk)
    ones = jnp.zeros(vst.shape, BF16).at[..., 0].set(1.0)
    vsa = jnp.concatenate([vst, ones], axis=-1)
    o_a, o_b = _attention(q_all, gl, gb, kct, vcs, _key_tiles(part(0), Q_BLOCK), ksa, vsa,
                          _key_tiles(part(2), Q_BLOCK), _value_tiles(part(3), Q_BLOCK),
                          _key_tiles(part(4), Q_BLOCK), _value_tiles(part(5), Q_BLOCK), sinks, n_sel)

    w_o = w_out.astype(BF16)
    out = _out_proj(h.reshape(m, d), [o_a.reshape(m, nq_a), o_b.reshape(m, nq_b)], [w_o[:nq_a], w_o[nq_a:]])
    return out.reshape(b, t, d)


def _mlstm_layer(h, norm_g, w_in, conv_w, conv_b, igate_bias, fgate_bias, w_out):
    b, t, d = h.shape
    m = b * t
    nqk = 2 * MLSTM_HEADS * MLSTM_QK_DIM
    nv = MLSTM_HEADS * MLSTM_V_DIM
    w = w_in.astype(BF16)
    w_gates = jnp.concatenate([w[:, nqk + 2 * nv:], jnp.zeros((d, LANES - 2 * MLSTM_HEADS), BF16)], axis=1)
    qk_raw, v, og, gates = _norm_proj(
        h.reshape(m, d), norm_g,
        [w[:, :nqk], w[:, nqk:nqk + nv], w[:, nqk + nv:nqk + 2 * nv], w_gates],
        [F32, BF16, F32, F32])
    qk = _conv_silu(qk_raw.reshape(b, t, nqk), conv_w, conv_b)
    gates = gates.reshape(b, t, -1)
    chunked = lambda a: a.transpose(0, 2, 1).reshape(b, MLSTM_HEADS, t // MLSTM_CHUNK, MLSTM_CHUNK)
    ig = chunked(gates[..., :MLSTM_HEADS])
    fg = chunked(gates[..., MLSTM_HEADS:2 * MLSTM_HEADS])
    hh = _mlstm(qk, v.reshape(b, t, nv), ig, fg, jnp.stack([igate_bias, fgate_bias]))
    out = _out_proj(h.reshape(m, d), [hh.reshape(m, nv)], [w_out.astype(BF16)], gate=og)
    return out.reshape(b, t, d)


def kernel(x, norm_mix, norm_ffn, ffn_w_gate, ffn_w_up, ffn_w_down, ab_w_in, ab_gate_bias, nsa_pos_k, nsa_pos_v, nsa_cmp_k_w1, nsa_cmp_k_w2, nsa_cmp_v_w1, nsa_cmp_v_w2, swa_sinks, ab_w_out, c_w_in, c_conv_w, c_conv_b, c_igate_bias, c_fgate_bias, c_w_out, final_norm):
    depth = norm_mix.shape[0]
    b, t, d = x.shape
    h = x
    for layer in range(depth):
        j = layer // 2
        if layer % 2 == 0:
            h = _nsa_swa_layer(h, norm_mix[layer], ab_w_in[j], ab_gate_bias[j], nsa_pos_k[j], nsa_pos_v[j],
                               nsa_cmp_k_w1[j], nsa_cmp_k_w2[j], nsa_cmp_v_w1[j], nsa_cmp_v_w2[j],
                               swa_sinks[j], ab_w_out[j])
        else:
            h = _mlstm_layer(h, norm_mix[layer], c_w_in[j], c_conv_w[j], c_conv_b[j], c_igate_bias[j],
                             c_fgate_bias[j], c_w_out[j])
        last = layer == depth - 1
        h = _ffn(h.reshape(b * t, d), norm_ffn[layer], ffn_w_gate[layer].astype(BF16),
                 ffn_w_up[layer].astype(BF16), ffn_w_down[layer].astype(BF16),
                 final_norm if last else None).reshape(b, t, d)
    if depth == 0:
        raise ValueError("depth must be positive")
    return h
```

```python
import functools

import numpy as np
import jax
import jax.numpy as jnp
from jax import lax
from jax.experimental import pallas as pl
from jax.experimental.pallas import tpu as pltpu

HEAD_DIM = 64
NSA_HEADS = 8
NSA_KV_HEADS = 2
NSA_REP = NSA_HEADS // NSA_KV_HEADS
CMP_LEN = 32
CMP_STRIDE = 16
SEL_LEN = 64
SEL_TOPK = 16
NSA_WINDOW = 512
SWA_HEADS = 8
SWA_KV_HEADS = 2
SWA_REP = SWA_HEADS // SWA_KV_HEADS
SWA_WINDOW = 128
Q_BLOCK = 128
MLSTM_HEADS = 4
MLSTM_QK_DIM = 128
MLSTM_V_DIM = 256
MLSTM_CHUNK = 64
CONV_WIDTH = 4
RMS_EPS = 1e-6
NEG_INF = -1e30
FORCE_SCORE = 1e9
BELOW_NEG_INF = -3e38
SEL_SHIFT = SEL_LEN.bit_length() - 1
assert 1 << SEL_SHIFT == SEL_LEN

LANES = 128
SUBLANES = 8
VMEM_LIMIT_BYTES = 56 * 1024 * 1024

ROW_TILE = 512
SEL_KEY_TILE = 512
MLSTM_CHUNKS_PER_STEP = 8

F32 = jnp.float32
BF16 = jnp.bfloat16


def _cparams(n_axes):
    return pltpu.CompilerParams(
        dimension_semantics=("arbitrary",) * n_axes,
        vmem_limit_bytes=VMEM_LIMIT_BYTES,
    )


def _resident(shape):
    nd = len(shape)
    return pl.BlockSpec(shape, lambda *_: (0,) * nd)


def _dot(a, b):
    return jnp.dot(a, b, preferred_element_type=F32)


def _sigmoid(x):
    return 1.0 / (1.0 + jnp.exp(-x))


def _silu(x):
    return x * _sigmoid(x)


def _rmsnorm(x, g):
    y = x * lax.rsqrt(jnp.mean(x * x, axis=-1, keepdims=True) + RMS_EPS)
    return y * g


def _norm_proj_kernel(n_out, x_ref, g_ref, *refs):
    w_refs, o_refs = refs[:n_out], refs[n_out:]
    xn = _rmsnorm(x_ref[...], g_ref[...]).astype(BF16)
    for w_ref, o_ref in zip(w_refs, o_refs):
        o_ref[...] = _dot(xn, w_ref[...]).astype(o_ref.dtype)


def _norm_proj(x, g, ws, out_dtypes):
    m, d = x.shape
    n_out = len(ws)
    return pl.pallas_call(
        functools.partial(_norm_proj_kernel, n_out),
        grid=(m // ROW_TILE,),
        in_specs=[pl.BlockSpec((ROW_TILE, d), lambda i: (i, 0)), _resident((1, d))]
        + [_resident(w.shape) for w in ws],
        out_specs=[pl.BlockSpec((ROW_TILE, w.shape[1]), lambda i: (i, 0)) for w in ws],
        out_shape=[jax.ShapeDtypeStruct((m, w.shape[1]), dt) for w, dt in zip(ws, out_dtypes)],
        compiler_params=_cparams(1),
        name="norm_proj",
    )(x, g.reshape(1, d), *ws)


def _out_proj_kernel(n_in, gated, res_ref, *refs):
    a_refs = refs[:n_in]
    w_refs = refs[n_in:2 * n_in]
    rest = refs[2 * n_in:]
    o_ref = rest[-1]
    acc = res_ref[...]
    for idx, (a_ref, w_ref) in enumerate(zip(a_refs, w_refs)):
        a = a_ref[...]
        if gated and idx == 0:
            a = (a.astype(F32) * _sigmoid(rest[0][...])).astype(BF16)
        acc = acc + _dot(a.astype(BF16), w_ref[...])
    o_ref[...] = acc


def _out_proj(res, a_list, w_list, gate=None):
    m, d = res.shape
    n_in = len(a_list)
    row = lambda w: pl.BlockSpec((ROW_TILE, w), lambda i: (i, 0))
    in_specs = [row(d)] + [row(a.shape[1]) for a in a_list] + [_resident(w.shape) for w in w_list]
    args = [res, *a_list, *w_list]
    if gate is not None:
        in_specs.append(row(gate.shape[1]))
        args.append(gate)
    return pl.pallas_call(
        functools.partial(_out_proj_kernel, n_in, gate is not None),
        grid=(m // ROW_TILE,),
        in_specs=in_specs,
        out_specs=row(d),
        out_shape=jax.ShapeDtypeStruct((m, d), F32),
        compiler_params=_cparams(1),
        name="out_proj",
    )(*args)


def _ffn_kernel(n_chunks, chunk, final, x_ref, g_ref, wg_ref, wu_ref, wd_ref, *rest):
    o_ref = rest[-1]
    x = x_ref[...]
    xn = _rmsnorm(x, g_ref[...]).astype(BF16)
    acc = x
    for c in range(n_chunks):
        cols = slice(c * chunk, (c + 1) * chunk)
        gate = _dot(xn, wg_ref[:, cols])
        up = _dot(xn, wu_ref[:, cols])
        h = (_silu(gate) * up).astype(BF16)
        acc = acc + _dot(h, wd_ref[cols, :])
    if final:
        acc = _rmsnorm(acc, rest[0][...])
    o_ref[...] = acc


def _ffn_chunk(hidden):
    return hidden


def _ffn(x, g, wg, wu, wd, final_g=None):
    m, d = x.shape
    hidden = wg.shape[1]
    chunk = _ffn_chunk(hidden)
    row = pl.BlockSpec((ROW_TILE, d), lambda i: (i, 0))
    in_specs = [row, _resident((1, d)), _resident(wg.shape), _resident(wu.shape), _resident(wd.shape)]
    args = [x, g.reshape(1, d), wg, wu, wd]
    if final_g is not None:
        in_specs.append(_resident((1, d)))
        args.append(final_g.reshape(1, d))
    return pl.pallas_call(
        functools.partial(_ffn_kernel, hidden // chunk, chunk, final_g is not None),
        grid=(m // ROW_TILE,),
        in_specs=in_specs,
        out_specs=row,
        out_shape=jax.ShapeDtypeStruct((m, d), F32),
        compiler_params=_cparams(1),
        name="ffn",
    )(*args)


def _compress_kernel(x_ref, p_ref, w1_ref, w2_ref, o_ref):
    x = x_ref[0, 0, 0]
    rows = x.shape[0]
    half = x.shape[1]
    a = _dot((x + p_ref[0, 0]).astype(BF16), w1_ref[0, :half, :])
    b = _dot((x + p_ref[0, 1]).astype(BF16), w1_ref[0, half:, :])
    h = a + pltpu.roll(b, rows - 1, 0)
    out = _dot(_silu(h).astype(BF16), w2_ref[0])
    live = lax.broadcasted_iota(jnp.int32, out.shape, 0) < rows - 1
    o_ref[0, 0, 0] = jnp.where(live, out, 0.0).astype(o_ref.dtype)


def _compress(x, pos, w1, w2):
    two, b, g, rows, width = x.shape
    return pl.pallas_call(
        _compress_kernel,
        grid=(two, b, g),
        in_specs=[
            pl.BlockSpec((1, 1, 1, rows, width), lambda s, i, j: (s, i, j, 0, 0)),
            pl.BlockSpec((1, 2, 1, width), lambda s, i, j: (s, 0, 0, 0)),
            pl.BlockSpec((1,) + w1.shape[1:], lambda s, i, j: (s, 0, 0)),
            pl.BlockSpec((1,) + w2.shape[1:], lambda s, i, j: (s, 0, 0)),
        ],
        out_specs=pl.BlockSpec((1, 1, 1, rows, HEAD_DIM), lambda s, i, j: (s, i, j, 0, 0)),
        out_shape=jax.ShapeDtypeStruct((two, b, g, rows, HEAD_DIM), BF16),
        compiler_params=_cparams(3),
        name="compress_kv",
    )(x, pos, w1, w2)


def _stack_heads_t(q_ref, rep):
    qt = (q_ref[0].astype(F32) * (HEAD_DIM ** -0.5)).T
    return jnp.concatenate([qt[r * HEAD_DIM:(r + 1) * HEAD_DIM] for r in range(rep)], axis=1).astype(BF16)


def _unstack_heads_t(o_t, rep):
    tq = o_t.shape[1] // rep
    return jnp.concatenate([o_t[:, r * tq:(r + 1) * tq] for r in range(rep)], axis=0).T


def _tile_lanes(x, rep):
    return jnp.concatenate([x] * rep, axis=1)


def _softmax_pv_t(s, bias, v_aug_t, sink_row=None):
    rep = s.shape[1] // bias.shape[1]
    s = s + _tile_lanes(bias, rep)
    m = jnp.max(s, axis=0, keepdims=True)
    if sink_row is not None:
        m = jnp.maximum(m, sink_row)
    acc = _dot(v_aug_t, jnp.exp(s - m).astype(BF16))
    den = acc[HEAD_DIM:HEAD_DIM + 1]
    if sink_row is not None:
        den = den + jnp.exp(sink_row - m)
    return acc[:HEAD_DIM] * (1.0 / den)


def _window_bias(n_rows, first_pos, n, window):
    row = lax.broadcasted_iota(jnp.int32, (n_rows, Q_BLOCK), 0)
    col = lax.broadcasted_iota(jnp.int32, (n_rows, Q_BLOCK), 1)
    sp = first_pos + row
    dist = n * Q_BLOCK + col - sp
    return jnp.where((sp >= 0) & (dist >= 0) & (dist < window), 0.0, NEG_INF)


def _attn_kernel(n_sel_blocks, sink_ref, qa_ref, qb_ref, gl_ref, gb_ref, kc_ref, vcs_ref, ksd_ref, vsd_ref,
                 ksa_ref, vsa_ref, kw_ref, vw_ref, kbp_ref, kbc_ref, vbp_ref, vbc_ref, oa_ref, ob_ref,
                 m_sc, acc_sc, s_sc):
    g, n = pl.program_id(1), pl.program_id(2)
    rep = NSA_REP
    cols = rep * Q_BLOCK
    tq_row = n * Q_BLOCK + lax.broadcasted_iota(jnp.int32, (1, Q_BLOCK), 1)

    q_b = _stack_heads_t(qb_ref, SWA_REP)
    sink_row = jnp.concatenate([jnp.full((1, Q_BLOCK), sink_ref[g * SWA_REP + r], F32) for r in range(SWA_REP)],
                               axis=1)
    k_b = jnp.concatenate([kbp_ref[0, 0, 0], kbc_ref[0, 0, 0]], axis=0)
    v_b = jnp.concatenate([vbp_ref[0, 0, 0], vbc_ref[0, 0, 0]], axis=1)
    o_b = _softmax_pv_t(_dot(k_b, q_b), _window_bias(2 * Q_BLOCK, (n - 1) * Q_BLOCK, n, SWA_WINDOW), v_b, sink_row)
    ob_ref[0] = _unstack_heads_t(o_b, SWA_REP).astype(ob_ref.dtype)

    q_t = _stack_heads_t(qa_ref, rep)

    kc = kc_ref[0, 0]
    n_cmp = kc.shape[0]
    cmp_end = lax.broadcasted_iota(jnp.int32, (n_cmp, Q_BLOCK), 0) * CMP_STRIDE + (CMP_LEN - 1)
    s = _dot(kc, q_t) + _tile_lanes(jnp.where(cmp_end <= tq_row, 0.0, NEG_INF), rep)
    m = jnp.max(s, axis=0, keepdims=True)
    oc_imp = _dot(vcs_ref[0, 0], jnp.exp(s - m).astype(BF16))
    den = oc_imp[HEAD_DIM:HEAD_DIM + 1]
    seen = m > 0.5 * NEG_INF
    oc_imp = jnp.where(seen, oc_imp * (1.0 / jnp.where(seen, den, 1.0)), 0.0)
    o_c = oc_imp[:HEAD_DIM]
    imp = oc_imp[2 * HEAD_DIM:]
    imp = sum(imp[:, r * Q_BLOCK:(r + 1) * Q_BLOCK] for r in range(rep))

    n_win_tiles = NSA_WINDOW // Q_BLOCK + 1
    tiles = [jnp.maximum(n - (n_win_tiles - 1) + i, 0) for i in range(n_win_tiles)]
    k_w = jnp.concatenate([kw_ref[0, 0, j] for j in tiles], axis=0)
    v_w = jnp.concatenate([vw_ref[0, 0, j] for j in tiles], axis=1)
    first = (n - (n_win_tiles - 1)) * Q_BLOCK
    o_w = _softmax_pv_t(_dot(k_w, q_t), _window_bias(n_win_tiles * Q_BLOCK, first, n, NSA_WINDOW), v_w)

    ns_pad = imp.shape[0]
    blk = lax.broadcasted_iota(jnp.int32, (ns_pad, Q_BLOCK), 0)
    blk_f = blk.astype(F32)
    cur = jnp.right_shift(tq_row, SEL_SHIFT)
    forced = (blk == 0) | (blk == cur) | (blk == cur - 1)
    valid = jnp.left_shift(blk, SEL_SHIFT) <= tq_row
    score = jnp.where(valid, imp, NEG_INF)
    score = jnp.where(forced | (blk >= n_sel_blocks), BELOW_NEG_INF, score)
    sel = forced
    for _ in range(min(SEL_TOPK, n_sel_blocks) - 3):
        top = jnp.max(score, axis=0, keepdims=True)
        first_hit = jnp.min(jnp.where(score == top, blk_f, float(ns_pad)), axis=0, keepdims=True)
        pick = blk_f == first_hit
        sel = sel | pick
        score = jnp.where(pick, BELOW_NEG_INF, score)
    n_diag_blk = n * (Q_BLOCK // SEL_LEN)
    unselected = jnp.where(sel & (blk < n_diag_blk), 0.0, 1.0).astype(BF16)

    row = lax.broadcasted_iota(jnp.int32, (Q_BLOCK, Q_BLOCK), 0)
    col = lax.broadcasted_iota(jnp.int32, (Q_BLOCK, Q_BLOCK), 1)
    s = _dot(ksd_ref[0, 0, 0], q_t) + _tile_lanes(jnp.where(row <= col, 0.0, NEG_INF), rep)
    m0 = jnp.max(s, axis=0, keepdims=True)
    m_sc[...] = m0
    acc_sc[...] = _dot(vsd_ref[0, 0, 0], jnp.exp(s - m0).astype(BF16))

    q_aug = jnp.concatenate([_tile_lanes(unselected, rep), q_t], axis=0)
    tk = ksa_ref.shape[3]

    def absorb(kt, s):
        m_old = m_sc[...]
        m_new = jnp.maximum(m_old, jnp.max(s, axis=0, keepdims=True))
        pv = _dot(vsa_ref[0, 0, kt], jnp.exp(s - m_new).astype(BF16))
        acc_sc[...] = jnp.exp(m_old - m_new) * acc_sc[...] + pv
        m_sc[...] = m_new

    n_tiles = ksa_ref.shape[2]
    s_sc[0] = _dot(ksa_ref[0, 0, 0], q_aug)

    def tile_pair(j, carry):
        s_sc[1] = _dot(ksa_ref[0, 0, 2 * j + 1], q_aug)
        absorb(2 * j, s_sc[0])
        s_sc[0] = _dot(ksa_ref[0, 0, jnp.minimum(2 * j + 2, n_tiles - 1)], q_aug)
        absorb(2 * j + 1, s_sc[1])
        return carry

    lax.fori_loop(0, lax.div(n * Q_BLOCK + 2 * tk - 1, 2 * tk), tile_pair, 0)
    acc = acc_sc[...]
    o_s = acc[:HEAD_DIM] * (1.0 / acc[HEAD_DIM:HEAD_DIM + 1])

    gates = _sigmoid(gl_ref[0, 0] + gb_ref[0])
    gate = lambda c: jnp.concatenate([gates[3 * r + c:3 * r + c + 1] for r in range(rep)], axis=1)
    o = gate(0) * o_c + gate(1) * o_s + gate(2) * o_w
    oa_ref[0] = _unstack_heads_t(o, rep).astype(oa_ref.dtype)


def _selection_map(t, nc_pad, ns_pad):
    nc = (t - CMP_LEN) // CMP_STRIDE + 1
    ns = t // SEL_LEN
    cs = np.arange(nc, dtype=np.int64)[:, None] * CMP_STRIDE
    ss = np.arange(ns, dtype=np.int64)[None, :] * SEL_LEN
    out = np.zeros((nc_pad, ns_pad), np.float32)
    out[:nc, :ns] = (cs < ss + SEL_LEN) & (cs + CMP_LEN > ss)
    return out


def _block_mask_cols(t, ns_pad, tk):
    key_block = (np.arange(t, dtype=np.int64) // SEL_LEN).reshape(t // tk, tk, 1)
    hit = np.arange(ns_pad, dtype=np.int64)[None, None, :] == key_block
    return np.where(hit, NEG_INF, 0.0).astype(np.float32)


def _attention(q, gate_logits, gate_bias, kc, vcs, ksd, vsd, ksa, vsa, kw, vw, kb, vb, sinks, n_sel_blocks):
    b, t = q.shape[:2]
    g = kc.shape[1]
    assert ksa.shape[2] % 2 == 0, "selected-branch key tiles are consumed in pairs"
    width = NSA_REP * HEAD_DIM
    ng = gate_logits.shape[2]
    per_group = lambda a: pl.BlockSpec((1, 1) + a.shape[2:], lambda i, j, n: (i, j) + (0,) * (a.ndim - 2))
    prev = lambda i, j, n: (i, j, jnp.maximum(n - 1, 0), 0, 0)
    cur = lambda i, j, n: (i, j, n, 0, 0)
    k_tile = lambda idx: pl.BlockSpec((1, 1, 1, Q_BLOCK, HEAD_DIM), idx)
    v_tile = lambda idx: pl.BlockSpec((1, 1, 1, 2 * HEAD_DIM, Q_BLOCK), idx)
    out_spec = pl.BlockSpec((1, Q_BLOCK, width), lambda i, j, n: (i, n, j))
    out_shape = jax.ShapeDtypeStruct((b, t, g * width), BF16)
    return pl.pallas_call(
        functools.partial(_attn_kernel, n_sel_blocks),
        grid=(b, g, t // Q_BLOCK),
        in_specs=[
            pl.BlockSpec(memory_space=pltpu.SMEM),
            pl.BlockSpec((1, Q_BLOCK, width), lambda i, j, n: (i, n, j)),
            pl.BlockSpec((1, Q_BLOCK, width), lambda i, j, n: (i, n, g + j)),
            pl.BlockSpec((1, 1, ng, Q_BLOCK), lambda i, j, n: (i, j, 0, n)),
            pl.BlockSpec((1, ng, 1), lambda i, j, n: (j, 0, 0)),
            per_group(kc), per_group(vcs), k_tile(cur), v_tile(cur), per_group(ksa), per_group(vsa),
            per_group(kw), per_group(vw),
            k_tile(prev), k_tile(cur), v_tile(prev), v_tile(cur),
        ],
        out_specs=[out_spec, out_spec],
        out_shape=[out_shape, out_shape],
        scratch_shapes=[
            pltpu.VMEM((1, NSA_REP * Q_BLOCK), F32),
            pltpu.VMEM((2 * HEAD_DIM, NSA_REP * Q_BLOCK), F32),
            pltpu.VMEM((2, ksa.shape[3], NSA_REP * Q_BLOCK), F32),
        ],
        compiler_params=_cparams(3),
        name="nsa_swa_attention",
    )(sinks, q, q, gate_logits, gate_bias, kc, vcs, ksd, vsd, ksa, vsa, kw, vw, kb, kb, vb, vb)


def _conv_silu_kernel(n_q_cols, x_ref, prev_ref, w_ref, b_ref, o_ref, buf):
    i = pl.program_id(1)
    rows = x_ref.shape[1]
    halo = prev_ref.shape[1]
    buf[:halo, :] = jnp.where(i > 0, prev_ref[0], 0.0)
    buf[halo:, :] = x_ref[0]
    y = b_ref[...]
    for j in range(CONV_WIDTH):
        start = halo - (CONV_WIDTH - 1) + j
        y = y + w_ref[j:j + 1, :] * buf[start:start + rows, :]
    y = _silu(y)
    col = lax.broadcasted_iota(jnp.int32, y.shape, 1)
    o_ref[0] = jnp.where(col >= n_q_cols, y * (MLSTM_QK_DIM ** -0.5), y).astype(o_ref.dtype)


def _conv_silu(x, w, bias):
    b, t, c = x.shape
    halo = SUBLANES
    ratio = ROW_TILE // halo
    return pl.pallas_call(
        functools.partial(_conv_silu_kernel, c // 2),
        grid=(b, t // ROW_TILE),
        in_specs=[
            pl.BlockSpec((1, ROW_TILE, c), lambda i, j: (i, j, 0)),
            pl.BlockSpec((1, halo, c), lambda i, j: (i, jnp.maximum(j * ratio - 1, 0), 0)),
            _resident(w.shape),
            _resident((1, c)),
        ],
        out_specs=pl.BlockSpec((1, ROW_TILE, c), lambda i, j: (i, j, 0)),
        out_shape=jax.ShapeDtypeStruct((b, t, c), BF16),
        scratch_shapes=[pltpu.VMEM((ROW_TILE + halo, c), F32)],
        compiler_params=_cparams(2),
        name="conv_silu",
    )(x, x, w, bias.reshape(1, c))


def _mlstm_kernel(bias_ref, q_ref, k_ref, v_ref, ig_ref, fg_ref, o_ref, state, m_state):
    c = pl.program_id(1)
    L = MLSTM_CHUNK
    dqk, dv = MLSTM_QK_DIM, MLSTM_V_DIM

    @pl.when(c == 0)
    def _():
        state[...] = jnp.zeros(state.shape, F32)
        m_state[...] = jnp.zeros(m_state.shape, F32)

    t_idx = lax.broadcasted_iota(jnp.int32, (L, L), 0)
    s_idx = lax.broadcasted_iota(jnp.int32, (L, L), 1)
    causal = s_idx <= t_idx
    eye = s_idx == t_idx
    ones_col = (lax.broadcasted_iota(jnp.int32, (L, LANES), 1) == 0).astype(BF16)

    def to_col(row):
        return jnp.sum(jnp.where(eye, row, 0.0), axis=1, keepdims=True)

    for j in range(MLSTM_CHUNKS_PER_STEP):
        rows = slice(j * L, (j + 1) * L)
        for h in range(MLSTM_HEADS):
            qc = q_ref[0, rows, h * dqk:(h + 1) * dqk]
            kc = k_ref[0, rows, h * dqk:(h + 1) * dqk]
            va = jnp.concatenate([v_ref[0, rows, h * dv:(h + 1) * dv], ones_col], axis=1)
            i_row = ig_ref[0, h, j:j + 1, :] + bias_ref[0, h]
            f_pre = fg_ref[0, h, j:j + 1, :] + bias_ref[1, h]
            f_row = jnp.minimum(f_pre, 0.0) - jnp.log(1.0 + jnp.exp(-jnp.abs(f_pre)))
            f_col = to_col(f_row)
            b_col = jnp.sum(jnp.where(causal, f_row, 0.0), axis=1, keepdims=True)
            b_row = jnp.sum(jnp.where(t_idx <= s_idx, f_col, 0.0), axis=0, keepdims=True)
            m_old = m_state[h:h + 1, 0:1]
            a = b_col + m_old
            d = jnp.where(causal, b_col - b_row + i_row, -jnp.inf)
            mt = jnp.maximum(a, jnp.max(d, axis=1, keepdims=True))
            w_inter = jnp.exp(a - mt)
            sm = lax.dot_general(qc, kc, (((1,), (1,)), ((), ())), preferred_element_type=F32) * jnp.exp(d - mt)
            st = state[h]
            nd = w_inter * _dot(qc, st.astype(BF16)) + _dot(sm.astype(BF16), va)
            hc = nd[:, :dv] * (1.0 / jnp.maximum(jnp.abs(nd[:, dv:dv + 1]), jnp.exp(-mt)))
            o_ref[0, rows, h * dv:(h + 1) * dv] = hc.astype(o_ref.dtype)
            b_last = b_row[:, L - 1:L]
            wl = b_last - b_row + i_row
            m_new = jnp.maximum(b_last + m_old, jnp.max(wl, axis=1, keepdims=True))
            decay = jnp.exp(b_last + m_old - m_new)
            w_col = to_col(jnp.exp(wl - m_new))
            kw = (kc.astype(F32) * w_col).astype(BF16)
            upd = lax.dot_general(kw, va, (((0,), (0,)), ((), ())), preferred_element_type=F32)
            state[h] = decay * st + upd
            m_state[h:h + 1, 0:1] = m_new


def _mlstm(qk, v, ig, fg, gate_bias):
    b, t = qk.shape[:2]
    nh = MLSTM_HEADS
    rows = MLSTM_CHUNKS_PER_STEP * MLSTM_CHUNK
    gate_spec = pl.BlockSpec((1, nh, MLSTM_CHUNKS_PER_STEP, MLSTM_CHUNK), lambda i, c: (i, 0, c, 0))
    return pl.pallas_call(
        _mlstm_kernel,
        grid=(b, t // rows),
        in_specs=[
            pl.BlockSpec(memory_space=pltpu.SMEM),
            pl.BlockSpec((1, rows, nh * MLSTM_QK_DIM), lambda i, c: (i, c, 0)),
            pl.BlockSpec((1, rows, nh * MLSTM_QK_DIM), lambda i, c: (i, c, 1)),
            pl.BlockSpec((1, rows, nh * MLSTM_V_DIM), lambda i, c: (i, c, 0)),
            gate_spec, gate_spec,
        ],
        out_specs=pl.BlockSpec((1, rows, nh * MLSTM_V_DIM), lambda i, c: (i, c, 0)),
        out_shape=jax.ShapeDtypeStruct((b, t, nh * MLSTM_V_DIM), BF16),
        scratch_shapes=[
            pltpu.VMEM((nh, MLSTM_QK_DIM, MLSTM_V_DIM + LANES), F32),
            pltpu.VMEM((SUBLANES, LANES), F32),
        ],
        compiler_params=_cparams(2),
        name="mlstm_scan",
    )(gate_bias, qk, qk, v, ig, fg)


def _row_tiles(a, tk):
    b, t, _ = a.shape
    a = a.reshape(b, t // tk, tk, -1, HEAD_DIM)
    return a.transpose(0, 3, 1, 2, 4)


def _with_ones_rows(a_t):
    ones = jnp.zeros(a_t.shape, a_t.dtype).at[..., 0, :].set(1.0)
    return jnp.concatenate([a_t, ones], axis=-2)


def _value_tiles_t(a, tk):
    b, t, _ = a.shape
    a = a.reshape(b, t // tk, tk, -1, HEAD_DIM)
    return _with_ones_rows(a.transpose(0, 3, 1, 4, 2))


def _nsa_swa_layer(h, norm_g, w_in, gate_bias, pos_k, pos_v, ck_w1, ck_w2, cv_w1, cv_w2, sinks, w_out):
    b, t, d = h.shape
    m = b * t
    gk = NSA_KV_HEADS * HEAD_DIM
    nq_a, nq_b = NSA_HEADS * HEAD_DIM, SWA_HEADS * HEAD_DIM
    ngate = NSA_HEADS * 3
    o_kc, o_vc, o_ks, o_vs, o_kw, o_vw = (nq_a + i * gk for i in range(6))
    o_g = nq_a + 6 * gk
    o_qb = o_g + ngate
    o_kb, o_vb = o_qb + nq_b, o_qb + nq_b + gk
    cols = lambda o, n: w_in[:, o:o + n]
    w_q = jnp.concatenate([cols(0, nq_a), cols(o_qb, nq_b)], axis=1).astype(BF16)
    w_kv = jnp.concatenate([cols(o_ks, gk), cols(o_vs, gk), cols(o_kw, gk), cols(o_vw, gk),
                            cols(o_kb, gk), cols(o_vb, gk)], axis=1).astype(BF16)
    w_f32 = jnp.concatenate([cols(o_kc, gk), cols(o_vc, gk), cols(o_g, ngate),
                             jnp.zeros((d, LANES - ngate), w_in.dtype)], axis=1).astype(BF16)
    q_all, kv_all, misc = _norm_proj(h.reshape(m, d), norm_g, [w_q, w_kv, w_f32], [BF16, BF16, F32])
    q_all = q_all.reshape(b, t, nq_a + nq_b)
    kv_all = kv_all.reshape(b, t, 6 * gk)
    misc = misc.reshape(b, t, -1)

    rows = t // CMP_STRIDE
    flat = CMP_STRIDE * HEAD_DIM
    raw = misc[..., :2 * gk].reshape(b, rows, CMP_STRIDE, 2, NSA_KV_HEADS, HEAD_DIM)
    raw = raw.transpose(3, 0, 4, 1, 2, 5).reshape(2, b, NSA_KV_HEADS, rows, flat)
    pos = jnp.stack([pos_k, pos_v]).reshape(2, 2, 1, flat)
    w1 = jnp.stack([ck_w1, cv_w1]).astype(BF16)
    w2 = jnp.stack([ck_w2, cv_w2]).astype(BF16)
    cmp_kv = _compress(raw, pos, w1, w2)
    n_sel = t // SEL_LEN
    ns_pad = -(-n_sel // LANES) * LANES
    sel_map_t = jnp.asarray(_selection_map(t, rows, ns_pad).T, BF16)
    vcs = jnp.concatenate([_with_ones_rows(cmp_kv[1].transpose(0, 1, 3, 2)),
                           jnp.broadcast_to(sel_map_t, (b, NSA_KV_HEADS) + sel_map_t.shape)], axis=2)

    part = lambda i: kv_all[..., i * gk:(i + 1) * gk]
    tk = min(SEL_KEY_TILE, t)
    gl = misc[..., 2 * gk:2 * gk + ngate].reshape(b, t, NSA_KV_HEADS, NSA_REP * 3).transpose(0, 2, 3, 1)
    gb = gate_bias.reshape(NSA_KV_HEADS, NSA_REP * 3, 1)
    mask_cols = jnp.asarray(_block_mask_cols(t, ns_pad, tk), BF16)
    ks_rows = _row_tiles(part(0), tk)
    ksa = jnp.concatenate([jnp.broadcast_to(mask_cols, ks_rows.shape[:2] + mask_cols.shape), ks_rows], axis=-1)
    o_a, o_b = _attention(q_all, gl, gb, cmp_kv[0], vcs,
                          _row_tiles(part(0), Q_BLOCK), _value_tiles_t(part(1), Q_BLOCK),
                          ksa, _value_tiles_t(part(1), tk),
                          _row_tiles(part(2), Q_BLOCK), _value_tiles_t(part(3), Q_BLOCK),
                          _row_tiles(part(4), Q_BLOCK), _value_tiles_t(part(5), Q_BLOCK), sinks, n_sel)

    w_o = w_out.astype(BF16)
    out = _out_proj(h.reshape(m, d), [o_a.reshape(m, nq_a), o_b.reshape(m, nq_b)], [w_o[:nq_a], w_o[nq_a:]])
    return out.reshape(b, t, d)


def _mlstm_layer(h, norm_g, w_in, conv_w, conv_b, igate_bias, fgate_bias, w_out):
    b, t, d = h.shape
    m = b * t
    nqk = 2 * MLSTM_HEADS * MLSTM_QK_DIM
    nv = MLSTM_HEADS * MLSTM_V_DIM
    w = w_in.astype(BF16)
    w_gates = jnp.concatenate([w[:, nqk + 2 * nv:], jnp.zeros((d, LANES - 2 * MLSTM_HEADS), BF16)], axis=1)
    qk_raw, v, og, gates = _norm_proj(
        h.reshape(m, d), norm_g,
        [w[:, :nqk], w[:, nqk:nqk + nv], w[:, nqk + nv:nqk + 2 * nv], w_gates],
        [F32, BF16, F32, F32])
    qk = _conv_silu(qk_raw.reshape(b, t, nqk), conv_w, conv_b)
    gates = gates.reshape(b, t, -1)
    chunked = lambda a: a.transpose(0, 2, 1).reshape(b, MLSTM_HEADS, t // MLSTM_CHUNK, MLSTM_CHUNK)
    ig = chunked(gates[..., :MLSTM_HEADS])
    fg = chunked(gates[..., MLSTM_HEADS:2 * MLSTM_HEADS])
    hh = _mlstm(qk, v.reshape(b, t, nv), ig, fg, jnp.stack([igate_bias, fgate_bias]))
    out = _out_proj(h.reshape(m, d), [hh.reshape(m, nv)], [w_out.astype(BF16)], gate=og)
    return out.reshape(b, t, d)


def kernel(x, norm_mix, norm_ffn, ffn_w_gate, ffn_w_up, ffn_w_down, ab_w_in, ab_gate_bias, nsa_pos_k, nsa_pos_v, nsa_cmp_k_w1, nsa_cmp_k_w2, nsa_cmp_v_w1, nsa_cmp_v_w2, swa_sinks, ab_w_out, c_w_in, c_conv_w, c_conv_b, c_igate_bias, c_fgate_bias, c_w_out, final_norm):
    depth = norm_mix.shape[0]
    b, t, d = x.shape
    h = x
    for layer in range(depth):
        j = layer // 2
        if layer % 2 == 0:
            h = _nsa_swa_layer(h, norm_mix[layer], ab_w_in[j], ab_gate_bias[j], nsa_pos_k[j], nsa_pos_v[j],
                               nsa_cmp_k_w1[j], nsa_cmp_k_w2[j], nsa_cmp_v_w1[j], nsa_cmp_v_w2[j],
                               swa_sinks[j], ab_w_out[j])
        else:
            h = _mlstm_layer(h, norm_mix[layer], c_w_in[j], c_conv_w[j], c_conv_b[j], c_igate_bias[j],
                             c_fgate_bias[j], c_w_out[j])
        last = layer == depth - 1
        h = _ffn(h.reshape(b * t, d), norm_ffn[layer], ffn_w_gate[layer].astype(BF16),
                 ffn_w_up[layer].astype(BF16), ffn_w_down[layer].astype(BF16),
                 final_norm if last else None).reshape(b, t, d)
    if depth == 0:
        raise ValueError("depth must be positive")
    return h
```

```python
import functools

import numpy as np
import jax
import jax.numpy as jnp
from jax import lax
from jax.experimental import pallas as pl
from jax.experimental.pallas import tpu as pltpu

HEAD_DIM = 64
NSA_HEADS = 8
NSA_KV_HEADS = 2
NSA_REP = NSA_HEADS // NSA_KV_HEADS
CMP_LEN = 32
CMP_STRIDE = 16
SEL_LEN = 64
SEL_TOPK = 16
NSA_WINDOW = 512
SWA_HEADS = 8
SWA_KV_HEADS = 2
SWA_REP = SWA_HEADS // SWA_KV_HEADS
SWA_WINDOW = 128
Q_BLOCK = 128
MLSTM_HEADS = 4
MLSTM_QK_DIM = 128
MLSTM_V_DIM = 256
MLSTM_CHUNK = 64
CONV_WIDTH = 4
RMS_EPS = 1e-6
NEG_INF = -1e30
FORCE_SCORE = 1e9
BELOW_NEG_INF = -3e38
SEL_SHIFT = SEL_LEN.bit_length() - 1
assert 1 << SEL_SHIFT == SEL_LEN

LANES = 128
SUBLANES = 8
VMEM_LIMIT_BYTES = 56 * 1024 * 1024

ROW_TILE = 512
SCAN_CHUNK = 256
SCAN_ROWS = 512

F32 = jnp.float32
BF16 = jnp.bfloat16


def _cparams(n_axes):
    return pltpu.CompilerParams(
        dimension_semantics=("arbitrary",) * n_axes,
        vmem_limit_bytes=VMEM_LIMIT_BYTES,
    )


def _resident(shape):
    nd = len(shape)
    return pl.BlockSpec(shape, lambda *_: (0,) * nd)


def _dot(a, b):
    return jnp.dot(a, b, preferred_element_type=F32)


def _sigmoid(x):
    return 1.0 / (1.0 + jnp.exp(-x))


def _silu(x):
    return x * _sigmoid(x)


def _rmsnorm(x, g):
    y = x * lax.rsqrt(jnp.mean(x * x, axis=-1, keepdims=True) + RMS_EPS)
    return y * g


def _norm_proj_kernel(n_out, x_ref, g_ref, *refs):
    w_refs, o_refs = refs[:n_out], refs[n_out:]
    xn = _rmsnorm(x_ref[...], g_ref[...]).astype(BF16)
    for w_ref, o_ref in zip(w_refs, o_refs):
        o_ref[...] = _dot(xn, w_ref[...]).astype(o_ref.dtype)


def _norm_proj(x, g, ws, out_dtypes):
    m, d = x.shape
    n_out = len(ws)
    return pl.pallas_call(
        functools.partial(_norm_proj_kernel, n_out),
        grid=(m // ROW_TILE,),
        in_specs=[pl.BlockSpec((ROW_TILE, d), lambda i: (i, 0)), _resident((1, d))]
        + [_resident(w.shape) for w in ws],
        out_specs=[pl.BlockSpec((ROW_TILE, w.shape[1]), lambda i: (i, 0)) for w in ws],
        out_shape=[jax.ShapeDtypeStruct((m, w.shape[1]), dt) for w, dt in zip(ws, out_dtypes)],
        compiler_params=_cparams(1),
        name="norm_proj",
    )(x, g.reshape(1, d), *ws)


def _proj0_kernel(x_ref, g_ref, wq_ref, wk_ref, wv_ref, wc_ref,
                  q_ref, k_ref, c_ref, vbig_ref, vsmall_ref, gt_ref):
    xn = _rmsnorm(x_ref[...], g_ref[...]).astype(BF16)
    q_ref[...] = _dot(xn, wq_ref[...]).astype(q_ref.dtype)
    k_ref[...] = _dot(xn, wk_ref[...]).astype(k_ref.dtype)
    cg = _dot(xn, wc_ref[...])
    c_ref[...] = cg[:, :c_ref.shape[1]]
    gt_ref[0] = cg[:, c_ref.shape[1]:].T[:gt_ref.shape[1]]
    v = _dot(xn, wv_ref[...])
    width = v.shape[1] // 3
    for a in range(3):
        v_t = v[:, a * width:(a + 1) * width].T.astype(BF16)
        if a == 0:
            vbig_ref[0] = v_t
        for j in range(x_ref.shape[0] // Q_BLOCK):
            vsmall_ref[a, j] = v_t[:, j * Q_BLOCK:(j + 1) * Q_BLOCK]


def _proj0(x, g, w_q, w_k, w_v, w_c, n_cmp_cols, gate_rows):
    m, d = x.shape
    steps = m // ROW_TILE
    tiles = ROW_TILE // Q_BLOCK
    gd = w_v.shape[1] // 3
    row = lambda w: pl.BlockSpec((ROW_TILE, w), lambda i: (i, 0))
    return pl.pallas_call(
        _proj0_kernel,
        grid=(steps,),
        in_specs=[row(d), _resident((1, d))] + [_resident(w.shape) for w in (w_q, w_k, w_v, w_c)],
        out_specs=[
            row(w_q.shape[1]), row(w_k.shape[1]), row(n_cmp_cols),
            pl.BlockSpec((1, gd, ROW_TILE), lambda i: (i, 0, 0)),
            pl.BlockSpec((3, tiles, gd, Q_BLOCK), lambda i: (0, i, 0, 0)),
            pl.BlockSpec((1, gate_rows, ROW_TILE), lambda i: (i, 0, 0)),
        ],
        out_shape=[
            jax.ShapeDtypeStruct((m, w_q.shape[1]), BF16),
            jax.ShapeDtypeStruct((m, w_k.shape[1]), BF16),
            jax.ShapeDtypeStruct((m, n_cmp_cols), F32),
            jax.ShapeDtypeStruct((steps, gd, ROW_TILE), BF16),
            jax.ShapeDtypeStruct((3, steps * tiles, gd, Q_BLOCK), BF16),
            jax.ShapeDtypeStruct((steps, gate_rows, ROW_TILE), F32),
        ],
        compiler_params=_cparams(1),
        name="norm_proj_attn",
    )(x, g.reshape(1, d), w_q, w_k, w_v, w_c)


def _out_proj_kernel(n_in, gated, res_ref, *refs):
    a_refs = refs[:n_in]
    w_refs = refs[n_in:2 * n_in]
    rest = refs[2 * n_in:]
    o_ref = rest[-1]
    acc = res_ref[...]
    for idx, (a_ref, w_ref) in enumerate(zip(a_refs, w_refs)):
        a = a_ref[...]
        if gated and idx == 0:
            a = (a.astype(F32) * _sigmoid(rest[0][...])).astype(BF16)
        acc = acc + _dot(a.astype(BF16), w_ref[...])
    o_ref[...] = acc


def _out_proj(res, a_list, w_list, gate=None):
    m, d = res.shape
    n_in = len(a_list)
    row = lambda w: pl.BlockSpec((ROW_TILE, w), lambda i: (i, 0))
    in_specs = [row(d)] + [row(a.shape[1]) for a in a_list] + [_resident(w.shape) for w in w_list]
    args = [res, *a_list, *w_list]
    if gate is not None:
        in_specs.append(row(gate.shape[1]))
        args.append(gate)
    return pl.pallas_call(
        functools.partial(_out_proj_kernel, n_in, gate is not None),
        grid=(m // ROW_TILE,),
        in_specs=in_specs,
        out_specs=row(d),
        out_shape=jax.ShapeDtypeStruct((m, d), F32),
        compiler_params=_cparams(1),
        name="out_proj",
    )(*args)


def _ffn_kernel(n_chunks, chunk, final, x_ref, g_ref, wg_ref, wu_ref, wd_ref, *rest):
    o_ref = rest[-1]
    x = x_ref[...]
    xn = _rmsnorm(x, g_ref[...]).astype(BF16)
    acc = x
    for c in range(n_chunks):
        cols = slice(c * chunk, (c + 1) * chunk)
        gate = _dot(xn, wg_ref[:, cols])
        up = _dot(xn, wu_ref[:, cols])
        h = (_silu(gate) * up).astype(BF16)
        acc = acc + _dot(h, wd_ref[cols, :])
    if final:
        acc = _rmsnorm(acc, rest[0][...])
    o_ref[...] = acc


def _ffn_chunk(hidden):
    return hidden


def _ffn(x, g, wg, wu, wd, final_g=None):
    m, d = x.shape
    hidden = wg.shape[1]
    chunk = _ffn_chunk(hidden)
    row = pl.BlockSpec((ROW_TILE, d), lambda i: (i, 0))
    in_specs = [row, _resident((1, d)), _resident(wg.shape), _resident(wu.shape), _resident(wd.shape)]
    args = [x, g.reshape(1, d), wg, wu, wd]
    if final_g is not None:
        in_specs.append(_resident((1, d)))
        args.append(final_g.reshape(1, d))
    return pl.pallas_call(
        functools.partial(_ffn_kernel, hidden // chunk, chunk, final_g is not None),
        grid=(m // ROW_TILE,),
        in_specs=in_specs,
        out_specs=row,
        out_shape=jax.ShapeDtypeStruct((m, d), F32),
        compiler_params=_cparams(1),
        name="ffn",
    )(*args)


def _compress_kernel(x_ref, p_ref, w1_ref, w2_ref, rows_ref, cols_ref):
    n = x_ref.shape[1] // CMP_STRIDE
    a = jnp.zeros((n, x_ref.shape[2]), F32)
    b = jnp.zeros((n, x_ref.shape[2]), F32)
    for r in range(CMP_STRIDE):
        x = x_ref[0, pl.ds(r, n, stride=CMP_STRIDE), :]
        a = a + _dot((x + p_ref[0, r]).astype(BF16), w1_ref[0, r])
        b = b + _dot((x + p_ref[0, CMP_STRIDE + r]).astype(BF16), w1_ref[0, CMP_STRIDE + r])
    h = a + pltpu.roll(b, n - 1, 0)
    out = _dot(_silu(h).astype(BF16), w2_ref[0])
    live = lax.broadcasted_iota(jnp.int32, out.shape, 0) < n - 1
    out = jnp.where(live, out, 0.0)
    rows_ref[0, 0] = out.astype(rows_ref.dtype)
    cols_ref[0, 0] = out.T.astype(cols_ref.dtype)


def _compress(x, pos, w1, w2):
    b, t, two_gd = x.shape
    gd = two_gd // 2
    n = t // CMP_STRIDE
    return pl.pallas_call(
        _compress_kernel,
        grid=(2, b),
        in_specs=[
            pl.BlockSpec((1, t, gd), lambda s, i: (i, 0, s)),
            pl.BlockSpec((1,) + pos.shape[1:], lambda s, i: (s, 0, 0, 0)),
            pl.BlockSpec((1,) + w1.shape[1:], lambda s, i: (s, 0, 0, 0)),
            pl.BlockSpec((1,) + w2.shape[1:], lambda s, i: (s, 0, 0)),
        ],
        out_specs=[pl.BlockSpec((1, 1, n, gd), lambda s, i: (s, i, 0, 0)),
                   pl.BlockSpec((1, 1, gd, n), lambda s, i: (s, i, 0, 0))],
        out_shape=[jax.ShapeDtypeStruct((2, b, n, gd), BF16), jax.ShapeDtypeStruct((2, b, gd, n), BF16)],
        compiler_params=_cparams(2),
        name="compress_kv",
    )(x, pos, w1, w2)


def _stack_heads_t(q_ref, rep):
    qt = (q_ref[0].astype(F32) * (HEAD_DIM ** -0.5)).T
    return jnp.concatenate([qt[r * HEAD_DIM:(r + 1) * HEAD_DIM] for r in range(rep)], axis=1).astype(BF16)


def _unstack_heads_t(o_t, rep):
    tq = o_t.shape[1] // rep
    return jnp.concatenate([o_t[:, r * tq:(r + 1) * tq] for r in range(rep)], axis=0).T


def _tile_lanes(x, rep):
    return jnp.concatenate([x] * rep, axis=1)


def _softmax_pv_t(s, bias, v_aug_t, sink_row=None):
    rep = s.shape[1] // bias.shape[1]
    s = s + _tile_lanes(bias, rep)
    m = jnp.max(s, axis=0, keepdims=True)
    if sink_row is not None:
        m = jnp.maximum(m, sink_row)
    acc = _dot(v_aug_t, jnp.exp(s - m).astype(BF16))
    den = acc[HEAD_DIM:HEAD_DIM + 1]
    if sink_row is not None:
        den = den + jnp.exp(sink_row - m)
    return acc[:HEAD_DIM] * (1.0 / den)


def _window_bias(n_rows, first_pos, n, window):
    row = lax.broadcasted_iota(jnp.int32, (n_rows, Q_BLOCK), 0)
    col = lax.broadcasted_iota(jnp.int32, (n_rows, Q_BLOCK), 1)
    sp = first_pos + row
    dist = n * Q_BLOCK + col - sp
    return jnp.where((sp >= 0) & (dist >= 0) & (dist < window), 0.0, NEG_INF)


def _attn_kernel(n_sel_blocks, sink_ref, qa_ref, qb_ref, gl_ref, gb_ref, kc_ref, vct_ref, selmap_ref,
                 ksd_ref, ks_ref, mask_ref, vsd_ref, vbig_ref, kw_ref, vw_ref, kbp_ref, kbc_ref, vbp_ref, vbc_ref,
                 oa_ref, ob_ref, m_sc, acc_sc, s_sc):
    g, n = pl.program_id(1), pl.program_id(2)
    rep = NSA_REP
    n_groups = ksd_ref.shape[2] // HEAD_DIM
    grp = pl.ds(pl.multiple_of(g * HEAD_DIM, HEAD_DIM), HEAD_DIM)
    tq_row = n * Q_BLOCK + lax.broadcasted_iota(jnp.int32, (1, Q_BLOCK), 1)

    def pad_q(q_t):
        return jnp.concatenate([jnp.where(g == i, q_t, jnp.zeros_like(q_t)) for i in range(n_groups)], axis=0)

    def with_ones(v_t):
        return jnp.concatenate([v_t, jnp.ones((SUBLANES, v_t.shape[1]), v_t.dtype)], axis=0)

    q_b = pad_q(_stack_heads_t(qb_ref, SWA_REP))
    sink_row = jnp.concatenate([jnp.full((1, Q_BLOCK), sink_ref[g * SWA_REP + r], F32) for r in range(SWA_REP)],
                               axis=1)
    k_b = jnp.concatenate([kbp_ref[0], kbc_ref[0]], axis=0)
    v_b = with_ones(jnp.concatenate([vbp_ref[0, 0, 0, grp, :], vbc_ref[0, 0, 0, grp, :]], axis=1))
    o_b = _softmax_pv_t(_dot(k_b, q_b), _window_bias(2 * Q_BLOCK, (n - 1) * Q_BLOCK, n, SWA_WINDOW), v_b, sink_row)
    ob_ref[0] = _unstack_heads_t(o_b, SWA_REP).astype(ob_ref.dtype)

    q_t = pad_q(_stack_heads_t(qa_ref, rep))

    kc = kc_ref[0, 0]
    n_cmp = kc.shape[0]
    cmp_end = lax.broadcasted_iota(jnp.int32, (n_cmp, Q_BLOCK), 0) * CMP_STRIDE + (CMP_LEN - 1)
    s = _dot(kc, q_t) + _tile_lanes(jnp.where(cmp_end <= tq_row, 0.0, NEG_INF), rep)
    m = jnp.max(s, axis=0, keepdims=True)
    lhs = jnp.concatenate([with_ones(vct_ref[0, 0, grp, :]), selmap_ref[...]], axis=0)
    oc_imp = _dot(lhs, jnp.exp(s - m).astype(BF16))
    den = oc_imp[HEAD_DIM:HEAD_DIM + 1]
    seen = m > 0.5 * NEG_INF
    oc_imp = jnp.where(seen, oc_imp * (1.0 / jnp.where(seen, den, 1.0)), 0.0)
    o_c = oc_imp[:HEAD_DIM]
    imp = oc_imp[HEAD_DIM + SUBLANES:]
    imp = sum(imp[:, r * Q_BLOCK:(r + 1) * Q_BLOCK] for r in range(rep))

    n_win_tiles = NSA_WINDOW // Q_BLOCK + 1
    tiles = [jnp.maximum(n - (n_win_tiles - 1) + i, 0) for i in range(n_win_tiles)]
    k_w = jnp.concatenate([kw_ref[0, pl.ds(pl.multiple_of(j * Q_BLOCK, Q_BLOCK), Q_BLOCK), :] for j in tiles], axis=0)
    v_w = with_ones(jnp.concatenate([vw_ref[0, 0, j, grp, :] for j in tiles], axis=1))
    first = (n - (n_win_tiles - 1)) * Q_BLOCK
    o_w = _softmax_pv_t(_dot(k_w, q_t), _window_bias(n_win_tiles * Q_BLOCK, first, n, NSA_WINDOW), v_w)

    ns_pad = imp.shape[0]
    blk = lax.broadcasted_iota(jnp.int32, (ns_pad, Q_BLOCK), 0)
    blk_f = blk.astype(F32)
    cur = jnp.right_shift(tq_row, SEL_SHIFT)
    forced = (blk == 0) | (blk == cur) | (blk == cur - 1)
    valid = jnp.left_shift(blk, SEL_SHIFT) <= tq_row
    score = jnp.where(valid, imp, NEG_INF)
    score = jnp.where(forced | (blk >= n_sel_blocks), BELOW_NEG_INF, score)
    sel = forced
    for _ in range(min(SEL_TOPK, n_sel_blocks) - 3):
        top = jnp.max(score, axis=0, keepdims=True)
        first_hit = jnp.min(jnp.where(score == top, blk_f, float(ns_pad)), axis=0, keepdims=True)
        pick = blk_f == first_hit
        sel = sel | pick
        score = jnp.where(pick, BELOW_NEG_INF, score)
    n_diag_blk = n * (Q_BLOCK // SEL_LEN)
    unselected = jnp.where(sel & (blk < n_diag_blk), 0.0, 1.0).astype(BF16)

    row = lax.broadcasted_iota(jnp.int32, (Q_BLOCK, Q_BLOCK), 0)
    col = lax.broadcasted_iota(jnp.int32, (Q_BLOCK, Q_BLOCK), 1)
    s = _dot(ksd_ref[0], q_t) + _tile_lanes(jnp.where(row <= col, 0.0, NEG_INF), rep)
    m0 = jnp.max(s, axis=0, keepdims=True)
    m_sc[...] = m0
    acc_sc[...] = _dot(with_ones(vsd_ref[0, 0, 0, grp, :]), jnp.exp(s - m0).astype(BF16))

    q_aug = jnp.concatenate([_tile_lanes(unselected, rep), q_t], axis=0)
    tk = s_sc.shape[1]
    n_tiles = ks_ref.shape[1] // tk

    def logits(kt):
        keys = pl.ds(pl.multiple_of(kt * tk, tk), tk)
        return _dot(jnp.concatenate([mask_ref[keys, :], ks_ref[0, keys, :]], axis=1), q_aug)

    def absorb(kt, s):
        m_old = m_sc[...]
        m_new = jnp.maximum(m_old, jnp.max(s, axis=0, keepdims=True))
        pv = _dot(with_ones(vbig_ref[0, kt, grp, :]), jnp.exp(s - m_new).astype(BF16))
        acc_sc[...] = jnp.exp(m_old - m_new) * acc_sc[...] + pv
        m_sc[...] = m_new

    s_sc[0] = logits(0)

    def tile_pair(j, carry):
        s_sc[1] = logits(2 * j + 1)
        absorb(2 * j, s_sc[0])
        s_sc[0] = logits(jnp.minimum(2 * j + 2, n_tiles - 1))
        absorb(2 * j + 1, s_sc[1])
        return carry

    lax.fori_loop(0, lax.div(n * Q_BLOCK + 2 * tk - 1, 2 * tk), tile_pair, 0)
    acc = acc_sc[...]
    o_s = acc[:HEAD_DIM] * (1.0 / acc[HEAD_DIM:HEAD_DIM + 1])

    gates = _sigmoid(gl_ref[0, 0] + gb_ref[0])
    gate = lambda c: jnp.concatenate([gates[3 * r + c:3 * r + c + 1] for r in range(rep)], axis=1)
    o = gate(0) * o_c + gate(1) * o_s + gate(2) * o_w
    oa_ref[0] = _unstack_heads_t(o, rep).astype(oa_ref.dtype)


def _selection_map(t, nc_pad, ns_pad):
    nc = (t - CMP_LEN) // CMP_STRIDE + 1
    ns = t // SEL_LEN
    cs = np.arange(nc, dtype=np.int64)[:, None] * CMP_STRIDE
    ss = np.arange(ns, dtype=np.int64)[None, :] * SEL_LEN
    out = np.zeros((nc_pad, ns_pad), np.float32)
    out[:nc, :ns] = (cs < ss + SEL_LEN) & (cs + CMP_LEN > ss)
    return out


def _block_mask_cols(t, ns_pad):
    hit = np.arange(ns_pad, dtype=np.int64)[None, :] == (np.arange(t, dtype=np.int64) // SEL_LEN)[:, None]
    return np.where(hit, NEG_INF, 0.0).astype(np.float32)


def _attention(q, k_all, v_big, v_small, gate_t, gate_bias, cmp_rows, cmp_cols, sel_map_t, mask_cols, sinks,
               n_sel_blocks):
    b, t = q.shape[:2]
    gd = k_all.shape[2] // 3
    g = gd // HEAD_DIM
    tk = v_big.shape[3]
    assert (t // tk) % 2 == 0, "selected-branch key tiles are consumed in pairs"
    width = NSA_REP * HEAD_DIM
    gate_cols = gate_t.shape[3] // Q_BLOCK
    prev = lambda n: jnp.maximum(n - 1, 0)
    cur = lambda n: n
    k_tile = lambda blk, at: pl.BlockSpec((1, Q_BLOCK, gd), lambda i, j, n: (i, at(n), blk))
    k_all_rows = lambda blk: pl.BlockSpec((1, t, gd), lambda i, j, n: (i, 0, blk))
    v_tile = lambda a, at: pl.BlockSpec((1, 1, 1, gd, Q_BLOCK), lambda i, j, n: (a, i, at(n), 0, 0))
    out_spec = pl.BlockSpec((1, Q_BLOCK, width), lambda i, j, n: (i, n, j))
    out_shape = jax.ShapeDtypeStruct((b, t, g * width), BF16)
    return pl.pallas_call(
        functools.partial(_attn_kernel, n_sel_blocks),
        grid=(b, g, t // Q_BLOCK),
        in_specs=[
            pl.BlockSpec(memory_space=pltpu.SMEM),
            pl.BlockSpec((1, Q_BLOCK, width), lambda i, j, n: (i, n, j)),
            pl.BlockSpec((1, Q_BLOCK, width), lambda i, j, n: (i, n, g + j)),
            pl.BlockSpec((1, 1, 16, Q_BLOCK), lambda i, j, n: (i, n // gate_cols, j, n % gate_cols)),
            pl.BlockSpec((1, 16, 1), lambda i, j, n: (j, 0, 0)),
            pl.BlockSpec((1, 1) + cmp_rows.shape[2:], lambda i, j, n: (0, i, 0, 0)),
            pl.BlockSpec((1, 1) + cmp_cols.shape[2:], lambda i, j, n: (1, i, 0, 0)),
            _resident(sel_map_t.shape),
            k_tile(0, cur), k_all_rows(0), _resident(mask_cols.shape),
            v_tile(0, cur), pl.BlockSpec((1,) + v_big.shape[1:], lambda i, j, n: (i, 0, 0, 0)),
            k_all_rows(1), pl.BlockSpec((1, 1) + v_small.shape[2:], lambda i, j, n: (1, i, 0, 0, 0)),
            k_tile(2, prev), k_tile(2, cur), v_tile(2, prev), v_tile(2, cur),
        ],
        out_specs=[out_spec, out_spec],
        out_shape=[out_shape, out_shape],
        scratch_shapes=[
            pltpu.VMEM((1, NSA_REP * Q_BLOCK), F32),
            pltpu.VMEM((HEAD_DIM + SUBLANES, NSA_REP * Q_BLOCK), F32),
            pltpu.VMEM((2, tk, NSA_REP * Q_BLOCK), F32),
        ],
        compiler_params=_cparams(3),
        name="nsa_swa_attention",
    )(sinks, q, q, gate_t, gate_bias, cmp_rows, cmp_cols, sel_map_t, k_all, k_all, mask_cols, v_small, v_big,
      k_all, v_small, k_all, k_all, v_small, v_small)


def _conv_silu_kernel(n_q_cols, x_ref, prev_ref, w_ref, b_ref, o_ref, buf):
    i = pl.program_id(1)
    rows = x_ref.shape[1]
    halo = prev_ref.shape[1]
    buf[:halo, :] = jnp.where(i > 0, prev_ref[0], 0.0)
    buf[halo:, :] = x_ref[0]
    y = b_ref[...]
    for j in range(CONV_WIDTH):
        start = halo - (CONV_WIDTH - 1) + j
        y = y + w_ref[j:j + 1, :] * buf[start:start + rows, :]
    y = _silu(y)
    col = lax.broadcasted_iota(jnp.int32, y.shape, 1)
    o_ref[0] = jnp.where(col >= n_q_cols, y * (MLSTM_QK_DIM ** -0.5), y).astype(o_ref.dtype)


def _conv_silu(x, w, bias):
    b, t, c = x.shape
    halo = SUBLANES
    ratio = ROW_TILE // halo
    return pl.pallas_call(
        functools.partial(_conv_silu_kernel, c // 2),
        grid=(b, t // ROW_TILE),
        in_specs=[
            pl.BlockSpec((1, ROW_TILE, c), lambda i, j: (i, j, 0)),
            pl.BlockSpec((1, halo, c), lambda i, j: (i, jnp.maximum(j * ratio - 1, 0), 0)),
            _resident(w.shape),
            _resident((1, c)),
        ],
        out_specs=pl.BlockSpec((1, ROW_TILE, c), lambda i, j: (i, j, 0)),
        out_shape=jax.ShapeDtypeStruct((b, t, c), BF16),
        scratch_shapes=[pltpu.VMEM((ROW_TILE + halo, c), F32)],
        compiler_params=_cparams(2),
        name="conv_silu",
    )(x, x, w, bias.reshape(1, c))


def _mlstm_kernel(bias_ref, q_ref, k_ref, v_ref, ig_ref, fg_ref, o_ref, state, m_state):
    c = pl.program_id(1)
    L = SCAN_CHUNK
    dqk, dv = MLSTM_QK_DIM, MLSTM_V_DIM

    @pl.when(c == 0)
    def _():
        state[...] = jnp.zeros(state.shape, F32)
        m_state[...] = jnp.zeros(m_state.shape, F32)

    t_idx = lax.broadcasted_iota(jnp.int32, (L, L), 0)
    s_idx = lax.broadcasted_iota(jnp.int32, (L, L), 1)
    causal = s_idx <= t_idx
    eye = s_idx == t_idx
    ones_col = (lax.broadcasted_iota(jnp.int32, (L, LANES), 1) == 0).astype(BF16)

    def to_col(row):
        return jnp.sum(jnp.where(eye, row, 0.0), axis=1, keepdims=True)

    for j in range(SCAN_ROWS // SCAN_CHUNK):
        rows = slice(j * L, (j + 1) * L)
        for h in range(MLSTM_HEADS):
            qc = q_ref[0, rows, h * dqk:(h + 1) * dqk]
            kc = k_ref[0, rows, h * dqk:(h + 1) * dqk]
            va = jnp.concatenate([v_ref[0, rows, h * dv:(h + 1) * dv], ones_col], axis=1)
            i_row = ig_ref[0, h:h + 1, rows] + bias_ref[0, h]
            f_pre = fg_ref[0, h:h + 1, rows] + bias_ref[1, h]
            f_row = jnp.minimum(f_pre, 0.0) - jnp.log(1.0 + jnp.exp(-jnp.abs(f_pre)))
            f_col = to_col(f_row)
            b_col = jnp.sum(jnp.where(causal, f_row, 0.0), axis=1, keepdims=True)
            b_row = jnp.sum(jnp.where(t_idx <= s_idx, f_col, 0.0), axis=0, keepdims=True)
            m_old = m_state[h:h + 1, 0:1]
            a = b_col + m_old
            d = jnp.where(causal, b_col - b_row + i_row, -jnp.inf)
            mt = jnp.maximum(a, jnp.max(d, axis=1, keepdims=True))
            w_inter = jnp.exp(a - mt)
            sm = lax.dot_general(qc, kc, (((1,), (1,)), ((), ())), preferred_element_type=F32) * jnp.exp(d - mt)
            st = state[h]
            nd = w_inter * _dot(qc, st.astype(BF16)) + _dot(sm.astype(BF16), va)
            hc = nd[:, :dv] * (1.0 / jnp.maximum(jnp.abs(nd[:, dv:dv + 1]), jnp.exp(-mt)))
            o_ref[0, rows, h * dv:(h + 1) * dv] = hc.astype(o_ref.dtype)
            b_last = b_row[:, L - 1:L]
            wl = b_last - b_row + i_row
            m_new = jnp.maximum(b_last + m_old, jnp.max(wl, axis=1, keepdims=True))
            decay = jnp.exp(b_last + m_old - m_new)
            w_col = to_col(jnp.exp(wl - m_new))
            kw = (kc.astype(F32) * w_col).astype(BF16)
            upd = lax.dot_general(kw, va, (((0,), (0,)), ((), ())), preferred_element_type=F32)
            state[h] = decay * st + upd
            m_state[h:h + 1, 0:1] = m_new


def _mlstm(qk, v, ig, fg, gate_bias):
    b, t = qk.shape[:2]
    nh = MLSTM_HEADS
    rows = SCAN_ROWS
    gate_spec = pl.BlockSpec((1, nh, rows), lambda i, c: (i, 0, c))
    return pl.pallas_call(
        _mlstm_kernel,
        grid=(b, t // rows),
        in_specs=[
            pl.BlockSpec(memory_space=pltpu.SMEM),
            pl.BlockSpec((1, rows, nh * MLSTM_QK_DIM), lambda i, c: (i, c, 0)),
            pl.BlockSpec((1, rows, nh * MLSTM_QK_DIM), lambda i, c: (i, c, 1)),
            pl.BlockSpec((1, rows, nh * MLSTM_V_DIM), lambda i, c: (i, c, 0)),
            gate_spec, gate_spec,
        ],
        out_specs=pl.BlockSpec((1, rows, nh * MLSTM_V_DIM), lambda i, c: (i, c, 0)),
        out_shape=jax.ShapeDtypeStruct((b, t, nh * MLSTM_V_DIM), BF16),
        scratch_shapes=[
            pltpu.VMEM((nh, MLSTM_QK_DIM, MLSTM_V_DIM + LANES), F32),
            pltpu.VMEM((SUBLANES, LANES), F32),
        ],
        compiler_params=_cparams(2),
        name="mlstm_scan",
    )(gate_bias, qk, qk, v, ig, fg)


def _block_diag(w, n):
    r, c = w.shape[-2:]
    out = jnp.zeros(w.shape[:-2] + (n * r, n * c), w.dtype)
    for i in range(n):
        out = out.at[..., i * r:(i + 1) * r, i * c:(i + 1) * c].set(w)
    return out


def _nsa_swa_layer(h, norm_g, w_in, gate_bias, pos_k, pos_v, ck_w1, ck_w2, cv_w1, cv_w2, sinks, w_out):
    b, t, d = h.shape
    m = b * t
    ng = NSA_KV_HEADS
    gk = ng * HEAD_DIM
    nq_a, nq_b = NSA_HEADS * HEAD_DIM, SWA_HEADS * HEAD_DIM
    ngate = NSA_HEADS * 3
    o_kc, o_vc, o_ks, o_vs, o_kw, o_vw = (nq_a + i * gk for i in range(6))
    o_g = nq_a + 6 * gk
    o_qb = o_g + ngate
    o_kb, o_vb = o_qb + nq_b, o_qb + nq_b + gk
    cols = lambda o, n: w_in[:, o:o + n]
    cat = lambda parts: jnp.concatenate(parts, axis=1).astype(BF16)
    gate_pad = jnp.zeros((d, 16 - NSA_REP * 3), w_in.dtype)
    w_gate = jnp.concatenate([p for i in range(ng) for p in (cols(o_g + i * NSA_REP * 3, NSA_REP * 3), gate_pad)]
                             + [jnp.zeros((d, LANES - 16 * ng), w_in.dtype)], axis=1)
    q_all, k_all, cmp_in, v_big, v_small, gate_t = _proj0(
        h.reshape(m, d), norm_g, cat([cols(0, nq_a), cols(o_qb, nq_b)]),
        cat([cols(o_ks, gk), cols(o_kw, gk), cols(o_kb, gk)]), cat([cols(o_vs, gk), cols(o_vw, gk), cols(o_vb, gk)]),
        cat([cols(o_kc, gk), cols(o_vc, gk), w_gate]), 2 * gk, 16 * ng)
    tk = ROW_TILE
    q_all = q_all.reshape(b, t, nq_a + nq_b)
    k_all = k_all.reshape(b, t, 3 * gk)
    v_big = v_big.reshape(b, t // tk, gk, tk)
    v_small = v_small.reshape(3, b, t // Q_BLOCK, gk, Q_BLOCK)
    gate_t = gate_t.reshape(b, t // ROW_TILE, 16 * ng, ROW_TILE)
    gb = jnp.pad(gate_bias.reshape(ng, NSA_REP * 3), ((0, 0), (0, 16 - NSA_REP * 3))).reshape(ng, 16, 1)

    tile2 = lambda p: jnp.tile(p[:, None, :], (1, 1, ng))
    pos = jnp.stack([tile2(pos_k), tile2(pos_v)])
    per_pos = lambda w: w.reshape(CMP_LEN, HEAD_DIM, HEAD_DIM)
    w1 = _block_diag(jnp.stack([per_pos(ck_w1), per_pos(cv_w1)]), ng).astype(BF16)
    w2 = _block_diag(jnp.stack([ck_w2, cv_w2]), ng).astype(BF16)
    cmp_rows, cmp_cols = _compress(cmp_in.reshape(b, t, 2 * gk), pos, w1, w2)

    n_sel = t // SEL_LEN
    ns_pad = -(-n_sel // LANES) * LANES
    sel_map_t = jnp.asarray(_selection_map(t, t // CMP_STRIDE, ns_pad).T, BF16)
    mask_cols = jnp.asarray(_block_mask_cols(t, ns_pad), BF16)
    o_a, o_b = _attention(q_all, k_all, v_big, v_small, gate_t, gb, cmp_rows, cmp_cols, sel_map_t, mask_cols,
                          sinks, n_sel)

    w_o = w_out.astype(BF16)
    out = _out_proj(h.reshape(m, d), [o_a.reshape(m, nq_a), o_b.reshape(m, nq_b)], [w_o[:nq_a], w_o[nq_a:]])
    return out.reshape(b, t, d)


def _mlstm_layer(h, norm_g, w_in, conv_w, conv_b, igate_bias, fgate_bias, w_out):
    b, t, d = h.shape
    m = b * t
    nqk = 2 * MLSTM_HEADS * MLSTM_QK_DIM
    nv = MLSTM_HEADS * MLSTM_V_DIM
    w = w_in.astype(BF16)
    w_gates = jnp.concatenate([w[:, nqk + 2 * nv:], jnp.zeros((d, LANES - 2 * MLSTM_HEADS), BF16)], axis=1)
    qk_raw, v, og, gates = _norm_proj(
        h.reshape(m, d), norm_g,
        [w[:, :nqk], w[:, nqk:nqk + nv], w[:, nqk + nv:nqk + 2 * nv], w_gates],
        [F32, BF16, F32, F32])
    qk = _conv_silu(qk_raw.reshape(b, t, nqk), conv_w, conv_b)
    gates = gates.reshape(b, t, -1)
    ig = gates[..., :MLSTM_HEADS].transpose(0, 2, 1)
    fg = gates[..., MLSTM_HEADS:2 * MLSTM_HEADS].transpose(0, 2, 1)
    hh = _mlstm(qk, v.reshape(b, t, nv), ig, fg, jnp.stack([igate_bias, fgate_bias]))
    out = _out_proj(h.reshape(m, d), [hh.reshape(m, nv)], [w_out.astype(BF16)], gate=og)
    return out.reshape(b, t, d)


def kernel(x, norm_mix, norm_ffn, ffn_w_gate, ffn_w_up, ffn_w_down, ab_w_in, ab_gate_bias, nsa_pos_k, nsa_pos_v, nsa_cmp_k_w1, nsa_cmp_k_w2, nsa_cmp_v_w1, nsa_cmp_v_w2, swa_sinks, ab_w_out, c_w_in, c_conv_w, c_conv_b, c_igate_bias, c_fgate_bias, c_w_out, final_norm):
    depth = norm_mix.shape[0]
    b, t, d = x.shape
    h = x
    for layer in range(depth):
        j = layer // 2
        if layer % 2 == 0:
            h = _nsa_swa_layer(h, norm_mix[layer], ab_w_in[j], ab_gate_bias[j], nsa_pos_k[j], nsa_pos_v[j],
                               nsa_cmp_k_w1[j], nsa_cmp_k_w2[j], nsa_cmp_v_w1[j], nsa_cmp_v_w2[j],
                               swa_sinks[j], ab_w_out[j])
        else:
            h = _mlstm_layer(h, norm_mix[layer], c_w_in[j], c_conv_w[j], c_conv_b[j], c_igate_bias[j],
                             c_fgate_bias[j], c_w_out[j])
        last = layer == depth - 1
        h = _ffn(h.reshape(b * t, d), norm_ffn[layer], ffn_w_gate[layer].astype(BF16),
                 ffn_w_up[layer].astype(BF16), ffn_w_down[layer].astype(BF16),
                 final_norm if last else None).reshape(b, t, d)
    if depth == 0:
        raise ValueError("depth must be positive")
    return h
```

```python
import functools

import numpy as np
import jax
import jax.numpy as jnp
from jax import lax
from jax.experimental import pallas as pl
from jax.experimental.pallas import tpu as pltpu

HEAD_DIM = 64
NSA_HEADS = 8
NSA_KV_HEADS = 2
NSA_REP = NSA_HEADS // NSA_KV_HEADS
CMP_LEN = 32
CMP_STRIDE = 16
SEL_LEN = 64
SEL_TOPK = 16
NSA_WINDOW = 512
SWA_HEADS = 8
SWA_KV_HEADS = 2
SWA_REP = SWA_HEADS // SWA_KV_HEADS
SWA_WINDOW = 128
Q_BLOCK = 128
MLSTM_HEADS = 4
MLSTM_QK_DIM = 128
MLSTM_V_DIM = 256
MLSTM_CHUNK = 64
CONV_WIDTH = 4
RMS_EPS = 1e-6
NEG_INF = -1e30
FORCE_SCORE = 1e9
BELOW_NEG_INF = -3e38
LOG2_E = 1.4426950408889634
SEL_SHIFT = SEL_LEN.bit_length() - 1
assert 1 << SEL_SHIFT == SEL_LEN

LANES = 128
SUBLANES = 8
VMEM_LIMIT_BYTES = 56 * 1024 * 1024

ROW_TILE = 512
SCAN_CHUNK = 256
SCAN_ROWS = 512

F32 = jnp.float32
BF16 = jnp.bfloat16


def _cparams(n_axes):
    return pltpu.CompilerParams(
        dimension_semantics=("arbitrary",) * n_axes,
        vmem_limit_bytes=VMEM_LIMIT_BYTES,
    )


def _resident(shape):
    nd = len(shape)
    return pl.BlockSpec(shape, lambda *_: (0,) * nd)


def _dot(a, b):
    return jnp.dot(a, b, preferred_element_type=F32)


def _sigmoid(x):
    return 1.0 / (1.0 + jnp.exp(-x))


def _silu(x):
    return x * _sigmoid(x)


def _rmsnorm(x, g):
    y = x * lax.rsqrt(jnp.mean(x * x, axis=-1, keepdims=True) + RMS_EPS)
    return y * g


def _norm_proj_kernel(n_out, x_ref, g_ref, *refs):
    w_refs, o_refs = refs[:n_out], refs[n_out:]
    xn = _rmsnorm(x_ref[...], g_ref[...]).astype(BF16)
    for w_ref, o_ref in zip(w_refs, o_refs):
        o_ref[...] = _dot(xn, w_ref[...]).astype(o_ref.dtype)


def _norm_proj(x, g, ws, out_dtypes):
    m, d = x.shape
    n_out = len(ws)
    return pl.pallas_call(
        functools.partial(_norm_proj_kernel, n_out),
        grid=(m // ROW_TILE,),
        in_specs=[pl.BlockSpec((ROW_TILE, d), lambda i: (i, 0)), _resident((1, d))]
        + [_resident(w.shape) for w in ws],
        out_specs=[pl.BlockSpec((ROW_TILE, w.shape[1]), lambda i: (i, 0)) for w in ws],
        out_shape=[jax.ShapeDtypeStruct((m, w.shape[1]), dt) for w, dt in zip(ws, out_dtypes)],
        compiler_params=_cparams(1),
        name="norm_proj",
    )(x, g.reshape(1, d), *ws)


def _proj0_kernel(x_ref, g_ref, wq_ref, wk_ref, wv_ref, wc_ref,
                  q_ref, k_ref, c_ref, vbig_ref, vsmall_ref, gt_ref):
    xn = _rmsnorm(x_ref[...], g_ref[...]).astype(BF16)
    q_ref[...] = _dot(xn, wq_ref[...]).astype(q_ref.dtype)
    k_ref[...] = _dot(xn, wk_ref[...]).astype(k_ref.dtype)
    cg = _dot(xn, wc_ref[...])
    c_ref[...] = cg[:, :c_ref.shape[1]]
    gt_ref[0] = cg[:, c_ref.shape[1]:].T[:gt_ref.shape[1]]
    v = _dot(xn, wv_ref[...])
    width = v.shape[1] // 3
    for a in range(3):
        v_t = v[:, a * width:(a + 1) * width].T.astype(BF16)
        if a == 0:
            vbig_ref[0] = v_t
        for j in range(x_ref.shape[0] // Q_BLOCK):
            vsmall_ref[a, j] = v_t[:, j * Q_BLOCK:(j + 1) * Q_BLOCK]


def _proj0(x, g, w_q, w_k, w_v, w_c, n_cmp_cols, gate_rows):
    m, d = x.shape
    steps = m // ROW_TILE
    tiles = ROW_TILE // Q_BLOCK
    gd = w_v.shape[1] // 3
    row = lambda w: pl.BlockSpec((ROW_TILE, w), lambda i: (i, 0))
    return pl.pallas_call(
        _proj0_kernel,
        grid=(steps,),
        in_specs=[row(d), _resident((1, d))] + [_resident(w.shape) for w in (w_q, w_k, w_v, w_c)],
        out_specs=[
            row(w_q.shape[1]), row(w_k.shape[1]), row(n_cmp_cols),
            pl.BlockSpec((1, gd, ROW_TILE), lambda i: (i, 0, 0)),
            pl.BlockSpec((3, tiles, gd, Q_BLOCK), lambda i: (0, i, 0, 0)),
            pl.BlockSpec((1, gate_rows, ROW_TILE), lambda i: (i, 0, 0)),
        ],
        out_shape=[
            jax.ShapeDtypeStruct((m, w_q.shape[1]), BF16),
            jax.ShapeDtypeStruct((m, w_k.shape[1]), BF16),
            jax.ShapeDtypeStruct((m, n_cmp_cols), F32),
            jax.ShapeDtypeStruct((steps, gd, ROW_TILE), BF16),
            jax.ShapeDtypeStruct((3, steps * tiles, gd, Q_BLOCK), BF16),
            jax.ShapeDtypeStruct((steps, gate_rows, ROW_TILE), F32),
        ],
        compiler_params=_cparams(1),
        name="norm_proj_attn",
    )(x, g.reshape(1, d), w_q, w_k, w_v, w_c)


def _mix_ffn_kernel(n_in, gated, final, res_ref, *refs):
    a_refs, w_refs = refs[:n_in], refs[n_in:2 * n_in]
    rest = list(refs[2 * n_in:])
    o_ref = rest.pop()
    gate_ref = rest.pop(0) if gated else None
    g_ref, wg_ref, wu_ref, wd_ref = rest[:4]
    h = res_ref[...]
    for idx, (a_ref, w_ref) in enumerate(zip(a_refs, w_refs)):
        a = a_ref[...]
        if gated and idx == 0:
            a = (a.astype(F32) * _sigmoid(gate_ref[...].astype(F32))).astype(BF16)
        h = h + _dot(a, w_ref[...])
    xn = _rmsnorm(h, g_ref[...]).astype(BF16)
    act = (_silu(_dot(xn, wg_ref[...])) * _dot(xn, wu_ref[...])).astype(BF16)
    out = h + _dot(act, wd_ref[...])
    if final:
        out = _rmsnorm(out, rest[4][...])
    o_ref[...] = out


def _mix_ffn(res, a_list, w_list, gate, g, wg, wu, wd, final_g=None):
    m, d = res.shape
    row = lambda w: pl.BlockSpec((ROW_TILE, w), lambda i: (i, 0))
    once = lambda a: pl.BlockSpec(a.shape, lambda *_: (0,) * a.ndim, pipeline_mode=pl.Buffered(1))
    in_specs = [row(d)] + [row(a.shape[1]) for a in a_list] + [once(w) for w in w_list]
    args = [res, *a_list, *w_list]
    if gate is not None:
        in_specs.append(row(gate.shape[1]))
        args.append(gate)
    consts = [g.reshape(1, d), wg, wu, wd] + ([final_g.reshape(1, d)] if final_g is not None else [])
    return pl.pallas_call(
        functools.partial(_mix_ffn_kernel, len(a_list), gate is not None, final_g is not None),
        grid=(m // ROW_TILE,),
        in_specs=in_specs + [once(c) for c in consts],
        out_specs=row(d),
        out_shape=jax.ShapeDtypeStruct((m, d), F32),
        compiler_params=_cparams(1),
        name="mix_out_ffn",
    )(*args, *consts)


def _compress_kernel(x_ref, p_ref, w1_ref, w2_ref, rows_ref, cols_ref):
    n = x_ref.shape[1] // CMP_STRIDE
    a = jnp.zeros((n, x_ref.shape[2]), F32)
    b = jnp.zeros((n, x_ref.shape[2]), F32)
    for r in range(CMP_STRIDE):
        x = x_ref[0, pl.ds(r, n, stride=CMP_STRIDE), :]
        a = a + _dot((x + p_ref[0, r]).astype(BF16), w1_ref[0, r])
        b = b + _dot((x + p_ref[0, CMP_STRIDE + r]).astype(BF16), w1_ref[0, CMP_STRIDE + r])
    h = a + pltpu.roll(b, n - 1, 0)
    out = _dot(_silu(h).astype(BF16), w2_ref[0])
    live = lax.broadcasted_iota(jnp.int32, out.shape, 0) < n - 1
    out = jnp.where(live, out, 0.0)
    rows_ref[0, 0] = out.astype(rows_ref.dtype)
    cols_ref[0, 0] = out.T.astype(cols_ref.dtype)


def _compress(x, pos, w1, w2):
    b, t, two_gd = x.shape
    gd = two_gd // 2
    n = t // CMP_STRIDE
    return pl.pallas_call(
        _compress_kernel,
        grid=(2, b),
        in_specs=[
            pl.BlockSpec((1, t, gd), lambda s, i: (i, 0, s)),
            pl.BlockSpec((1,) + pos.shape[1:], lambda s, i: (s, 0, 0, 0)),
            pl.BlockSpec((1,) + w1.shape[1:], lambda s, i: (s, 0, 0, 0)),
            pl.BlockSpec((1,) + w2.shape[1:], lambda s, i: (s, 0, 0)),
        ],
        out_specs=[pl.BlockSpec((1, 1, n, gd), lambda s, i: (s, i, 0, 0)),
                   pl.BlockSpec((1, 1, gd, n), lambda s, i: (s, i, 0, 0))],
        out_shape=[jax.ShapeDtypeStruct((2, b, n, gd), BF16), jax.ShapeDtypeStruct((2, b, gd, n), BF16)],
        compiler_params=_cparams(2),
        name="compress_kv",
    )(x, pos, w1, w2)


def _stack_heads_t(q_ref, rep):
    qt = q_ref[0].astype(F32).T
    return jnp.concatenate([qt[r * HEAD_DIM:(r + 1) * HEAD_DIM] for r in range(rep)], axis=1).astype(BF16)


def _unstack_heads_t(o_t, rep):
    tq = o_t.shape[1] // rep
    return jnp.concatenate([o_t[:, r * tq:(r + 1) * tq] for r in range(rep)], axis=0).T


def _tile_lanes(x, rep):
    return jnp.concatenate([x] * rep, axis=1)


def _col_tree(x, pair, reduce):
    while x.shape[0] > SUBLANES:
        half = x.shape[0] // 2
        x = pair(x[:half], x[half:])
    return reduce(x, axis=0, keepdims=True)


def _attn_kernel(n_sel_blocks, sink_ref, qa_ref, qb_ref, gl_ref, gb_ref, kc_ref, vct_ref, selmap_ref,
                 ksd_ref, ks_ref, mask_ref, vsd_ref, vbig_ref, kw_ref, vw_ref, kbp_ref, kbc_ref, vbp_ref, vbc_ref,
                 eye_ref, cmask_ref, dmask_ref, wmask_ref, bmask_ref, oa_ref, ob_ref, m_sc, acc_sc, s_sc):
    g, n = pl.program_id(1), pl.program_id(2)
    rep = NSA_REP
    n_groups = ksd_ref.shape[2] // HEAD_DIM
    grp = pl.ds(pl.multiple_of(g * HEAD_DIM, HEAD_DIM), HEAD_DIM)
    tq_row = n * Q_BLOCK + lax.broadcasted_iota(jnp.int32, (1, Q_BLOCK), 1)

    def pad_q(q_t):
        return jnp.concatenate([jnp.where(g == i, q_t, jnp.zeros_like(q_t)) for i in range(n_groups)], axis=0)

    def masked_logits(keys, mask, q_rows):
        return _dot(jnp.concatenate([keys, mask], axis=1), jnp.concatenate([q_rows, eye_ref[...]], axis=0))

    def with_ones(v_t):
        return jnp.concatenate([v_t, jnp.ones((SUBLANES, v_t.shape[1]), v_t.dtype)], axis=0)

    q_t = pad_q(_stack_heads_t(qa_ref, rep))
    kc = kc_ref[0, 0]
    n_cmp = kc.shape[0]
    odd = lax.rem(n, 2)
    shift = Q_BLOCK // CMP_STRIDE
    start = pl.multiple_of(n_cmp - shift * (n + odd), 2 * SUBLANES)
    s_c = masked_logits(kc, cmask_ref[odd, pl.ds(start, n_cmp), :], q_t)
    q_b = pad_q(_stack_heads_t(qb_ref, SWA_REP))
    k_b = jnp.concatenate([kbp_ref[0], kbc_ref[0]], axis=0)
    s_b = masked_logits(k_b, bmask_ref[jnp.minimum(n, bmask_ref.shape[0] - 1)], q_b)
    n_win_tiles = NSA_WINDOW // Q_BLOCK + 1
    tiles = [jnp.maximum(n - (n_win_tiles - 1) + i, 0) for i in range(n_win_tiles)]
    k_w = jnp.concatenate([kw_ref[0, pl.ds(pl.multiple_of(j * Q_BLOCK, Q_BLOCK), Q_BLOCK), :] for j in tiles], axis=0)
    s_w = masked_logits(k_w, wmask_ref[jnp.minimum(n, wmask_ref.shape[0] - 1)], q_t)
    s_d = masked_logits(ksd_ref[0], dmask_ref[...], q_t)

    m = jnp.max(s_c, axis=0, keepdims=True)
    lhs = jnp.concatenate([with_ones(vct_ref[0, 0, grp, :]), selmap_ref[...]], axis=0)
    oc_imp = _dot(lhs, jnp.exp2(s_c - m).astype(BF16))
    den = oc_imp[HEAD_DIM:HEAD_DIM + 1]
    seen = m > 0.5 * NEG_INF
    oc_imp = jnp.where(seen, oc_imp * (1.0 / jnp.where(seen, den, 1.0)), 0.0)
    o_c = oc_imp[:HEAD_DIM]
    imp = oc_imp[HEAD_DIM + SUBLANES:]
    imp = sum(imp[:, r * Q_BLOCK:(r + 1) * Q_BLOCK] for r in range(rep))

    sink_row = jnp.concatenate([jnp.full((1, Q_BLOCK), sink_ref[g * SWA_REP + r] * LOG2_E, F32)
                                for r in range(SWA_REP)], axis=1)
    v_tiles_b = [vbp_ref[0, 0, 0, grp, :], vbc_ref[0, 0, 0, grp, :]]
    v_tiles_w = [vw_ref[0, 0, j, grp, :] for j in tiles]
    v_tiles_d = [vsd_ref[0, 0, 0, grp, :]]
    maxes, partial = {}, {"b": [], "w": [], "d": []}

    def col_max(name, s, extra=None):
        def run():
            m = jnp.max(s, axis=0, keepdims=True)
            maxes[name] = m if extra is None else jnp.maximum(m, extra)
        return run

    def pv_tile(name, s, v_tiles, i):
        def run():
            rows = slice(i * Q_BLOCK, (i + 1) * Q_BLOCK)
            partial[name].append(_dot(with_ones(v_tiles[i]), jnp.exp2(s[rows] - maxes[name]).astype(BF16)))
        return run

    pieces = ([col_max("w", s_w), col_max("b", s_b, sink_row), col_max("d", s_d)]
              + [pv_tile("w", s_w, v_tiles_w, i) for i in range(len(v_tiles_w))]
              + [pv_tile("b", s_b, v_tiles_b, i) for i in range(len(v_tiles_b))]
              + [pv_tile("d", s_d, v_tiles_d, 0)])

    ns_pad = imp.shape[0]
    blk = lax.broadcasted_iota(jnp.int32, (ns_pad, Q_BLOCK), 0)
    blk_f = blk.astype(F32)
    cur = jnp.right_shift(tq_row, SEL_SHIFT)
    forced = (blk == 0) | (blk == cur) | (blk == cur - 1)
    valid = jnp.left_shift(blk, SEL_SHIFT) <= tq_row
    score = jnp.where(valid, imp, NEG_INF)
    score = jnp.where(forced | (blk >= n_sel_blocks), BELOW_NEG_INF, score)
    sel = forced
    for _ in range(min(SEL_TOPK, n_sel_blocks) - 3):
        if pieces:
            pieces.pop(0)()
        top = _col_tree(score, jnp.maximum, jnp.max)
        first_hit = _col_tree(jnp.where(score == top, blk_f, float(ns_pad)), jnp.minimum, jnp.min)
        pick = blk_f == first_hit
        sel = sel | pick
        score = jnp.where(pick, BELOW_NEG_INF, score)
    n_diag_blk = n * (Q_BLOCK // SEL_LEN)
    unselected = jnp.where(sel & (blk < n_diag_blk), 0.0, 1.0).astype(BF16)

    while pieces:
        pieces.pop(0)()

    def finish(name, sink=None):
        acc = sum(partial[name][1:], partial[name][0])
        den = acc[HEAD_DIM:HEAD_DIM + 1]
        if sink is not None:
            den = den + jnp.exp2(sink - maxes[name])
        return acc[:HEAD_DIM] * (1.0 / den)

    ob_ref[0] = _unstack_heads_t(finish("b", sink_row), SWA_REP).astype(ob_ref.dtype)
    o_w = finish("w")
    m_sc[...] = maxes["d"]
    acc_sc[...] = partial["d"][0]

    q_aug = jnp.concatenate([_tile_lanes(unselected, rep), q_t], axis=0)
    tk = s_sc.shape[1]
    n_tiles = ks_ref.shape[1] // tk

    def logits(kt):
        keys = pl.ds(pl.multiple_of(kt * tk, tk), tk)
        return _dot(jnp.concatenate([mask_ref[keys, :], ks_ref[0, keys, :]], axis=1), q_aug)

    def absorb(kt, s):
        m_old = m_sc[...]
        m_new = jnp.maximum(m_old, jnp.max(s, axis=0, keepdims=True))
        pv = _dot(with_ones(vbig_ref[0, kt, grp, :]), jnp.exp2(s - m_new).astype(BF16))
        acc_sc[...] = jnp.exp2(m_old - m_new) * acc_sc[...] + pv
        m_sc[...] = m_new

    s_sc[0] = logits(0)

    def tile_pair(j, carry):
        s_sc[1] = logits(2 * j + 1)
        absorb(2 * j, s_sc[0])
        s_sc[0] = logits(jnp.minimum(2 * j + 2, n_tiles - 1))
        absorb(2 * j + 1, s_sc[1])
        return carry

    lax.fori_loop(0, lax.div(n * Q_BLOCK + 2 * tk - 1, 2 * tk), tile_pair, 0)
    acc = acc_sc[...]
    o_s = acc[:HEAD_DIM] * (1.0 / acc[HEAD_DIM:HEAD_DIM + 1])

    gates = _sigmoid(gl_ref[0, 0] + gb_ref[0])
    gate = lambda c: jnp.concatenate([gates[3 * r + c:3 * r + c + 1] for r in range(rep)], axis=1)
    o = gate(0) * o_c + gate(1) * o_s + gate(2) * o_w
    oa_ref[0] = _unstack_heads_t(o, rep).astype(oa_ref.dtype)


def _selection_map(t, nc_pad, ns_pad):
    nc = (t - CMP_LEN) // CMP_STRIDE + 1
    ns = t // SEL_LEN
    cs = np.arange(nc, dtype=np.int64)[:, None] * CMP_STRIDE
    ss = np.arange(ns, dtype=np.int64)[None, :] * SEL_LEN
    out = np.zeros((nc_pad, ns_pad), np.float32)
    out[:nc, :ns] = (cs < ss + SEL_LEN) & (cs + CMP_LEN > ss)
    return out


def _block_mask_cols(t, ns_pad):
    hit = np.arange(ns_pad, dtype=np.int64)[None, :] == (np.arange(t, dtype=np.int64) // SEL_LEN)[:, None]
    return np.where(hit, NEG_INF, 0.0).astype(np.float32)


def _band_masks(n_rows, window):
    n_front = n_rows // Q_BLOCK - 1
    r = np.arange(n_rows)[None, :, None]
    tok = np.arange(Q_BLOCK)[None, None, :]
    v = np.arange(n_front + 1)[:, None, None]
    dist = n_front * Q_BLOCK + tok - r
    ok = (dist >= 0) & (dist < window) & (r >= (n_front - v) * Q_BLOCK)
    return np.where(ok, 0.0, NEG_INF).astype(np.float32)


def _compressed_masks(n_cmp):
    shift = Q_BLOCK // CMP_STRIDE
    u = np.arange(2 * n_cmp)[None, :, None] + shift * np.arange(2)[:, None, None]
    tok = np.arange(Q_BLOCK)[None, None, :]
    ok = CMP_STRIDE * (u - n_cmp) + CMP_LEN - 1 <= tok
    return np.where(ok, 0.0, NEG_INF).astype(np.float32)


def _attention(q, k_all, v_big, v_small, gate_t, gate_bias, cmp_rows, cmp_cols, sel_map_t, mask_cols, sinks,
               n_sel_blocks):
    b, t = q.shape[:2]
    gd = k_all.shape[2] // 3
    g = gd // HEAD_DIM
    tk = v_big.shape[3]
    assert (t // tk) % 2 == 0, "selected-branch key tiles are consumed in pairs"
    width = NSA_REP * HEAD_DIM
    gate_cols = gate_t.shape[3] // Q_BLOCK
    prev = lambda n: jnp.maximum(n - 1, 0)
    cur = lambda n: n
    k_tile = lambda blk, at: pl.BlockSpec((1, Q_BLOCK, gd), lambda i, j, n: (i, at(n), blk))
    k_all_rows = lambda blk: pl.BlockSpec((1, t, gd), lambda i, j, n: (i, 0, blk))
    v_tile = lambda a, at: pl.BlockSpec((1, 1, 1, gd, Q_BLOCK), lambda i, j, n: (a, i, at(n), 0, 0))
    out_spec = pl.BlockSpec((1, Q_BLOCK, width), lambda i, j, n: (i, n, j))
    out_shape = jax.ShapeDtypeStruct((b, t, g * width), BF16)
    tri = np.where(np.arange(Q_BLOCK)[:, None] <= np.arange(Q_BLOCK)[None, :], 0.0, NEG_INF)
    consts = [jnp.asarray(c, BF16) for c in (
        np.tile(np.eye(Q_BLOCK, dtype=np.float32), (1, NSA_REP)),
        _compressed_masks(cmp_rows.shape[2]), tri,
        _band_masks(NSA_WINDOW + Q_BLOCK, NSA_WINDOW), _band_masks(SWA_WINDOW + Q_BLOCK, SWA_WINDOW))]
    return pl.pallas_call(
        functools.partial(_attn_kernel, n_sel_blocks),
        grid=(b, g, t // Q_BLOCK),
        in_specs=[
            pl.BlockSpec(memory_space=pltpu.SMEM),
            pl.BlockSpec((1, Q_BLOCK, width), lambda i, j, n: (i, n, j)),
            pl.BlockSpec((1, Q_BLOCK, width), lambda i, j, n: (i, n, g + j)),
            pl.BlockSpec((1, 1, 16, Q_BLOCK), lambda i, j, n: (i, n // gate_cols, j, n % gate_cols)),
            pl.BlockSpec((1, 16, 1), lambda i, j, n: (j, 0, 0)),
            pl.BlockSpec((1, 1) + cmp_rows.shape[2:], lambda i, j, n: (0, i, 0, 0)),
            pl.BlockSpec((1, 1) + cmp_cols.shape[2:], lambda i, j, n: (1, i, 0, 0)),
            _resident(sel_map_t.shape),
            k_tile(0, cur), k_all_rows(0), _resident(mask_cols.shape),
            v_tile(0, cur), pl.BlockSpec((1,) + v_big.shape[1:], lambda i, j, n: (i, 0, 0, 0)),
            k_all_rows(1), pl.BlockSpec((1, 1) + v_small.shape[2:], lambda i, j, n: (1, i, 0, 0, 0)),
            k_tile(2, prev), k_tile(2, cur), v_tile(2, prev), v_tile(2, cur),
        ] + [_resident(c.shape) for c in consts],
        out_specs=[out_spec, out_spec],
        out_shape=[out_shape, out_shape],
        scratch_shapes=[
            pltpu.VMEM((1, NSA_REP * Q_BLOCK), F32),
            pltpu.VMEM((HEAD_DIM + SUBLANES, NSA_REP * Q_BLOCK), F32),
            pltpu.VMEM((2, tk, NSA_REP * Q_BLOCK), F32),
        ],
        compiler_params=_cparams(3),
        name="nsa_swa_attention",
    )(sinks, q, q, gate_t, gate_bias, cmp_rows, cmp_cols, sel_map_t, k_all, k_all, mask_cols, v_small, v_big,
      k_all, v_small, k_all, k_all, v_small, v_small, *consts)


def _conv_silu_kernel(n_q_cols, x_ref, prev_ref, w_ref, b_ref, o_ref, buf):
    i = pl.program_id(1)
    rows = x_ref.shape[1]
    halo = prev_ref.shape[1]
    buf[:halo, :] = jnp.where(i > 0, prev_ref[0].astype(F32), 0.0)
    buf[halo:, :] = x_ref[0].astype(F32)
    y = b_ref[...]
    for j in range(CONV_WIDTH):
        start = halo - (CONV_WIDTH - 1) + j
        y = y + w_ref[j:j + 1, :] * buf[start:start + rows, :]
    y = _silu(y)
    col = lax.broadcasted_iota(jnp.int32, y.shape, 1)
    o_ref[0] = jnp.where(col >= n_q_cols, y * (MLSTM_QK_DIM ** -0.5), y).astype(o_ref.dtype)


def _conv_silu(x, w, bias):
    b, t, c = x.shape
    halo = 2 * SUBLANES
    ratio = ROW_TILE // halo
    return pl.pallas_call(
        functools.partial(_conv_silu_kernel, c // 2),
        grid=(b, t // ROW_TILE),
        in_specs=[
            pl.BlockSpec((1, ROW_TILE, c), lambda i, j: (i, j, 0)),
            pl.BlockSpec((1, halo, c), lambda i, j: (i, jnp.maximum(j * ratio - 1, 0), 0)),
            _resident(w.shape),
            _resident((1, c)),
        ],
        out_specs=pl.BlockSpec((1, ROW_TILE, c), lambda i, j: (i, j, 0)),
        out_shape=jax.ShapeDtypeStruct((b, t, c), BF16),
        scratch_shapes=[pltpu.VMEM((ROW_TILE + halo, c), F32)],
        compiler_params=_cparams(2),
        name="conv_silu",
    )(x, x, w, bias.reshape(1, c))


def _mlstm_kernel(bias_ref, q_ref, k_ref, v_ref, ig_ref, fg_ref, o_ref, state, m_state):
    c = pl.program_id(1)
    L = SCAN_CHUNK
    dqk, dv = MLSTM_QK_DIM, MLSTM_V_DIM

    @pl.when(c == 0)
    def _():
        state[...] = jnp.zeros(state.shape, F32)
        m_state[...] = jnp.zeros(m_state.shape, F32)

    t_idx = lax.broadcasted_iota(jnp.int32, (L, L), 0)
    s_idx = lax.broadcasted_iota(jnp.int32, (L, L), 1)
    causal = s_idx <= t_idx
    eye = s_idx == t_idx
    ones_col = (lax.broadcasted_iota(jnp.int32, (L, LANES), 1) == 0).astype(BF16)

    def to_col(row):
        return jnp.sum(jnp.where(eye, row, 0.0), axis=1, keepdims=True)

    for j in range(SCAN_ROWS // SCAN_CHUNK):
        rows = slice(j * L, (j + 1) * L)
        for h in range(MLSTM_HEADS):
            qc = q_ref[0, rows, h * dqk:(h + 1) * dqk]
            kc = k_ref[0, rows, h * dqk:(h + 1) * dqk]
            va = jnp.concatenate([v_ref[0, rows, h * dv:(h + 1) * dv], ones_col], axis=1)
            i_row = ig_ref[0, h:h + 1, rows] + bias_ref[0, h]
            f_pre = fg_ref[0, h:h + 1, rows] + bias_ref[1, h]
            f_row = jnp.minimum(f_pre, 0.0) - jnp.log(1.0 + jnp.exp(-jnp.abs(f_pre)))
            f_col = to_col(f_row)
            b_col = jnp.sum(jnp.where(causal, f_row, 0.0), axis=1, keepdims=True)
            b_row = jnp.sum(jnp.where(t_idx <= s_idx, f_col, 0.0), axis=0, keepdims=True)
            m_old = m_state[h:h + 1, 0:1]
            a = b_col + m_old
            d = jnp.where(causal, b_col - b_row + i_row, -jnp.inf)
            mt = jnp.maximum(a, jnp.max(d, axis=1, keepdims=True))
            w_inter = jnp.exp(a - mt)
            sm = lax.dot_general(qc, kc, (((1,), (1,)), ((), ())), preferred_element_type=F32) * jnp.exp(d - mt)
            st = state[h]
            nd = w_inter * _dot(qc, st.astype(BF16)) + _dot(sm.astype(BF16), va)
            hc = nd[:, :dv] * (1.0 / jnp.maximum(jnp.abs(nd[:, dv:dv + 1]), jnp.exp(-mt)))
            o_ref[0, rows, h * dv:(h + 1) * dv] = hc.astype(o_ref.dtype)
            b_last = b_row[:, L - 1:L]
            wl = b_last - b_row + i_row
            m_new = jnp.maximum(b_last + m_old, jnp.max(wl, axis=1, keepdims=True))
            decay = jnp.exp(b_last + m_old - m_new)
            w_col = to_col(jnp.exp(wl - m_new))
            kw = (kc.astype(F32) * w_col).astype(BF16)
            upd = lax.dot_general(kw, va, (((0,), (0,)), ((), ())), preferred_element_type=F32)
            state[h] = decay * st + upd
            m_state[h:h + 1, 0:1] = m_new


def _mlstm(qk, v, ig, fg, gate_bias):
    b, t = qk.shape[:2]
    nh = MLSTM_HEADS
    rows = SCAN_ROWS
    gate_spec = pl.BlockSpec((1, nh, rows), lambda i, c: (i, 0, c))
    return pl.pallas_call(
        _mlstm_kernel,
        grid=(b, t // rows),
        in_specs=[
            pl.BlockSpec(memory_space=pltpu.SMEM),
            pl.BlockSpec((1, rows, nh * MLSTM_QK_DIM), lambda i, c: (i, c, 0)),
            pl.BlockSpec((1, rows, nh * MLSTM_QK_DIM), lambda i, c: (i, c, 1)),
            pl.BlockSpec((1, rows, nh * MLSTM_V_DIM), lambda i, c: (i, c, 0)),
            gate_spec, gate_spec,
        ],
        out_specs=pl.BlockSpec((1, rows, nh * MLSTM_V_DIM), lambda i, c: (i, c, 0)),
        out_shape=jax.ShapeDtypeStruct((b, t, nh * MLSTM_V_DIM), BF16),
        scratch_shapes=[
            pltpu.VMEM((nh, MLSTM_QK_DIM, MLSTM_V_DIM + LANES), F32),
            pltpu.VMEM((SUBLANES, LANES), F32),
        ],
        compiler_params=_cparams(2),
        name="mlstm_scan",
    )(gate_bias, qk, qk, v, ig, fg)


def _block_diag(w, n):
    r, c = w.shape[-2:]
    out = jnp.zeros(w.shape[:-2] + (n * r, n * c), w.dtype)
    for i in range(n):
        out = out.at[..., i * r:(i + 1) * r, i * c:(i + 1) * c].set(w)
    return out


def _nsa_swa_layer(h, norm_g, w_in, gate_bias, pos_k, pos_v, ck_w1, ck_w2, cv_w1, cv_w2, sinks, w_out):
    b, t, d = h.shape
    m = b * t
    ng = NSA_KV_HEADS
    gk = ng * HEAD_DIM
    nq_a, nq_b = NSA_HEADS * HEAD_DIM, SWA_HEADS * HEAD_DIM
    ngate = NSA_HEADS * 3
    o_kc, o_vc, o_ks, o_vs, o_kw, o_vw = (nq_a + i * gk for i in range(6))
    o_g = nq_a + 6 * gk
    o_qb = o_g + ngate
    o_kb, o_vb = o_qb + nq_b, o_qb + nq_b + gk
    cols = lambda o, n: w_in[:, o:o + n]
    cat = lambda parts: jnp.concatenate(parts, axis=1).astype(BF16)
    q_scale = HEAD_DIM ** -0.5 * LOG2_E
    gate_pad = jnp.zeros((d, 16 - NSA_REP * 3), w_in.dtype)
    w_gate = jnp.concatenate([p for i in range(ng) for p in (cols(o_g + i * NSA_REP * 3, NSA_REP * 3), gate_pad)]
                             + [jnp.zeros((d, LANES - 16 * ng), w_in.dtype)], axis=1)
    q_all, k_all, cmp_in, v_big, v_small, gate_t = _proj0(
        h.reshape(m, d), norm_g, (jnp.concatenate([cols(0, nq_a), cols(o_qb, nq_b)], axis=1) * q_scale).astype(BF16),
        cat([cols(o_ks, gk), cols(o_kw, gk), cols(o_kb, gk)]), cat([cols(o_vs, gk), cols(o_vw, gk), cols(o_vb, gk)]),
        cat([cols(o_kc, gk), cols(o_vc, gk), w_gate]), 2 * gk, 16 * ng)
    tk = ROW_TILE
    q_all = q_all.reshape(b, t, nq_a + nq_b)
    k_all = k_all.reshape(b, t, 3 * gk)
    v_big = v_big.reshape(b, t // tk, gk, tk)
    v_small = v_small.reshape(3, b, t // Q_BLOCK, gk, Q_BLOCK)
    gate_t = gate_t.reshape(b, t // ROW_TILE, 16 * ng, ROW_TILE)
    gb = jnp.pad(gate_bias.reshape(ng, NSA_REP * 3), ((0, 0), (0, 16 - NSA_REP * 3))).reshape(ng, 16, 1)

    tile2 = lambda p: jnp.tile(p[:, None, :], (1, 1, ng))
    pos = jnp.stack([tile2(pos_k), tile2(pos_v)])
    per_pos = lambda w: w.reshape(CMP_LEN, HEAD_DIM, HEAD_DIM)
    w1 = _block_diag(jnp.stack([per_pos(ck_w1), per_pos(cv_w1)]), ng).astype(BF16)
    w2 = _block_diag(jnp.stack([ck_w2, cv_w2]), ng).astype(BF16)
    cmp_rows, cmp_cols = _compress(cmp_in.reshape(b, t, 2 * gk), pos, w1, w2)

    n_sel = t // SEL_LEN
    ns_pad = -(-n_sel // LANES) * LANES
    sel_map_t = jnp.asarray(_selection_map(t, t // CMP_STRIDE, ns_pad).T, BF16)
    mask_cols = jnp.asarray(_block_mask_cols(t, ns_pad), BF16)
    o_a, o_b = _attention(q_all, k_all, v_big, v_small, gate_t, gb, cmp_rows, cmp_cols, sel_map_t, mask_cols,
                          sinks, n_sel)

    w_o = w_out.astype(BF16)
    return [o_a.reshape(m, nq_a), o_b.reshape(m, nq_b)], [w_o[:nq_a], w_o[nq_a:]], None


def _mlstm_layer(h, norm_g, w_in, conv_w, conv_b, igate_bias, fgate_bias, w_out):
    b, t, d = h.shape
    m = b * t
    nqk = 2 * MLSTM_HEADS * MLSTM_QK_DIM
    nv = MLSTM_HEADS * MLSTM_V_DIM
    w = w_in.astype(BF16)
    w_gates = jnp.concatenate([w[:, nqk + 2 * nv:], jnp.zeros((d, LANES - 2 * MLSTM_HEADS), BF16)], axis=1)
    qk_raw, v, og, gates = _norm_proj(
        h.reshape(m, d), norm_g,
        [w[:, :nqk], w[:, nqk:nqk + nv], w[:, nqk + nv:nqk + 2 * nv], w_gates],
        [BF16, BF16, BF16, F32])
    qk = _conv_silu(qk_raw.reshape(b, t, nqk), conv_w, conv_b)
    gates = gates.reshape(b, t, -1)
    ig = gates[..., :MLSTM_HEADS].transpose(0, 2, 1)
    fg = gates[..., MLSTM_HEADS:2 * MLSTM_HEADS].transpose(0, 2, 1)
    hh = _mlstm(qk, v.reshape(b, t, nv), ig, fg, jnp.stack([igate_bias, fgate_bias]))
    return [hh.reshape(m, nv)], [w_out.astype(BF16)], og


def kernel(x, norm_mix, norm_ffn, ffn_w_gate, ffn_w_up, ffn_w_down, ab_w_in, ab_gate_bias, nsa_pos_k, nsa_pos_v, nsa_cmp_k_w1, nsa_cmp_k_w2, nsa_cmp_v_w1, nsa_cmp_v_w2, swa_sinks, ab_w_out, c_w_in, c_conv_w, c_conv_b, c_igate_bias, c_fgate_bias, c_w_out, final_norm):
    depth = norm_mix.shape[0]
    b, t, d = x.shape
    h = x
    for layer in range(depth):
        j = layer // 2
        if layer % 2 == 0:
            mixed = _nsa_swa_layer(h, norm_mix[layer], ab_w_in[j], ab_gate_bias[j], nsa_pos_k[j], nsa_pos_v[j],
                               nsa_cmp_k_w1[j], nsa_cmp_k_w2[j], nsa_cmp_v_w1[j], nsa_cmp_v_w2[j],
                               swa_sinks[j], ab_w_out[j])
        else:
            mixed = _mlstm_layer(h, norm_mix[layer], c_w_in[j], c_conv_w[j], c_conv_b[j], c_igate_bias[j],
                             c_fgate_bias[j], c_w_out[j])
        last = layer == depth - 1
        h = _mix_ffn(h.reshape(b * t, d), *mixed, norm_ffn[layer], ffn_w_gate[layer].astype(BF16),
                     ffn_w_up[layer].astype(BF16), ffn_w_down[layer].astype(BF16),
                     final_norm if last else None).reshape(b, t, d)
    if depth == 0:
        raise ValueError("depth must be positive")
    return h
```

```python
import functools

import numpy as np
import jax
import jax.numpy as jnp
from jax import lax
from jax.experimental import pallas as pl
from jax.experimental.pallas import tpu as pltpu

HEAD_DIM = 64
NSA_HEADS = 8
NSA_KV_HEADS = 2
NSA_REP = NSA_HEADS // NSA_KV_HEADS
CMP_LEN = 32
CMP_STRIDE = 16
SEL_LEN = 64
SEL_TOPK = 16
NSA_WINDOW = 512
SWA_HEADS = 8
SWA_KV_HEADS = 2
SWA_REP = SWA_HEADS // SWA_KV_HEADS
SWA_WINDOW = 128
Q_BLOCK = 128
MLSTM_HEADS = 4
MLSTM_QK_DIM = 128
MLSTM_V_DIM = 256
MLSTM_CHUNK = 64
CONV_WIDTH = 4
RMS_EPS = 1e-6
NEG_INF = -1e30
FORCE_SCORE = 1e9
BELOW_NEG_INF = -3e38
LOG2_E = 1.4426950408889634
SEL_SHIFT = SEL_LEN.bit_length() - 1
assert 1 << SEL_SHIFT == SEL_LEN

LANES = 128
SUBLANES = 8
VMEM_LIMIT_BYTES = 56 * 1024 * 1024

ROW_TILE = 512
SCAN_CHUNK = 256
SCAN_ROWS = 512

F32 = jnp.float32
BF16 = jnp.bfloat16


def _cparams(n_axes):
    return pltpu.CompilerParams(
        dimension_semantics=("arbitrary",) * n_axes,
        vmem_limit_bytes=VMEM_LIMIT_BYTES,
    )


def _resident(shape):
    nd = len(shape)
    return pl.BlockSpec(shape, lambda *_: (0,) * nd)


def _dot(a, b):
    return jnp.dot(a, b, preferred_element_type=F32)


def _sigmoid(x):
    return 1.0 / (1.0 + jnp.exp(-x))


def _silu(x):
    return x * _sigmoid(x)


def _rmsnorm(x, g):
    y = x * lax.rsqrt(jnp.mean(x * x, axis=-1, keepdims=True) + RMS_EPS)
    return y * g


def _norm_proj_kernel(n_out, x_ref, g_ref, *refs):
    w_refs, o_refs = refs[:n_out], refs[n_out:]
    xn = _rmsnorm(x_ref[...], g_ref[...]).astype(BF16)
    for w_ref, o_ref in zip(w_refs, o_refs):
        o_ref[...] = _dot(xn, w_ref[...]).astype(o_ref.dtype)


def _norm_proj(x, g, ws, out_dtypes):
    m, d = x.shape
    n_out = len(ws)
    return pl.pallas_call(
        functools.partial(_norm_proj_kernel, n_out),
        grid=(m // ROW_TILE,),
        in_specs=[pl.BlockSpec((ROW_TILE, d), lambda i: (i, 0)), _resident((1, d))]
        + [_resident(w.shape) for w in ws],
        out_specs=[pl.BlockSpec((ROW_TILE, w.shape[1]), lambda i: (i, 0)) for w in ws],
        out_shape=[jax.ShapeDtypeStruct((m, w.shape[1]), dt) for w, dt in zip(ws, out_dtypes)],
        compiler_params=_cparams(1),
        name="norm_proj",
    )(x, g.reshape(1, d), *ws)


def _proj0_kernel(x_ref, g_ref, wq_ref, wk_ref, wv_ref, wc_ref,
                  q_ref, k_ref, c_ref, vbig_ref, vsmall_ref, gt_ref):
    xn = _rmsnorm(x_ref[...], g_ref[...]).astype(BF16)
    q_ref[...] = _dot(xn, wq_ref[...]).astype(q_ref.dtype)
    k_ref[...] = _dot(xn, wk_ref[...]).astype(k_ref.dtype)
    cg = _dot(xn, wc_ref[...])
    c_ref[...] = cg[:, :c_ref.shape[1]]
    gt_ref[0] = cg[:, c_ref.shape[1]:].T[:gt_ref.shape[1]]
    v = _dot(xn, wv_ref[...])
    width = v.shape[1] // 3
    for a in range(3):
        v_t = v[:, a * width:(a + 1) * width].T.astype(BF16)
        if a == 0:
            vbig_ref[0] = v_t
        for j in range(x_ref.shape[0] // Q_BLOCK):
            vsmall_ref[a, j] = v_t[:, j * Q_BLOCK:(j + 1) * Q_BLOCK]


def _proj0(x, g, w_q, w_k, w_v, w_c, n_cmp_cols, gate_rows):
    m, d = x.shape
    steps = m // ROW_TILE
    tiles = ROW_TILE // Q_BLOCK
    gd = w_v.shape[1] // 3
    row = lambda w: pl.BlockSpec((ROW_TILE, w), lambda i: (i, 0))
    return pl.pallas_call(
        _proj0_kernel,
        grid=(steps,),
        in_specs=[row(d), _resident((1, d))] + [_resident(w.shape) for w in (w_q, w_k, w_v, w_c)],
        out_specs=[
            row(w_q.shape[1]), row(w_k.shape[1]), row(n_cmp_cols),
            pl.BlockSpec((1, gd, ROW_TILE), lambda i: (i, 0, 0)),
            pl.BlockSpec((3, tiles, gd, Q_BLOCK), lambda i: (0, i, 0, 0)),
            pl.BlockSpec((1, gate_rows, ROW_TILE), lambda i: (i, 0, 0)),
        ],
        out_shape=[
            jax.ShapeDtypeStruct((m, w_q.shape[1]), BF16),
            jax.ShapeDtypeStruct((m, w_k.shape[1]), BF16),
            jax.ShapeDtypeStruct((m, n_cmp_cols), F32),
            jax.ShapeDtypeStruct((steps, gd, ROW_TILE), BF16),
            jax.ShapeDtypeStruct((3, steps * tiles, gd, Q_BLOCK), BF16),
            jax.ShapeDtypeStruct((steps, gate_rows, ROW_TILE), F32),
        ],
        compiler_params=_cparams(1),
        name="norm_proj_attn",
    )(x, g.reshape(1, d), w_q, w_k, w_v, w_c)


def _mix_ffn_kernel(n_in, gated, final, res_ref, *refs):
    a_refs, w_refs = refs[:n_in], refs[n_in:2 * n_in]
    rest = list(refs[2 * n_in:])
    o_ref = rest.pop()
    gate_ref = rest.pop(0) if gated else None
    g_ref, wg_ref, wu_ref, wd_ref = rest[:4]
    h = res_ref[...]
    for idx, (a_ref, w_ref) in enumerate(zip(a_refs, w_refs)):
        a = a_ref[...]
        if gated and idx == 0:
            a = (a.astype(F32) * _sigmoid(gate_ref[...].astype(F32))).astype(BF16)
        h = h + _dot(a, w_ref[...])
    xn = _rmsnorm(h, g_ref[...]).astype(BF16)
    act = (_silu(_dot(xn, wg_ref[...])) * _dot(xn, wu_ref[...])).astype(BF16)
    out = h + _dot(act, wd_ref[...])
    if final:
        out = _rmsnorm(out, rest[4][...])
    o_ref[...] = out


def _mix_ffn(res, a_list, w_list, gate, g, wg, wu, wd, final_g=None):
    m, d = res.shape
    row = lambda w: pl.BlockSpec((ROW_TILE, w), lambda i: (i, 0))
    once = lambda a: pl.BlockSpec(a.shape, lambda *_: (0,) * a.ndim, pipeline_mode=pl.Buffered(1))
    in_specs = [row(d)] + [row(a.shape[1]) for a in a_list] + [once(w) for w in w_list]
    args = [res, *a_list, *w_list]
    if gate is not None:
        in_specs.append(row(gate.shape[1]))
        args.append(gate)
    consts = [g.reshape(1, d), wg, wu, wd] + ([final_g.reshape(1, d)] if final_g is not None else [])
    return pl.pallas_call(
        functools.partial(_mix_ffn_kernel, len(a_list), gate is not None, final_g is not None),
        grid=(m // ROW_TILE,),
        in_specs=in_specs + [once(c) for c in consts],
        out_specs=row(d),
        out_shape=jax.ShapeDtypeStruct((m, d), F32),
        compiler_params=_cparams(1),
        name="mix_out_ffn",
    )(*args, *consts)


def _compress_kernel(x_ref, p_ref, w1_ref, w2_ref, rows_ref, cols_ref):
    n = x_ref.shape[1] // CMP_STRIDE
    a = jnp.zeros((n, x_ref.shape[2]), F32)
    b = jnp.zeros((n, x_ref.shape[2]), F32)
    for r in range(CMP_STRIDE):
        x = x_ref[0, pl.ds(r, n, stride=CMP_STRIDE), :]
        a = a + _dot((x + p_ref[0, r]).astype(BF16), w1_ref[0, r])
        b = b + _dot((x + p_ref[0, CMP_STRIDE + r]).astype(BF16), w1_ref[0, CMP_STRIDE + r])
    h = a + pltpu.roll(b, n - 1, 0)
    out = _dot(_silu(h).astype(BF16), w2_ref[0])
    live = lax.broadcasted_iota(jnp.int32, out.shape, 0) < n - 1
    out = jnp.where(live, out, 0.0)
    rows_ref[0, 0] = out.astype(rows_ref.dtype)
    cols_ref[0, 0] = out.T.astype(cols_ref.dtype)


def _compress(x, pos, w1, w2):
    b, t, two_gd = x.shape
    gd = two_gd // 2
    n = t // CMP_STRIDE
    return pl.pallas_call(
        _compress_kernel,
        grid=(2, b),
        in_specs=[
            pl.BlockSpec((1, t, gd), lambda s, i: (i, 0, s)),
            pl.BlockSpec((1,) + pos.shape[1:], lambda s, i: (s, 0, 0, 0)),
            pl.BlockSpec((1,) + w1.shape[1:], lambda s, i: (s, 0, 0, 0)),
            pl.BlockSpec((1,) + w2.shape[1:], lambda s, i: (s, 0, 0)),
        ],
        out_specs=[pl.BlockSpec((1, 1, n, gd), lambda s, i: (s, i, 0, 0)),
                   pl.BlockSpec((1, 1, gd, n), lambda s, i: (s, i, 0, 0))],
        out_shape=[jax.ShapeDtypeStruct((2, b, n, gd), BF16), jax.ShapeDtypeStruct((2, b, gd, n), BF16)],
        compiler_params=_cparams(2),
        name="compress_kv",
    )(x, pos, w1, w2)


def _stack_heads_t(q_ref, rep):
    qt = q_ref[0].astype(F32).T
    return jnp.concatenate([qt[r * HEAD_DIM:(r + 1) * HEAD_DIM] for r in range(rep)], axis=1).astype(BF16)


def _unstack_heads_t(o_t, rep):
    tq = o_t.shape[1] // rep
    return jnp.concatenate([o_t[:, r * tq:(r + 1) * tq] for r in range(rep)], axis=0).T


def _tile_lanes(x, rep):
    return jnp.concatenate([x] * rep, axis=1)


def _col_tree(x, pair, reduce):
    while x.shape[0] > SUBLANES:
        half = x.shape[0] // 2
        x = pair(x[:half], x[half:])
    return reduce(x, axis=0, keepdims=True)


def _attn_kernel(n_sel_blocks, sink_ref, qa_ref, qb_ref, gl_ref, gb_ref, kc_ref, vct_ref, selmap_ref,
                 ksd_ref, ks_ref, mask_ref, vsd_ref, vbig_ref, kw_ref, vw_ref, kbp_ref, kbc_ref, vbp_ref, vbc_ref,
                 eye_ref, cmask_ref, dmask_ref, wmask_ref, bmask_ref, oa_ref, ob_ref, m_sc, acc_sc, s_sc):
    n = pl.program_id(1)
    rep = NSA_REP
    n_groups = ksd_ref.shape[2] // HEAD_DIM
    gcols = rep * Q_BLOCK
    groups = range(n_groups)
    tq_row = n * Q_BLOCK + lax.broadcasted_iota(jnp.int32, (1, Q_BLOCK), 1)

    def pad_q(q_t):
        zero = jnp.zeros((HEAD_DIM, gcols), q_t.dtype)
        return jnp.concatenate([jnp.concatenate([q_t[:, i * gcols:(i + 1) * gcols] if i == j else zero
                                                 for i in groups], axis=1) for j in groups], axis=0)

    def per_group(fn):
        return jnp.concatenate([fn(i) for i in groups], axis=1)

    def apply_values(v_t, e):
        return per_group(lambda i: _dot(with_ones(v_t[i * HEAD_DIM:(i + 1) * HEAD_DIM]),
                                        e[:, i * gcols:(i + 1) * gcols]))

    def masked_logits(keys, mask, q_rows):
        return _dot(jnp.concatenate([keys, mask], axis=1), jnp.concatenate([q_rows, eye_ref[...]], axis=0))

    def with_ones(v_t):
        return jnp.concatenate([v_t, jnp.ones((SUBLANES, v_t.shape[1]), v_t.dtype)], axis=0)

    q_t = pad_q(_stack_heads_t(qa_ref, n_groups * rep))
    kc = kc_ref[0, 0]
    n_cmp = kc.shape[0]
    odd = lax.rem(n, 2)
    shift = Q_BLOCK // CMP_STRIDE
    start = pl.multiple_of(n_cmp - shift * (n + odd), 2 * SUBLANES)
    s_c = masked_logits(kc, cmask_ref[odd, pl.ds(start, n_cmp), :], q_t)
    q_b = pad_q(_stack_heads_t(qb_ref, n_groups * SWA_REP))
    k_b = jnp.concatenate([kbp_ref[0], kbc_ref[0]], axis=0)
    s_b = masked_logits(k_b, bmask_ref[jnp.minimum(n, bmask_ref.shape[0] - 1)], q_b)
    n_win_tiles = NSA_WINDOW // Q_BLOCK + 1
    tiles = [jnp.maximum(n - (n_win_tiles - 1) + i, 0) for i in range(n_win_tiles)]
    k_w = jnp.concatenate([kw_ref[0, pl.ds(pl.multiple_of(j * Q_BLOCK, Q_BLOCK), Q_BLOCK), :] for j in tiles], axis=0)
    s_w = masked_logits(k_w, wmask_ref[jnp.minimum(n, wmask_ref.shape[0] - 1)], q_t)
    s_d = masked_logits(ksd_ref[0], dmask_ref[...], q_t)

    m = jnp.max(s_c, axis=0, keepdims=True)
    e_c = jnp.exp2(s_c - m).astype(BF16)
    vct = vct_ref[0, 0]
    oc_imp = per_group(lambda i: _dot(
        jnp.concatenate([with_ones(vct[i * HEAD_DIM:(i + 1) * HEAD_DIM]), selmap_ref[...]], axis=0),
        e_c[:, i * gcols:(i + 1) * gcols]))
    den = oc_imp[HEAD_DIM:HEAD_DIM + 1]
    seen = m > 0.5 * NEG_INF
    oc_imp = jnp.where(seen, oc_imp * (1.0 / jnp.where(seen, den, 1.0)), 0.0)
    o_c = oc_imp[:HEAD_DIM]
    imp = oc_imp[HEAD_DIM + SUBLANES:]
    imp = per_group(lambda i: sum(imp[:, (i * rep + r) * Q_BLOCK:(i * rep + r + 1) * Q_BLOCK]
                                  for r in range(rep)))

    sink_row = jnp.concatenate([jnp.full((1, Q_BLOCK), sink_ref[h] * LOG2_E, F32)
                                for h in range(n_groups * SWA_REP)], axis=1)
    v_tiles_b = [vbp_ref[0, 0, 0], vbc_ref[0, 0, 0]]
    v_tiles_w = [vw_ref[0, 0, j] for j in tiles]
    v_tiles_d = [vsd_ref[0, 0, 0]]
    maxes, partial = {}, {"b": [], "w": [], "d": []}

    def col_max(name, s, extra=None):
        def run():
            m = jnp.max(s, axis=0, keepdims=True)
            maxes[name] = m if extra is None else jnp.maximum(m, extra)
        return run

    def pv_tile(name, s, v_tiles, i):
        def run():
            rows = slice(i * Q_BLOCK, (i + 1) * Q_BLOCK)
            partial[name].append(apply_values(v_tiles[i], jnp.exp2(s[rows] - maxes[name]).astype(BF16)))
        return run

    pieces = ([col_max("w", s_w), col_max("b", s_b, sink_row), col_max("d", s_d)]
              + [pv_tile("w", s_w, v_tiles_w, i) for i in range(len(v_tiles_w))]
              + [pv_tile("b", s_b, v_tiles_b, i) for i in range(len(v_tiles_b))]
              + [pv_tile("d", s_d, v_tiles_d, 0)])

    ns_pad = imp.shape[0]
    blk = lax.broadcasted_iota(jnp.int32, (ns_pad, n_groups * Q_BLOCK), 0)
    blk_f = blk.astype(F32)
    tq_row = _tile_lanes(tq_row, n_groups)
    cur = jnp.right_shift(tq_row, SEL_SHIFT)
    forced = (blk == 0) | (blk == cur) | (blk == cur - 1)
    valid = jnp.left_shift(blk, SEL_SHIFT) <= tq_row
    score = jnp.where(valid, imp, NEG_INF)
    score = jnp.where(forced | (blk >= n_sel_blocks), BELOW_NEG_INF, score)
    sel = forced
    for _ in range(min(SEL_TOPK, n_sel_blocks) - 3):
        if pieces:
            pieces.pop(0)()
        top = _col_tree(score, jnp.maximum, jnp.max)
        first_hit = _col_tree(jnp.where(score == top, blk_f, float(ns_pad)), jnp.minimum, jnp.min)
        pick = blk_f == first_hit
        sel = sel | pick
        score = jnp.where(pick, BELOW_NEG_INF, score)
    n_diag_blk = n * (Q_BLOCK // SEL_LEN)
    unselected = jnp.where(sel & (blk < n_diag_blk), 0.0, 1.0).astype(BF16)

    while pieces:
        pieces.pop(0)()

    def finish(name, sink=None):
        acc = sum(partial[name][1:], partial[name][0])
        den = acc[HEAD_DIM:HEAD_DIM + 1]
        if sink is not None:
            den = den + jnp.exp2(sink - maxes[name])
        return acc[:HEAD_DIM] * (1.0 / den)

    ob_ref[0] = _unstack_heads_t(finish("b", sink_row), n_groups * SWA_REP).astype(ob_ref.dtype)
    o_w = finish("w")
    m_sc[...] = maxes["d"]
    acc_sc[...] = partial["d"][0]

    q_aug = jnp.concatenate([per_group(lambda i: _tile_lanes(unselected[:, i * Q_BLOCK:(i + 1) * Q_BLOCK], rep)),
                             q_t], axis=0)
    tk = s_sc.shape[1]
    n_tiles = ks_ref.shape[1] // tk

    def logits(kt):
        keys = pl.ds(pl.multiple_of(kt * tk, tk), tk)
        return _dot(jnp.concatenate([mask_ref[keys, :], ks_ref[0, keys, :]], axis=1), q_aug)

    def absorb(kt, s):
        m_old = m_sc[...]
        m_new = jnp.maximum(m_old, jnp.max(s, axis=0, keepdims=True))
        pv = apply_values(vbig_ref[0, kt], jnp.exp2(s - m_new).astype(BF16))
        acc_sc[...] = jnp.exp2(m_old - m_new) * acc_sc[...] + pv
        m_sc[...] = m_new

    s_sc[0] = logits(0)

    def tile_pair(j, carry):
        s_sc[1] = logits(2 * j + 1)
        absorb(2 * j, s_sc[0])
        s_sc[0] = logits(jnp.minimum(2 * j + 2, n_tiles - 1))
        absorb(2 * j + 1, s_sc[1])
        return carry

    lax.fori_loop(0, lax.div(n * Q_BLOCK + 2 * tk - 1, 2 * tk), tile_pair, 0)
    acc = acc_sc[...]
    o_s = acc[:HEAD_DIM] * (1.0 / acc[HEAD_DIM:HEAD_DIM + 1])

    gates = _sigmoid(gl_ref[0, 0] + gb_ref[...])
    gate = lambda c: jnp.concatenate([gates[16 * i + 3 * r + c:16 * i + 3 * r + c + 1]
                                      for i in groups for r in range(rep)], axis=1)
    o = gate(0) * o_c + gate(1) * o_s + gate(2) * o_w
    oa_ref[0] = _unstack_heads_t(o, n_groups * rep).astype(oa_ref.dtype)


def _selection_map(t, nc_pad, ns_pad):
    nc = (t - CMP_LEN) // CMP_STRIDE + 1
    ns = t // SEL_LEN
    cs = np.arange(nc, dtype=np.int64)[:, None] * CMP_STRIDE
    ss = np.arange(ns, dtype=np.int64)[None, :] * SEL_LEN
    out = np.zeros((nc_pad, ns_pad), np.float32)
    out[:nc, :ns] = (cs < ss + SEL_LEN) & (cs + CMP_LEN > ss)
    return out


def _block_mask_cols(t, ns_pad):
    hit = np.arange(ns_pad, dtype=np.int64)[None, :] == (np.arange(t, dtype=np.int64) // SEL_LEN)[:, None]
    return np.where(hit, NEG_INF, 0.0).astype(np.float32)


def _band_masks(n_rows, window):
    n_front = n_rows // Q_BLOCK - 1
    r = np.arange(n_rows)[None, :, None]
    tok = np.arange(Q_BLOCK)[None, None, :]
    v = np.arange(n_front + 1)[:, None, None]
    dist = n_front * Q_BLOCK + tok - r
    ok = (dist >= 0) & (dist < window) & (r >= (n_front - v) * Q_BLOCK)
    return np.where(ok, 0.0, NEG_INF).astype(np.float32)


def _compressed_masks(n_cmp):
    shift = Q_BLOCK // CMP_STRIDE
    u = np.arange(2 * n_cmp)[None, :, None] + shift * np.arange(2)[:, None, None]
    tok = np.arange(Q_BLOCK)[None, None, :]
    ok = CMP_STRIDE * (u - n_cmp) + CMP_LEN - 1 <= tok
    return np.where(ok, 0.0, NEG_INF).astype(np.float32)


def _attention(q, k_all, v_big, v_small, gate_t, gate_bias, cmp_rows, cmp_cols, sel_map_t, mask_cols, sinks,
               n_sel_blocks):
    b, t = q.shape[:2]
    gd = k_all.shape[2] // 3
    g = gd // HEAD_DIM
    tk = v_big.shape[3]
    assert (t // tk) % 2 == 0, "selected-branch key tiles are consumed in pairs"
    width = g * NSA_REP * HEAD_DIM
    cols = g * NSA_REP * Q_BLOCK
    gate_cols = gate_t.shape[3] // Q_BLOCK
    prev = lambda n: jnp.maximum(n - 1, 0)
    cur = lambda n: n
    k_tile = lambda blk, at: pl.BlockSpec((1, Q_BLOCK, gd), lambda i, n: (i, at(n), blk))
    k_all_rows = lambda blk: pl.BlockSpec((1, t, gd), lambda i, n: (i, 0, blk))
    v_tile = lambda a, at: pl.BlockSpec((1, 1, 1, gd, Q_BLOCK), lambda i, n: (a, i, at(n), 0, 0))
    out_spec = pl.BlockSpec((1, Q_BLOCK, width), lambda i, n: (i, n, 0))
    out_shape = jax.ShapeDtypeStruct((b, t, width), BF16)
    tri = np.where(np.arange(Q_BLOCK)[:, None] <= np.arange(Q_BLOCK)[None, :], 0.0, NEG_INF)
    consts = [jnp.asarray(c, BF16) for c in (
        np.tile(np.eye(Q_BLOCK, dtype=np.float32), (1, g * NSA_REP)),
        _compressed_masks(cmp_rows.shape[2]), tri,
        _band_masks(NSA_WINDOW + Q_BLOCK, NSA_WINDOW), _band_masks(SWA_WINDOW + Q_BLOCK, SWA_WINDOW))]
    return pl.pallas_call(
        functools.partial(_attn_kernel, n_sel_blocks),
        grid=(b, t // Q_BLOCK),
        in_specs=[
            pl.BlockSpec(memory_space=pltpu.SMEM),
            pl.BlockSpec((1, Q_BLOCK, width), lambda i, n: (i, n, 0)),
            pl.BlockSpec((1, Q_BLOCK, width), lambda i, n: (i, n, 1)),
            pl.BlockSpec((1, 1, 16 * g, Q_BLOCK), lambda i, n: (i, n // gate_cols, 0, n % gate_cols)),
            _resident(gate_bias.shape),
            pl.BlockSpec((1, 1) + cmp_rows.shape[2:], lambda i, n: (0, i, 0, 0)),
            pl.BlockSpec((1, 1) + cmp_cols.shape[2:], lambda i, n: (1, i, 0, 0)),
            _resident(sel_map_t.shape),
            k_tile(0, cur), k_all_rows(0), _resident(mask_cols.shape),
            v_tile(0, cur), pl.BlockSpec((1,) + v_big.shape[1:], lambda i, n: (i, 0, 0, 0)),
            k_all_rows(1), pl.BlockSpec((1, 1) + v_small.shape[2:], lambda i, n: (1, i, 0, 0, 0)),
            k_tile(2, prev), k_tile(2, cur), v_tile(2, prev), v_tile(2, cur),
        ] + [_resident(c.shape) for c in consts],
        out_specs=[out_spec, out_spec],
        out_shape=[out_shape, out_shape],
        scratch_shapes=[
            pltpu.VMEM((1, cols), F32),
            pltpu.VMEM((HEAD_DIM + SUBLANES, cols), F32),
            pltpu.VMEM((2, tk, cols), F32),
        ],
        compiler_params=_cparams(2),
        name="nsa_swa_attention",
    )(sinks, q, q, gate_t, gate_bias, cmp_rows, cmp_cols, sel_map_t, k_all, k_all, mask_cols, v_small, v_big,
      k_all, v_small, k_all, k_all, v_small, v_small, *consts)


def _conv_silu_kernel(n_q_cols, x_ref, prev_ref, w_ref, b_ref, o_ref, buf):
    i = pl.program_id(1)
    rows = x_ref.shape[1]
    halo = prev_ref.shape[1]
    buf[:halo, :] = jnp.where(i > 0, prev_ref[0].astype(F32), 0.0)
    buf[halo:, :] = x_ref[0].astype(F32)
    y = b_ref[...]
    for j in range(CONV_WIDTH):
        start = halo - (CONV_WIDTH - 1) + j
        y = y + w_ref[j:j + 1, :] * buf[start:start + rows, :]
    y = _silu(y)
    col = lax.broadcasted_iota(jnp.int32, y.shape, 1)
    o_ref[0] = jnp.where(col >= n_q_cols, y * (MLSTM_QK_DIM ** -0.5), y).astype(o_ref.dtype)


def _conv_silu(x, w, bias):
    b, t, c = x.shape
    halo = 2 * SUBLANES
    ratio = ROW_TILE // halo
    return pl.pallas_call(
        functools.partial(_conv_silu_kernel, c // 2),
        grid=(b, t // ROW_TILE),
        in_specs=[
            pl.BlockSpec((1, ROW_TILE, c), lambda i, j: (i, j, 0)),
            pl.BlockSpec((1, halo, c), lambda i, j: (i, jnp.maximum(j * ratio - 1, 0), 0)),
            _resident(w.shape),
            _resident((1, c)),
        ],
        out_specs=pl.BlockSpec((1, ROW_TILE, c), lambda i, j: (i, j, 0)),
        out_shape=jax.ShapeDtypeStruct((b, t, c), BF16),
        scratch_shapes=[pltpu.VMEM((ROW_TILE + halo, c), F32)],
        compiler_params=_cparams(2),
        name="conv_silu",
    )(x, x, w, bias.reshape(1, c))


def _mlstm_kernel(bias_ref, q_ref, k_ref, v_ref, ig_ref, fg_ref, o_ref, state, m_state):
    c = pl.program_id(1)
    L = SCAN_CHUNK
    dqk, dv = MLSTM_QK_DIM, MLSTM_V_DIM

    @pl.when(c == 0)
    def _():
        state[...] = jnp.zeros(state.shape, F32)
        m_state[...] = jnp.zeros(m_state.shape, F32)

    t_idx = lax.broadcasted_iota(jnp.int32, (L, L), 0)
    s_idx = lax.broadcasted_iota(jnp.int32, (L, L), 1)
    causal = s_idx <= t_idx
    eye = s_idx == t_idx
    ones_col = (lax.broadcasted_iota(jnp.int32, (L, LANES), 1) == 0).astype(BF16)

    def to_col(row):
        return jnp.sum(jnp.where(eye, row, 0.0), axis=1, keepdims=True)

    for j in range(SCAN_ROWS // SCAN_CHUNK):
        rows = slice(j * L, (j + 1) * L)
        for h in range(MLSTM_HEADS):
            qc = q_ref[0, rows, h * dqk:(h + 1) * dqk]
            kc = k_ref[0, rows, h * dqk:(h + 1) * dqk]
            va = jnp.concatenate([v_ref[0, rows, h * dv:(h + 1) * dv], ones_col], axis=1)
            i_row = ig_ref[0, h:h + 1, rows] + bias_ref[0, h]
            f_pre = fg_ref[0, h:h + 1, rows] + bias_ref[1, h]
            f_row = jnp.minimum(f_pre, 0.0) - jnp.log(1.0 + jnp.exp(-jnp.abs(f_pre)))
            f_col = to_col(f_row)
            b_col = jnp.sum(jnp.where(causal, f_row, 0.0), axis=1, keepdims=True)
            b_row = jnp.sum(jnp.where(t_idx <= s_idx, f_col, 0.0), axis=0, keepdims=True)
            m_old = m_state[h:h + 1, 0:1]
            a = b_col + m_old
            d = jnp.where(causal, b_col - b_row + i_row, -jnp.inf)
            mt = jnp.maximum(a, jnp.max(d, axis=1, keepdims=True))
            w_inter = jnp.exp(a - mt)
            sm = lax.dot_general(qc, kc, (((1,), (1,)), ((), ())), preferred_element_type=F32) * jnp.exp(d - mt)
            st = state[h]
            nd = w_inter * _dot(qc, st.astype(BF16)) + _dot(sm.astype(BF16), va)
            hc = nd[:, :dv] * (1.0 / jnp.maximum(jnp.abs(nd[:, dv:dv + 1]), jnp.exp(-mt)))
            o_ref[0, rows, h * dv:(h + 1) * dv] = hc.astype(o_ref.dtype)
            b_last = b_row[:, L - 1:L]
            wl = b_last - b_row + i_row
            m_new = jnp.maximum(b_last + m_old, jnp.max(wl, axis=1, keepdims=True))
            decay = jnp.exp(b_last + m_old - m_new)
            w_col = to_col(jnp.exp(wl - m_new))
            kw = (kc.astype(F32) * w_col).astype(BF16)
            upd = lax.dot_general(kw, va, (((0,), (0,)), ((), ())), preferred_element_type=F32)
            state[h] = decay * st + upd
            m_state[h:h + 1, 0:1] = m_new


def _mlstm(qk, v, ig, fg, gate_bias):
    b, t = qk.shape[:2]
    nh = MLSTM_HEADS
    rows = SCAN_ROWS
    gate_spec = pl.BlockSpec((1, nh, rows), lambda i, c: (i, 0, c))
    return pl.pallas_call(
        _mlstm_kernel,
        grid=(b, t // rows),
        in_specs=[
            pl.BlockSpec(memory_space=pltpu.SMEM),
            pl.BlockSpec((1, rows, nh * MLSTM_QK_DIM), lambda i, c: (i, c, 0)),
            pl.BlockSpec((1, rows, nh * MLSTM_QK_DIM), lambda i, c: (i, c, 1)),
            pl.BlockSpec((1, rows, nh * MLSTM_V_DIM), lambda i, c: (i, c, 0)),
            gate_spec, gate_spec,
        ],
        out_specs=pl.BlockSpec((1, rows, nh * MLSTM_V_DIM), lambda i, c: (i, c, 0)),
        out_shape=jax.ShapeDtypeStruct((b, t, nh * MLSTM_V_DIM), BF16),
        scratch_shapes=[
            pltpu.VMEM((nh, MLSTM_QK_DIM, MLSTM_V_DIM + LANES), F32),
            pltpu.VMEM((SUBLANES, LANES), F32),
        ],
        compiler_params=_cparams(2),
        name="mlstm_scan",
    )(gate_bias, qk, qk, v, ig, fg)


def _block_diag(w, n):
    r, c = w.shape[-2:]
    out = jnp.zeros(w.shape[:-2] + (n * r, n * c), w.dtype)
    for i in range(n):
        out = out.at[..., i * r:(i + 1) * r, i * c:(i + 1) * c].set(w)
    return out


def _nsa_swa_layer(h, norm_g, w_in, gate_bias, pos_k, pos_v, ck_w1, ck_w2, cv_w1, cv_w2, sinks, w_out):
    b, t, d = h.shape
    m = b * t
    ng = NSA_KV_HEADS
    gk = ng * HEAD_DIM
    nq_a, nq_b = NSA_HEADS * HEAD_DIM, SWA_HEADS * HEAD_DIM
    ngate = NSA_HEADS * 3
    o_kc, o_vc, o_ks, o_vs, o_kw, o_vw = (nq_a + i * gk for i in range(6))
    o_g = nq_a + 6 * gk
    o_qb = o_g + ngate
    o_kb, o_vb = o_qb + nq_b, o_qb + nq_b + gk
    cols = lambda o, n: w_in[:, o:o + n]
    cat = lambda parts: jnp.concatenate(parts, axis=1).astype(BF16)
    q_scale = HEAD_DIM ** -0.5 * LOG2_E
    gate_pad = jnp.zeros((d, 16 - NSA_REP * 3), w_in.dtype)
    w_gate = jnp.concatenate([p for i in range(ng) for p in (cols(o_g + i * NSA_REP * 3, NSA_REP * 3), gate_pad)]
                             + [jnp.zeros((d, LANES - 16 * ng), w_in.dtype)], axis=1)
    q_all, k_all, cmp_in, v_big, v_small, gate_t = _proj0(
        h.reshape(m, d), norm_g, (jnp.concatenate([cols(0, nq_a), cols(o_qb, nq_b)], axis=1) * q_scale).astype(BF16),
        cat([cols(o_ks, gk), cols(o_kw, gk), cols(o_kb, gk)]), cat([cols(o_vs, gk), cols(o_vw, gk), cols(o_vb, gk)]),
        cat([cols(o_kc, gk), cols(o_vc, gk), w_gate]), 2 * gk, 16 * ng)
    tk = ROW_TILE
    q_all = q_all.reshape(b, t, nq_a + nq_b)
    k_all = k_all.reshape(b, t, 3 * gk)
    v_big = v_big.reshape(b, t // tk, gk, tk)
    v_small = v_small.reshape(3, b, t // Q_BLOCK, gk, Q_BLOCK)
    gate_t = gate_t.reshape(b, t // ROW_TILE, 16 * ng, ROW_TILE)
    gb = jnp.pad(gate_bias.reshape(ng, NSA_REP * 3), ((0, 0), (0, 16 - NSA_REP * 3))).reshape(ng * 16, 1)

    tile2 = lambda p: jnp.tile(p[:, None, :], (1, 1, ng))
    pos = jnp.stack([tile2(pos_k), tile2(pos_v)])
    per_pos = lambda w: w.reshape(CMP_LEN, HEAD_DIM, HEAD_DIM)
    w1 = _block_diag(jnp.stack([per_pos(ck_w1), per_pos(cv_w1)]), ng).astype(BF16)
    w2 = _block_diag(jnp.stack([ck_w2, cv_w2]), ng).astype(BF16)
    cmp_rows, cmp_cols = _compress(cmp_in.reshape(b, t, 2 * gk), pos, w1, w2)

    n_sel = t // SEL_LEN
    ns_pad = -(-n_sel // LANES) * LANES
    sel_map_t = jnp.asarray(_selection_map(t, t // CMP_STRIDE, ns_pad).T, BF16)
    mask_cols = jnp.asarray(_block_mask_cols(t, ns_pad), BF16)
    o_a, o_b = _attention(q_all, k_all, v_big, v_small, gate_t, gb, cmp_rows, cmp_cols, sel_map_t, mask_cols,
                          sinks, n_sel)

    w_o = w_out.astype(BF16)
    return [o_a.reshape(m, nq_a), o_b.reshape(m, nq_b)], [w_o[:nq_a], w_o[nq_a:]], None


def _mlstm_layer(h, norm_g, w_in, conv_w, conv_b, igate_bias, fgate_bias, w_out):
    b, t, d = h.shape
    m = b * t
    nqk = 2 * MLSTM_HEADS * MLSTM_QK_DIM
    nv = MLSTM_HEADS * MLSTM_V_DIM
    w = w_in.astype(BF16)
    w_gates = jnp.concatenate([w[:, nqk + 2 * nv:], jnp.zeros((d, LANES - 2 * MLSTM_HEADS), BF16)], axis=1)
    qk_raw, v, og, gates = _norm_proj(
        h.reshape(m, d), norm_g,
        [w[:, :nqk], w[:, nqk:nqk + nv], w[:, nqk + nv:nqk + 2 * nv], w_gates],
        [BF16, BF16, BF16, F32])
    qk = _conv_silu(qk_raw.reshape(b, t, nqk), conv_w, conv_b)
    gates = gates.reshape(b, t, -1)
    ig = gates[..., :MLSTM_HEADS].transpose(0, 2, 1)
    fg = gates[..., MLSTM_HEADS:2 * MLSTM_HEADS].transpose(0, 2, 1)
    hh = _mlstm(qk, v.reshape(b, t, nv), ig, fg, jnp.stack([igate_bias, fgate_bias]))
    return [hh.reshape(m, nv)], [w_out.astype(BF16)], og


def kernel(x, norm_mix, norm_ffn, ffn_w_gate, ffn_w_up, ffn_w_down, ab_w_in, ab_gate_bias, nsa_pos_k, nsa_pos_v, nsa_cmp_k_w1, nsa_cmp_k_w2, nsa_cmp_v_w1, nsa_cmp_v_w2, swa_sinks, ab_w_out, c_w_in, c_conv_w, c_conv_b, c_igate_bias, c_fgate_bias, c_w_out, final_norm):
    depth = norm_mix.shape[0]
    b, t, d = x.shape
    h = x
    for layer in range(depth):
        j = layer // 2
        if layer % 2 == 0:
            mixed = _nsa_swa_layer(h, norm_mix[layer], ab_w_in[j], ab_gate_bias[j], nsa_pos_k[j], nsa_pos_v[j],
                               nsa_cmp_k_w1[j], nsa_cmp_k_w2[j], nsa_cmp_v_w1[j], nsa_cmp_v_w2[j],
                               swa_sinks[j], ab_w_out[j])
        else:
            mixed = _mlstm_layer(h, norm_mix[layer], c_w_in[j], c_conv_w[j], c_conv_b[j], c_igate_bias[j],
                             c_fgate_bias[j], c_w_out[j])
        last = layer == depth - 1
        h = _mix_ffn(h.reshape(b * t, d), *mixed, norm_ffn[layer], ffn_w_gate[layer].astype(BF16),
                     ffn_w_up[layer].astype(BF16), ffn_w_down[layer].astype(BF16),
                     final_norm if last else None).reshape(b, t, d)
    if depth == 0:
        raise ValueError("depth must be positive")
    return h
```

```python
import functools

import numpy as np
import jax
import jax.numpy as jnp
from jax import lax
from jax.experimental import pallas as pl
from jax.experimental.pallas import tpu as pltpu

HEAD_DIM = 64
NSA_HEADS = 8
NSA_KV_HEADS = 2
NSA_REP = NSA_HEADS // NSA_KV_HEADS
CMP_LEN = 32
CMP_STRIDE = 16
SEL_LEN = 64
SEL_TOPK = 16
NSA_WINDOW = 512
SWA_HEADS = 8
SWA_KV_HEADS = 2
SWA_REP = SWA_HEADS // SWA_KV_HEADS
SWA_WINDOW = 128
Q_BLOCK = 128
MLSTM_HEADS = 4
MLSTM_QK_DIM = 128
MLSTM_V_DIM = 256
MLSTM_CHUNK = 64
CONV_WIDTH = 4
RMS_EPS = 1e-6
NEG_INF = -1e30
FORCE_SCORE = 1e9
BELOW_NEG_INF = -3e38
LOG2_E = 1.4426950408889634
SEL_SHIFT = SEL_LEN.bit_length() - 1
assert 1 << SEL_SHIFT == SEL_LEN

LANES = 128
SUBLANES = 8
VMEM_LIMIT_BYTES = 56 * 1024 * 1024

ROW_TILE = 512
SCAN_CHUNK = 256
SCAN_ROWS = 512

F32 = jnp.float32
BF16 = jnp.bfloat16


def _cparams(n_axes):
    return pltpu.CompilerParams(
        dimension_semantics=("arbitrary",) * n_axes,
        vmem_limit_bytes=VMEM_LIMIT_BYTES,
    )


def _resident(shape):
    nd = len(shape)
    return pl.BlockSpec(shape, lambda *_: (0,) * nd)


def _dot(a, b):
    return jnp.dot(a, b, preferred_element_type=F32)


def _sigmoid(x):
    return 1.0 / (1.0 + jnp.exp(-x))


def _silu(x):
    return x * _sigmoid(x)


def _rmsnorm(x, g):
    y = x * lax.rsqrt(jnp.mean(x * x, axis=-1, keepdims=True) + RMS_EPS)
    return y * g


def _norm_proj_kernel(n_out, x_ref, g_ref, *refs):
    w_refs, o_refs = refs[:n_out], refs[n_out:]
    xn = _rmsnorm(x_ref[...], g_ref[...]).astype(BF16)
    for w_ref, o_ref in zip(w_refs, o_refs):
        o_ref[...] = _dot(xn, w_ref[...]).astype(o_ref.dtype)


def _norm_proj(x, g, ws, out_dtypes):
    m, d = x.shape
    n_out = len(ws)
    return pl.pallas_call(
        functools.partial(_norm_proj_kernel, n_out),
        grid=(m // ROW_TILE,),
        in_specs=[pl.BlockSpec((ROW_TILE, d), lambda i: (i, 0)), _resident((1, d))]
        + [_resident(w.shape) for w in ws],
        out_specs=[pl.BlockSpec((ROW_TILE, w.shape[1]), lambda i: (i, 0)) for w in ws],
        out_shape=[jax.ShapeDtypeStruct((m, w.shape[1]), dt) for w, dt in zip(ws, out_dtypes)],
        compiler_params=_cparams(1),
        name="norm_proj",
    )(x, g.reshape(1, d), *ws)


def _proj0_kernel(x_ref, g_ref, wq_ref, wk_ref, wv_ref, wc_ref,
                  q_ref, k_ref, c_ref, vbig_ref, vsmall_ref, gt_ref):
    xn = _rmsnorm(x_ref[...], g_ref[...]).astype(BF16)
    q_ref[...] = _dot(xn, wq_ref[...]).astype(q_ref.dtype)
    k_ref[...] = _dot(xn, wk_ref[...]).astype(k_ref.dtype)
    cg = _dot(xn, wc_ref[...])
    c_ref[...] = cg[:, :c_ref.shape[1]]
    gt_ref[0] = cg[:, c_ref.shape[1]:].T[:gt_ref.shape[1]]
    v = _dot(xn, wv_ref[...])
    width = v.shape[1] // 3
    for a in range(3):
        v_t = v[:, a * width:(a + 1) * width].T.astype(BF16)
        if a == 0:
            vbig_ref[0] = v_t
        for j in range(x_ref.shape[0] // Q_BLOCK):
            vsmall_ref[a, j] = v_t[:, j * Q_BLOCK:(j + 1) * Q_BLOCK]


def _proj0(x, g, w_q, w_k, w_v, w_c, n_cmp_cols, gate_rows):
    m, d = x.shape
    steps = m // ROW_TILE
    tiles = ROW_TILE // Q_BLOCK
    gd = w_v.shape[1] // 3
    row = lambda w: pl.BlockSpec((ROW_TILE, w), lambda i: (i, 0))
    return pl.pallas_call(
        _proj0_kernel,
        grid=(steps,),
        in_specs=[row(d), _resident((1, d))] + [_resident(w.shape) for w in (w_q, w_k, w_v, w_c)],
        out_specs=[
            row(w_q.shape[1]), row(w_k.shape[1]), row(n_cmp_cols),
            pl.BlockSpec((1, gd, ROW_TILE), lambda i: (i, 0, 0)),
            pl.BlockSpec((3, tiles, gd, Q_BLOCK), lambda i: (0, i, 0, 0)),
            pl.BlockSpec((1, gate_rows, ROW_TILE), lambda i: (i, 0, 0)),
        ],
        out_shape=[
            jax.ShapeDtypeStruct((m, w_q.shape[1]), BF16),
            jax.ShapeDtypeStruct((m, w_k.shape[1]), BF16),
            jax.ShapeDtypeStruct((m, n_cmp_cols), F32),
            jax.ShapeDtypeStruct((steps, gd, ROW_TILE), BF16),
            jax.ShapeDtypeStruct((3, steps * tiles, gd, Q_BLOCK), BF16),
            jax.ShapeDtypeStruct((steps, gate_rows, ROW_TILE), F32),
        ],
        compiler_params=_cparams(1),
        name="norm_proj_attn",
    )(x, g.reshape(1, d), w_q, w_k, w_v, w_c)


def _mix_ffn_kernel(n_in, gated, final, res_ref, *refs):
    a_refs, w_refs = refs[:n_in], refs[n_in:2 * n_in]
    rest = list(refs[2 * n_in:])
    o_ref = rest.pop()
    gate_ref = rest.pop(0) if gated else None
    g_ref, wg_ref, wu_ref, wd_ref = rest[:4]
    h = res_ref[...]
    for idx, (a_ref, w_ref) in enumerate(zip(a_refs, w_refs)):
        a = a_ref[...]
        if gated and idx == 0:
            a = (a.astype(F32) * _sigmoid(gate_ref[...].astype(F32))).astype(BF16)
        h = h + _dot(a, w_ref[...])
    xn = _rmsnorm(h, g_ref[...]).astype(BF16)
    act = (_silu(_dot(xn, wg_ref[...])) * _dot(xn, wu_ref[...])).astype(BF16)
    out = h + _dot(act, wd_ref[...])
    if final:
        out = _rmsnorm(out, rest[4][...])
    o_ref[...] = out


def _mix_ffn(res, a_list, w_list, gate, g, wg, wu, wd, final_g=None):
    m, d = res.shape
    row = lambda w: pl.BlockSpec((ROW_TILE, w), lambda i: (i, 0))
    once = lambda a: pl.BlockSpec(a.shape, lambda *_: (0,) * a.ndim, pipeline_mode=pl.Buffered(1))
    in_specs = [row(d)] + [row(a.shape[1]) for a in a_list] + [once(w) for w in w_list]
    args = [res, *a_list, *w_list]
    if gate is not None:
        in_specs.append(row(gate.shape[1]))
        args.append(gate)
    consts = [g.reshape(1, d), wg, wu, wd] + ([final_g.reshape(1, d)] if final_g is not None else [])
    return pl.pallas_call(
        functools.partial(_mix_ffn_kernel, len(a_list), gate is not None, final_g is not None),
        grid=(m // ROW_TILE,),
        in_specs=in_specs + [once(c) for c in consts],
        out_specs=row(d),
        out_shape=jax.ShapeDtypeStruct((m, d), F32),
        compiler_params=_cparams(1),
        name="mix_out_ffn",
    )(*args, *consts)


def _compress_kernel(x_ref, p_ref, w1_ref, w2_ref, rows_ref, cols_ref):
    n = x_ref.shape[1] // CMP_STRIDE
    a = jnp.zeros((n, x_ref.shape[2]), F32)
    b = jnp.zeros((n, x_ref.shape[2]), F32)
    for r in range(CMP_STRIDE):
        x = x_ref[0, pl.ds(r, n, stride=CMP_STRIDE), :]
        a = a + _dot((x + p_ref[0, r]).astype(BF16), w1_ref[0, r])
        b = b + _dot((x + p_ref[0, CMP_STRIDE + r]).astype(BF16), w1_ref[0, CMP_STRIDE + r])
    h = a + pltpu.roll(b, n - 1, 0)
    out = _dot(_silu(h).astype(BF16), w2_ref[0])
    live = lax.broadcasted_iota(jnp.int32, out.shape, 0) < n - 1
    out = jnp.where(live, out, 0.0)
    rows_ref[0, 0] = out.astype(rows_ref.dtype)
    cols_ref[0, 0] = out.T.astype(cols_ref.dtype)


def _compress(x, pos, w1, w2):
    b, t, two_gd = x.shape
    gd = two_gd // 2
    n = t // CMP_STRIDE
    return pl.pallas_call(
        _compress_kernel,
        grid=(2, b),
        in_specs=[
            pl.BlockSpec((1, t, gd), lambda s, i: (i, 0, s)),
            pl.BlockSpec((1,) + pos.shape[1:], lambda s, i: (s, 0, 0, 0)),
            pl.BlockSpec((1,) + w1.shape[1:], lambda s, i: (s, 0, 0, 0)),
            pl.BlockSpec((1,) + w2.shape[1:], lambda s, i: (s, 0, 0)),
        ],
        out_specs=[pl.BlockSpec((1, 1, n, gd), lambda s, i: (s, i, 0, 0)),
                   pl.BlockSpec((1, 1, gd, n), lambda s, i: (s, i, 0, 0))],
        out_shape=[jax.ShapeDtypeStruct((2, b, n, gd), BF16), jax.ShapeDtypeStruct((2, b, gd, n), BF16)],
        compiler_params=_cparams(2),
        name="compress_kv",
    )(x, pos, w1, w2)


def _stack_heads_t(q_ref, rep):
    qt = q_ref[0].astype(F32).T
    return jnp.concatenate([qt[r * HEAD_DIM:(r + 1) * HEAD_DIM] for r in range(rep)], axis=1).astype(BF16)


def _unstack_heads_t(o_t, rep):
    tq = o_t.shape[1] // rep
    return jnp.concatenate([o_t[:, r * tq:(r + 1) * tq] for r in range(rep)], axis=0).T


def _tile_lanes(x, rep):
    return jnp.concatenate([x] * rep, axis=1)


def _col_tree(x, pair, reduce):
    while x.shape[0] > SUBLANES:
        half = x.shape[0] // 2
        x = pair(x[:half], x[half:])
    return reduce(x, axis=0, keepdims=True)


def _attn_kernel(n_sel_blocks, sink_ref, qa_ref, qb_ref, gl_ref, gb_ref, kc_ref, vct_ref, selmap_ref,
                 ksd_ref, ks_ref, mask_ref, vsd_ref, vbig_ref, kw_ref, vw_ref, kbp_ref, kbc_ref, vbp_ref, vbc_ref,
                 eye_ref, cmask_ref, dmask_ref, wmask_ref, bmask_ref, oa_ref, ob_ref, m_sc, acc_sc, s_sc):
    n = pl.program_id(1)
    rep = NSA_REP
    n_groups = ksd_ref.shape[2] // HEAD_DIM
    gcols = rep * Q_BLOCK
    groups = range(n_groups)
    tq_row = n * Q_BLOCK + lax.broadcasted_iota(jnp.int32, (1, Q_BLOCK), 1)

    def pad_q(q_t):
        zero = jnp.zeros((HEAD_DIM, gcols), q_t.dtype)
        return jnp.concatenate([jnp.concatenate([q_t[:, i * gcols:(i + 1) * gcols] if i == j else zero
                                                 for i in groups], axis=1) for j in groups], axis=0)

    def per_group(fn):
        return jnp.concatenate([fn(i) for i in groups], axis=1)

    def apply_values(v_t, e):
        return per_group(lambda i: _dot(with_ones(v_t[i * HEAD_DIM:(i + 1) * HEAD_DIM]),
                                        e[:, i * gcols:(i + 1) * gcols]))

    def masked_logits(keys, mask, q_rows):
        return _dot(jnp.concatenate([keys, mask], axis=1), jnp.concatenate([q_rows, eye_ref[...]], axis=0))

    def with_ones(v_t):
        return jnp.concatenate([v_t, jnp.ones((SUBLANES, v_t.shape[1]), v_t.dtype)], axis=0)

    q_t = pad_q(_stack_heads_t(qa_ref, n_groups * rep))
    kc = kc_ref[0, 0]
    n_cmp = kc.shape[0]
    odd = lax.rem(n, 2)
    shift = Q_BLOCK // CMP_STRIDE
    start = pl.multiple_of(n_cmp - shift * (n + odd), 2 * SUBLANES)
    s_c = masked_logits(kc, cmask_ref[odd, pl.ds(start, n_cmp), :], q_t)
    q_b = pad_q(_stack_heads_t(qb_ref, n_groups * SWA_REP))
    k_b = jnp.concatenate([kbp_ref[0], kbc_ref[0]], axis=0)
    s_b = masked_logits(k_b, bmask_ref[jnp.minimum(n, bmask_ref.shape[0] - 1)], q_b)
    n_win_tiles = NSA_WINDOW // Q_BLOCK + 1
    tiles = [jnp.maximum(n - (n_win_tiles - 1) + i, 0) for i in range(n_win_tiles)]
    k_w = jnp.concatenate([kw_ref[0, pl.ds(pl.multiple_of(j * Q_BLOCK, Q_BLOCK), Q_BLOCK), :] for j in tiles], axis=0)
    s_w = masked_logits(k_w, wmask_ref[jnp.minimum(n, wmask_ref.shape[0] - 1)], q_t)
    s_d = masked_logits(ksd_ref[0], dmask_ref[...], q_t)

    m = jnp.max(s_c, axis=0, keepdims=True)
    e_c = jnp.exp2(s_c - m).astype(BF16)
    vct = vct_ref[0, 0]
    oc_imp = per_group(lambda i: _dot(
        jnp.concatenate([with_ones(vct[i * HEAD_DIM:(i + 1) * HEAD_DIM]), selmap_ref[...]], axis=0),
        e_c[:, i * gcols:(i + 1) * gcols]))
    den = oc_imp[HEAD_DIM:HEAD_DIM + 1]
    seen = m > 0.5 * NEG_INF
    oc_imp = jnp.where(seen, oc_imp * (1.0 / jnp.where(seen, den, 1.0)), 0.0)
    o_c = oc_imp[:HEAD_DIM]
    imp = oc_imp[HEAD_DIM + SUBLANES:]
    imp = per_group(lambda i: sum(imp[:, (i * rep + r) * Q_BLOCK:(i * rep + r + 1) * Q_BLOCK]
                                  for r in range(rep)))

    sink_row = jnp.concatenate([jnp.full((1, Q_BLOCK), sink_ref[h] * LOG2_E, F32)
                                for h in range(n_groups * SWA_REP)], axis=1)
    v_tiles_b = [vbp_ref[0, 0, 0], vbc_ref[0, 0, 0]]
    v_tiles_w = [vw_ref[0, 0, j] for j in tiles]
    v_tiles_d = [vsd_ref[0, 0, 0]]
    maxes, partial = {}, {"b": [], "w": [], "d": []}

    def col_max(name, s, extra=None):
        def run():
            m = jnp.max(s, axis=0, keepdims=True)
            maxes[name] = m if extra is None else jnp.maximum(m, extra)
        return run

    def pv_tile(name, s, v_tiles, i):
        def run():
            rows = slice(i * Q_BLOCK, (i + 1) * Q_BLOCK)
            partial[name].append(apply_values(v_tiles[i], jnp.exp2(s[rows] - maxes[name]).astype(BF16)))
        return run

    pieces = ([col_max("w", s_w), col_max("b", s_b, sink_row), col_max("d", s_d)]
              + [pv_tile("w", s_w, v_tiles_w, i) for i in range(len(v_tiles_w))]
              + [pv_tile("b", s_b, v_tiles_b, i) for i in range(len(v_tiles_b))]
              + [pv_tile("d", s_d, v_tiles_d, 0)])

    ns_pad = imp.shape[0]
    blk = lax.broadcasted_iota(jnp.int32, (ns_pad, n_groups * Q_BLOCK), 0)
    blk_f = blk.astype(F32)
    tq_row = _tile_lanes(tq_row, n_groups)
    cur = jnp.right_shift(tq_row, SEL_SHIFT)
    forced = (blk == 0) | (blk == cur) | (blk == cur - 1)
    valid = jnp.left_shift(blk, SEL_SHIFT) <= tq_row
    score = jnp.where(valid, imp, NEG_INF)
    score = jnp.where(forced | (blk >= n_sel_blocks), BELOW_NEG_INF, score)
    sel = forced
    for _ in range(min(SEL_TOPK, n_sel_blocks) - 3):
        if pieces:
            pieces.pop(0)()
        top = _col_tree(score, jnp.maximum, jnp.max)
        first_hit = _col_tree(jnp.where(score == top, blk_f, float(ns_pad)), jnp.minimum, jnp.min)
        pick = blk_f == first_hit
        sel = sel | pick
        score = jnp.where(pick, BELOW_NEG_INF, score)
    n_diag_blk = n * (Q_BLOCK // SEL_LEN)
    unselected = jnp.where(sel & (blk < n_diag_blk), 0.0, 1.0).astype(BF16)

    while pieces:
        pieces.pop(0)()

    def finish(name, sink=None):
        acc = sum(partial[name][1:], partial[name][0])
        den = acc[HEAD_DIM:HEAD_DIM + 1]
        if sink is not None:
            den = den + jnp.exp2(sink - maxes[name])
        return acc[:HEAD_DIM] * (1.0 / den)

    m_sc[...] = maxes["d"]
    acc_sc[...] = partial["d"][0]

    q_aug = jnp.concatenate([per_group(lambda i: _tile_lanes(unselected[:, i * Q_BLOCK:(i + 1) * Q_BLOCK], rep)),
                             q_t], axis=0)
    tk = s_sc.shape[1]
    n_tiles = ks_ref.shape[1] // tk

    def logits(kt):
        keys = pl.ds(pl.multiple_of(kt * tk, tk), tk)
        return _dot(jnp.concatenate([mask_ref[keys, :], ks_ref[0, keys, :]], axis=1), q_aug)

    def absorb(kt, s):
        m_old = m_sc[...]
        m_new = jnp.maximum(m_old, jnp.max(s, axis=0, keepdims=True))
        pv = apply_values(vbig_ref[0, kt], jnp.exp2(s - m_new).astype(BF16))
        acc_sc[...] = jnp.exp2(m_old - m_new) * acc_sc[...] + pv
        m_sc[...] = m_new

    s_sc[0] = logits(0)
    ob_ref[0] = _unstack_heads_t(finish("b", sink_row), n_groups * SWA_REP).astype(ob_ref.dtype)
    o_w = finish("w")

    def tile_pair(j, carry):
        s_sc[1] = logits(2 * j + 1)
        absorb(2 * j, s_sc[0])
        s_sc[0] = logits(jnp.minimum(2 * j + 2, n_tiles - 1))
        absorb(2 * j + 1, s_sc[1])
        return carry

    lax.fori_loop(0, lax.div(n * Q_BLOCK + 2 * tk - 1, 2 * tk), tile_pair, 0)
    acc = acc_sc[...]
    o_s = acc[:HEAD_DIM] * (1.0 / acc[HEAD_DIM:HEAD_DIM + 1])

    gates = _sigmoid(gl_ref[0, 0] + gb_ref[...])
    gate = lambda c: jnp.concatenate([gates[16 * i + 3 * r + c:16 * i + 3 * r + c + 1]
                                      for i in groups for r in range(rep)], axis=1)
    o = gate(0) * o_c + gate(1) * o_s + gate(2) * o_w
    oa_ref[0] = _unstack_heads_t(o, n_groups * rep).astype(oa_ref.dtype)


def _selection_map(t, nc_pad, ns_pad):
    nc = (t - CMP_LEN) // CMP_STRIDE + 1
    ns = t // SEL_LEN
    cs = np.arange(nc, dtype=np.int64)[:, None] * CMP_STRIDE
    ss = np.arange(ns, dtype=np.int64)[None, :] * SEL_LEN
    out = np.zeros((nc_pad, ns_pad), np.float32)
    out[:nc, :ns] = (cs < ss + SEL_LEN) & (cs + CMP_LEN > ss)
    return out


def _block_mask_cols(t, ns_pad):
    hit = np.arange(ns_pad, dtype=np.int64)[None, :] == (np.arange(t, dtype=np.int64) // SEL_LEN)[:, None]
    return np.where(hit, NEG_INF, 0.0).astype(np.float32)


def _band_masks(n_rows, window):
    n_front = n_rows // Q_BLOCK - 1
    r = np.arange(n_rows)[None, :, None]
    tok = np.arange(Q_BLOCK)[None, None, :]
    v = np.arange(n_front + 1)[:, None, None]
    dist = n_front * Q_BLOCK + tok - r
    ok = (dist >= 0) & (dist < window) & (r >= (n_front - v) * Q_BLOCK)
    return np.where(ok, 0.0, NEG_INF).astype(np.float32)


def _compressed_masks(n_cmp):
    shift = Q_BLOCK // CMP_STRIDE
    u = np.arange(2 * n_cmp)[None, :, None] + shift * np.arange(2)[:, None, None]
    tok = np.arange(Q_BLOCK)[None, None, :]
    ok = CMP_STRIDE * (u - n_cmp) + CMP_LEN - 1 <= tok
    return np.where(ok, 0.0, NEG_INF).astype(np.float32)


def _attention(q, k_all, v_big, v_small, gate_t, gate_bias, cmp_rows, cmp_cols, sel_map_t, mask_cols, sinks,
               n_sel_blocks):
    b, t = q.shape[:2]
    gd = k_all.shape[2] // 3
    g = gd // HEAD_DIM
    tk = v_big.shape[3]
    assert (t // tk) % 2 == 0, "selected-branch key tiles are consumed in pairs"
    width = g * NSA_REP * HEAD_DIM
    cols = g * NSA_REP * Q_BLOCK
    gate_cols = gate_t.shape[3] // Q_BLOCK
    prev = lambda n: jnp.maximum(n - 1, 0)
    cur = lambda n: n
    k_tile = lambda blk, at: pl.BlockSpec((1, Q_BLOCK, gd), lambda i, n: (i, at(n), blk))
    k_all_rows = lambda blk: pl.BlockSpec((1, t, gd), lambda i, n: (i, 0, blk))
    v_tile = lambda a, at: pl.BlockSpec((1, 1, 1, gd, Q_BLOCK), lambda i, n: (a, i, at(n), 0, 0))
    out_spec = pl.BlockSpec((1, Q_BLOCK, width), lambda i, n: (i, n, 0))
    out_shape = jax.ShapeDtypeStruct((b, t, width), BF16)
    tri = np.where(np.arange(Q_BLOCK)[:, None] <= np.arange(Q_BLOCK)[None, :], 0.0, NEG_INF)
    consts = [jnp.asarray(c, BF16) for c in (
        np.tile(np.eye(Q_BLOCK, dtype=np.float32), (1, g * NSA_REP)),
        _compressed_masks(cmp_rows.shape[2]), tri,
        _band_masks(NSA_WINDOW + Q_BLOCK, NSA_WINDOW), _band_masks(SWA_WINDOW + Q_BLOCK, SWA_WINDOW))]
    return pl.pallas_call(
        functools.partial(_attn_kernel, n_sel_blocks),
        grid=(b, t // Q_BLOCK),
        in_specs=[
            pl.BlockSpec(memory_space=pltpu.SMEM),
            pl.BlockSpec((1, Q_BLOCK, width), lambda i, n: (i, n, 0)),
            pl.BlockSpec((1, Q_BLOCK, width), lambda i, n: (i, n, 1)),
            pl.BlockSpec((1, 1, 16 * g, Q_BLOCK), lambda i, n: (i, n // gate_cols, 0, n % gate_cols)),
            _resident(gate_bias.shape),
            pl.BlockSpec((1, 1) + cmp_rows.shape[2:], lambda i, n: (0, i, 0, 0)),
            pl.BlockSpec((1, 1) + cmp_cols.shape[2:], lambda i, n: (1, i, 0, 0)),
            _resident(sel_map_t.shape),
            k_tile(0, cur), k_all_rows(0), _resident(mask_cols.shape),
            v_tile(0, cur), pl.BlockSpec((1,) + v_big.shape[1:], lambda i, n: (i, 0, 0, 0)),
            k_all_rows(1), pl.BlockSpec((1, 1) + v_small.shape[2:], lambda i, n: (1, i, 0, 0, 0)),
            k_tile(2, prev), k_tile(2, cur), v_tile(2, prev), v_tile(2, cur),
        ] + [_resident(c.shape) for c in consts],
        out_specs=[out_spec, out_spec],
        out_shape=[out_shape, out_shape],
        scratch_shapes=[
            pltpu.VMEM((1, cols), F32),
            pltpu.VMEM((HEAD_DIM + SUBLANES, cols), F32),
            pltpu.VMEM((2, tk, cols), F32),
        ],
        compiler_params=_cparams(2),
        name="nsa_swa_attention",
    )(sinks, q, q, gate_t, gate_bias, cmp_rows, cmp_cols, sel_map_t, k_all, k_all, mask_cols, v_small, v_big,
      k_all, v_small, k_all, k_all, v_small, v_small, *consts)


def _conv_silu_kernel(n_q_cols, x_ref, prev_ref, w_ref, b_ref, o_ref, buf):
    i = pl.program_id(1)
    rows = x_ref.shape[1]
    halo = prev_ref.shape[1]
    buf[:halo, :] = jnp.where(i > 0, prev_ref[0].astype(F32), 0.0)
    buf[halo:, :] = x_ref[0].astype(F32)
    y = b_ref[...]
    for j in range(CONV_WIDTH):
        start = halo - (CONV_WIDTH - 1) + j
        y = y + w_ref[j:j + 1, :] * buf[start:start + rows, :]
    y = _silu(y)
    col = lax.broadcasted_iota(jnp.int32, y.shape, 1)
    o_ref[0] = jnp.where(col >= n_q_cols, y * (MLSTM_QK_DIM ** -0.5), y).astype(o_ref.dtype)


def _conv_silu(x, w, bias):
    b, t, c = x.shape
    halo = 2 * SUBLANES
    ratio = ROW_TILE // halo
    return pl.pallas_call(
        functools.partial(_conv_silu_kernel, c // 2),
        grid=(b, t // ROW_TILE),
        in_specs=[
            pl.BlockSpec((1, ROW_TILE, c), lambda i, j: (i, j, 0)),
            pl.BlockSpec((1, halo, c), lambda i, j: (i, jnp.maximum(j * ratio - 1, 0), 0)),
            _resident(w.shape),
            _resident((1, c)),
        ],
        out_specs=pl.BlockSpec((1, ROW_TILE, c), lambda i, j: (i, j, 0)),
        out_shape=jax.ShapeDtypeStruct((b, t, c), BF16),
        scratch_shapes=[pltpu.VMEM((ROW_TILE + halo, c), F32)],
        compiler_params=_cparams(2),
        name="conv_silu",
    )(x, x, w, bias.reshape(1, c))


def _mlstm_kernel(bias_ref, q_ref, k_ref, v_ref, ig_ref, fg_ref, o_ref, state, m_state):
    c = pl.program_id(1)
    L = SCAN_CHUNK
    dqk, dv = MLSTM_QK_DIM, MLSTM_V_DIM

    @pl.when(c == 0)
    def _():
        state[...] = jnp.zeros(state.shape, F32)
        m_state[...] = jnp.zeros(m_state.shape, F32)

    t_idx = lax.broadcasted_iota(jnp.int32, (L, L), 0)
    s_idx = lax.broadcasted_iota(jnp.int32, (L, L), 1)
    causal = s_idx <= t_idx
    eye = s_idx == t_idx
    ones_col = (lax.broadcasted_iota(jnp.int32, (L, LANES), 1) == 0).astype(BF16)

    def to_col(row):
        return jnp.sum(jnp.where(eye, row, 0.0), axis=1, keepdims=True)

    heads = range(MLSTM_HEADS)
    for j in range(SCAN_ROWS // SCAN_CHUNK):
        rows = slice(j * L, (j + 1) * L)
        qc = [q_ref[0, rows, h * dqk:(h + 1) * dqk] for h in heads]
        kc = [k_ref[0, rows, h * dqk:(h + 1) * dqk] for h in heads]
        va = [jnp.concatenate([v_ref[0, rows, h * dv:(h + 1) * dv], ones_col], axis=1) for h in heads]
        qk = [lax.dot_general(qc[h], kc[h], (((1,), (1,)), ((), ())), preferred_element_type=F32) for h in heads]
        i_row, b_col, b_row = [], [], []
        for h in heads:
            i_row.append(ig_ref[0, h:h + 1, rows] + bias_ref[0, h])
            f_pre = fg_ref[0, h:h + 1, rows] + bias_ref[1, h]
            f_row = jnp.minimum(f_pre, 0.0) - jnp.log(1.0 + jnp.exp(-jnp.abs(f_pre)))
            f_col = to_col(f_row)
            b_col.append(jnp.sum(jnp.where(causal, f_row, 0.0), axis=1, keepdims=True))
            b_row.append(jnp.sum(jnp.where(t_idx <= s_idx, f_col, 0.0), axis=0, keepdims=True))
        m_old = [m_state[h:h + 1, 0:1] for h in heads]
        st = [state[h] for h in heads]
        mt, w_inter, sm = [], [], []
        for h in heads:
            a = b_col[h] + m_old[h]
            d = jnp.where(causal, b_col[h] - b_row[h] + i_row[h], -jnp.inf)
            mt.append(jnp.maximum(a, jnp.max(d, axis=1, keepdims=True)))
            w_inter.append(jnp.exp(a - mt[h]))
            sm.append((qk[h] * jnp.exp(d - mt[h])).astype(BF16))
        for h in heads:
            nd = w_inter[h] * _dot(qc[h], st[h].astype(BF16)) + _dot(sm[h], va[h])
            hc = nd[:, :dv] * (1.0 / jnp.maximum(jnp.abs(nd[:, dv:dv + 1]), jnp.exp(-mt[h])))
            o_ref[0, rows, h * dv:(h + 1) * dv] = hc.astype(o_ref.dtype)
        for h in heads:
            b_last = b_row[h][:, L - 1:L]
            wl = b_last - b_row[h] + i_row[h]
            m_new = jnp.maximum(b_last + m_old[h], jnp.max(wl, axis=1, keepdims=True))
            decay = jnp.exp(b_last + m_old[h] - m_new)
            w_col = to_col(jnp.exp(wl - m_new))
            kw = (kc[h].astype(F32) * w_col).astype(BF16)
            upd = lax.dot_general(kw, va[h], (((0,), (0,)), ((), ())), preferred_element_type=F32)
            state[h] = decay * st[h] + upd
            m_state[h:h + 1, 0:1] = m_new


def _mlstm(qk, v, ig, fg, gate_bias):
    b, t = qk.shape[:2]
    nh = MLSTM_HEADS
    rows = SCAN_ROWS
    gate_spec = pl.BlockSpec((1, nh, rows), lambda i, c: (i, 0, c))
    return pl.pallas_call(
        _mlstm_kernel,
        grid=(b, t // rows),
        in_specs=[
            pl.BlockSpec(memory_space=pltpu.SMEM),
            pl.BlockSpec((1, rows, nh * MLSTM_QK_DIM), lambda i, c: (i, c, 0)),
            pl.BlockSpec((1, rows, nh * MLSTM_QK_DIM), lambda i, c: (i, c, 1)),
            pl.BlockSpec((1, rows, nh * MLSTM_V_DIM), lambda i, c: (i, c, 0)),
            gate_spec, gate_spec,
        ],
        out_specs=pl.BlockSpec((1, rows, nh * MLSTM_V_DIM), lambda i, c: (i, c, 0)),
        out_shape=jax.ShapeDtypeStruct((b, t, nh * MLSTM_V_DIM), BF16),
        scratch_shapes=[
            pltpu.VMEM((nh, MLSTM_QK_DIM, MLSTM_V_DIM + LANES), F32),
            pltpu.VMEM((SUBLANES, LANES), F32),
        ],
        compiler_params=_cparams(2),
        name="mlstm_scan",
    )(gate_bias, qk, qk, v, ig, fg)


def _block_diag(w, n):
    zero = jnp.zeros_like(w)
    return jnp.concatenate([jnp.concatenate([w if i == j else zero for i in range(n)], axis=-1)
                            for j in range(n)], axis=-2)


def _nsa_swa_layer(h, norm_g, w_in, gate_bias, pos_k, pos_v, ck_w1, ck_w2, cv_w1, cv_w2, sinks, w_out):
    b, t, d = h.shape
    m = b * t
    ng = NSA_KV_HEADS
    gk = ng * HEAD_DIM
    nq_a, nq_b = NSA_HEADS * HEAD_DIM, SWA_HEADS * HEAD_DIM
    ngate = NSA_HEADS * 3
    o_kc, o_vc, o_ks, o_vs, o_kw, o_vw = (nq_a + i * gk for i in range(6))
    o_g = nq_a + 6 * gk
    o_qb = o_g + ngate
    o_kb, o_vb = o_qb + nq_b, o_qb + nq_b + gk
    cols = lambda o, n: w_in[:, o:o + n]
    cat = lambda parts: jnp.concatenate(parts, axis=1).astype(BF16)
    q_scale = HEAD_DIM ** -0.5 * LOG2_E
    gate_pad = jnp.zeros((d, 16 - NSA_REP * 3), w_in.dtype)
    w_gate = jnp.concatenate([p for i in range(ng) for p in (cols(o_g + i * NSA_REP * 3, NSA_REP * 3), gate_pad)]
                             + [jnp.zeros((d, LANES - 16 * ng), w_in.dtype)], axis=1)
    q_all, k_all, cmp_in, v_big, v_small, gate_t = _proj0(
        h.reshape(m, d), norm_g, (jnp.concatenate([cols(0, nq_a), cols(o_qb, nq_b)], axis=1) * q_scale).astype(BF16),
        cat([cols(o_ks, gk), cols(o_kw, gk), cols(o_kb, gk)]), cat([cols(o_vs, gk), cols(o_vw, gk), cols(o_vb, gk)]),
        cat([cols(o_kc, gk), cols(o_vc, gk), w_gate]), 2 * gk, 16 * ng)
    tk = ROW_TILE
    q_all = q_all.reshape(b, t, nq_a + nq_b)
    k_all = k_all.reshape(b, t, 3 * gk)
    v_big = v_big.reshape(b, t // tk, gk, tk)
    v_small = v_small.reshape(3, b, t // Q_BLOCK, gk, Q_BLOCK)
    gate_t = gate_t.reshape(b, t // ROW_TILE, 16 * ng, ROW_TILE)
    gb = jnp.pad(gate_bias.reshape(ng, NSA_REP * 3), ((0, 0), (0, 16 - NSA_REP * 3))).reshape(ng * 16, 1)

    tile2 = lambda p: jnp.tile(p[:, None, :], (1, 1, ng))
    pos = jnp.stack([tile2(pos_k), tile2(pos_v)])
    per_pos = lambda w: w.reshape(CMP_LEN, HEAD_DIM, HEAD_DIM)
    w1 = _block_diag(jnp.stack([per_pos(ck_w1), per_pos(cv_w1)]), ng).astype(BF16)
    w2 = _block_diag(jnp.stack([ck_w2, cv_w2]), ng).astype(BF16)
    cmp_rows, cmp_cols = _compress(cmp_in.reshape(b, t, 2 * gk), pos, w1, w2)

    n_sel = t // SEL_LEN
    ns_pad = -(-n_sel // LANES) * LANES
    sel_map_t = jnp.asarray(_selection_map(t, t // CMP_STRIDE, ns_pad).T, BF16)
    mask_cols = jnp.asarray(_block_mask_cols(t, ns_pad), BF16)
    o_a, o_b = _attention(q_all, k_all, v_big, v_small, gate_t, gb, cmp_rows, cmp_cols, sel_map_t, mask_cols,
                          sinks, n_sel)

    w_o = w_out.astype(BF16)
    return [o_a.reshape(m, nq_a), o_b.reshape(m, nq_b)], [w_o[:nq_a], w_o[nq_a:]], None


def _mlstm_layer(h, norm_g, w_in, conv_w, conv_b, igate_bias, fgate_bias, w_out):
    b, t, d = h.shape
    m = b * t
    nqk = 2 * MLSTM_HEADS * MLSTM_QK_DIM
    nv = MLSTM_HEADS * MLSTM_V_DIM
    w = w_in.astype(BF16)
    w_gates = jnp.concatenate([w[:, nqk + 2 * nv:], jnp.zeros((d, LANES - 2 * MLSTM_HEADS), BF16)], axis=1)
    qk_raw, v, og, gates = _norm_proj(
        h.reshape(m, d), norm_g,
        [w[:, :nqk], w[:, nqk:nqk + nv], w[:, nqk + nv:nqk + 2 * nv], w_gates],
        [BF16, BF16, BF16, F32])
    qk = _conv_silu(qk_raw.reshape(b, t, nqk), conv_w, conv_b)
    gates = gates.reshape(b, t, -1)
    ig = gates[..., :MLSTM_HEADS].transpose(0, 2, 1)
    fg = gates[..., MLSTM_HEADS:2 * MLSTM_HEADS].transpose(0, 2, 1)
    hh = _mlstm(qk, v.reshape(b, t, nv), ig, fg, jnp.stack([igate_bias, fgate_bias]))
    return [hh.reshape(m, nv)], [w_out.astype(BF16)], og


def kernel(x, norm_mix, norm_ffn, ffn_w_gate, ffn_w_up, ffn_w_down, ab_w_in, ab_gate_bias, nsa_pos_k, nsa_pos_v, nsa_cmp_k_w1, nsa_cmp_k_w2, nsa_cmp_v_w1, nsa_cmp_v_w2, swa_sinks, ab_w_out, c_w_in, c_conv_w, c_conv_b, c_igate_bias, c_fgate_bias, c_w_out, final_norm):
    depth = norm_mix.shape[0]
    b, t, d = x.shape
    h = x
    for layer in range(depth):
        j = layer // 2
        if layer % 2 == 0:
            mixed = _nsa_swa_layer(h, norm_mix[layer], ab_w_in[j], ab_gate_bias[j], nsa_pos_k[j], nsa_pos_v[j],
                               nsa_cmp_k_w1[j], nsa_cmp_k_w2[j], nsa_cmp_v_w1[j], nsa_cmp_v_w2[j],
                               swa_sinks[j], ab_w_out[j])
        else:
            mixed = _mlstm_layer(h, norm_mix[layer], c_w_in[j], c_conv_w[j], c_conv_b[j], c_igate_bias[j],
                             c_fgate_bias[j], c_w_out[j])
        last = layer == depth - 1
        h = _mix_ffn(h.reshape(b * t, d), *mixed, norm_ffn[layer], ffn_w_gate[layer].astype(BF16),
                     ffn_w_up[layer].astype(BF16), ffn_w_down[layer].astype(BF16),
                     final_norm if last else None).reshape(b, t, d)
    if depth == 0:
        raise ValueError("depth must be positive")
    return h
```

```python
import functools

import numpy as np
import jax
import jax.numpy as jnp
from jax import lax
from jax.experimental import pallas as pl
from jax.experimental.pallas import tpu as pltpu

HEAD_DIM = 64
NSA_HEADS = 8
NSA_KV_HEADS = 2
NSA_REP = NSA_HEADS // NSA_KV_HEADS
CMP_LEN = 32
CMP_STRIDE = 16
SEL_LEN = 64
SEL_TOPK = 16
NSA_WINDOW = 512
SWA_HEADS = 8
SWA_KV_HEADS = 2
SWA_REP = SWA_HEADS // SWA_KV_HEADS
SWA_WINDOW = 128
Q_BLOCK = 128
MLSTM_HEADS = 4
MLSTM_QK_DIM = 128
MLSTM_V_DIM = 256
MLSTM_CHUNK = 64
CONV_WIDTH = 4
RMS_EPS = 1e-6
NEG_INF = -1e30
FORCE_SCORE = 1e9
BELOW_NEG_INF = -3e38
LOG2_E = 1.4426950408889634
SEL_SHIFT = SEL_LEN.bit_length() - 1
assert 1 << SEL_SHIFT == SEL_LEN

LANES = 128
SUBLANES = 8
VMEM_LIMIT_BYTES = 56 * 1024 * 1024

ROW_TILE = 512
SCAN_CHUNK = 512
SCAN_ROWS = 512

F32 = jnp.float32
BF16 = jnp.bfloat16


def _cparams(n_axes):
    return pltpu.CompilerParams(
        dimension_semantics=("arbitrary",) * n_axes,
        vmem_limit_bytes=VMEM_LIMIT_BYTES,
    )


def _resident(shape):
    nd = len(shape)
    return pl.BlockSpec(shape, lambda *_: (0,) * nd)


def _dot(a, b):
    return jnp.dot(a, b, preferred_element_type=F32)


def _sigmoid(x):
    return 1.0 / (1.0 + jnp.exp(-x))


def _silu(x):
    return x * _sigmoid(x)


def _rmsnorm(x, g):
    y = x * lax.rsqrt(jnp.mean(x * x, axis=-1, keepdims=True) + RMS_EPS)
    return y * g


def _col_ranges(widths):
    ends = np.cumsum(widths).tolist()
    return list(zip([0] + ends[:-1], ends))


def _norm_proj_kernel(widths, x_ref, g_ref, w_ref, *o_refs):
    xn = _rmsnorm(x_ref[...], g_ref[...]).astype(BF16)
    for (lo, hi), o_ref in zip(_col_ranges(widths), o_refs):
        o_ref[...] = _dot(xn, w_ref[:, lo:hi]).astype(o_ref.dtype)


def _norm_proj(x, g, w, widths, out_dtypes):
    m, d = x.shape
    return pl.pallas_call(
        functools.partial(_norm_proj_kernel, tuple(widths)),
        grid=(m // ROW_TILE,),
        in_specs=[pl.BlockSpec((ROW_TILE, d), lambda i: (i, 0)), _resident((1, d)), _resident(w.shape)],
        out_specs=[pl.BlockSpec((ROW_TILE, n), lambda i: (i, 0)) for n in widths],
        out_shape=[jax.ShapeDtypeStruct((m, n), dt) for n, dt in zip(widths, out_dtypes)],
        compiler_params=_cparams(1),
        name="norm_proj",
    )(x, g.reshape(1, d), w)


def _proj0_kernel(widths, x_ref, g_ref, w_ref, q_ref, k_ref, c_ref, vbig_ref, vsmall_ref, gt_ref):
    (q0, q1), (k0, k1), (v0, v1), (c0, c1) = _col_ranges(widths)
    xn = _rmsnorm(x_ref[...], g_ref[...]).astype(BF16)
    q_ref[...] = _dot(xn, w_ref[:, q0:q1]).astype(q_ref.dtype)
    k_ref[...] = _dot(xn, w_ref[:, k0:k1]).astype(k_ref.dtype)
    cg = _dot(xn, w_ref[:, c0:c1])
    c_ref[...] = cg[:, :c_ref.shape[1]]
    gt_ref[0] = cg[:, c_ref.shape[1]:].T[:gt_ref.shape[1]]
    v = _dot(xn, w_ref[:, v0:v1])
    width = v.shape[1] // 3
    for a in range(3):
        v_t = v[:, a * width:(a + 1) * width].T.astype(BF16)
        if a == 0:
            vbig_ref[0] = v_t
        for j in range(x_ref.shape[0] // Q_BLOCK):
            vsmall_ref[a, j] = v_t[:, j * Q_BLOCK:(j + 1) * Q_BLOCK]


def _proj0(x, g, w, widths, n_cmp_cols, gate_rows):
    m, d = x.shape
    steps = m // ROW_TILE
    tiles = ROW_TILE // Q_BLOCK
    nq, nk, nv, _ = widths
    gd = nv // 3
    row = lambda n: pl.BlockSpec((ROW_TILE, n), lambda i: (i, 0))
    return pl.pallas_call(
        functools.partial(_proj0_kernel, tuple(widths)),
        grid=(steps,),
        in_specs=[row(d), _resident((1, d)), _resident(w.shape)],
        out_specs=[
            row(nq), row(nk), row(n_cmp_cols),
            pl.BlockSpec((1, gd, ROW_TILE), lambda i: (i, 0, 0)),
            pl.BlockSpec((3, tiles, gd, Q_BLOCK), lambda i: (0, i, 0, 0)),
            pl.BlockSpec((1, gate_rows, ROW_TILE), lambda i: (i, 0, 0)),
        ],
        out_shape=[
            jax.ShapeDtypeStruct((m, nq), BF16),
            jax.ShapeDtypeStruct((m, nk), BF16),
            jax.ShapeDtypeStruct((m, n_cmp_cols), F32),
            jax.ShapeDtypeStruct((steps, gd, ROW_TILE), BF16),
            jax.ShapeDtypeStruct((3, steps * tiles, gd, Q_BLOCK), BF16),
            jax.ShapeDtypeStruct((steps, gate_rows, ROW_TILE), F32),
        ],
        compiler_params=_cparams(1),
        name="norm_proj_attn",
    )(x, g.reshape(1, d), w)


def _mix_ffn_kernel(n_in, gated, final, res_ref, *refs):
    a_refs, w_refs = refs[:n_in], refs[n_in:2 * n_in]
    rest = list(refs[2 * n_in:])
    o_ref = rest.pop()
    gate_ref = rest.pop(0) if gated else None
    g_ref, wg_ref, wu_ref, wd_ref = rest[:4]
    h = res_ref[...]
    for idx, (a_ref, w_ref) in enumerate(zip(a_refs, w_refs)):
        a = a_ref[...]
        if gated and idx == 0:
            a = (a.astype(F32) * _sigmoid(gate_ref[...].astype(F32))).astype(BF16)
        h = h + _dot(a, w_ref[...])
    xn = _rmsnorm(h, g_ref[...]).astype(BF16)
    act = (_silu(_dot(xn, wg_ref[...])) * _dot(xn, wu_ref[...])).astype(BF16)
    out = h + _dot(act, wd_ref[...])
    if final:
        out = _rmsnorm(out, rest[4][...])
    o_ref[...] = out


def _mix_ffn(res, a_list, w_list, gate, g, wg, wu, wd, final_g=None):
    m, d = res.shape
    row = lambda w: pl.BlockSpec((ROW_TILE, w), lambda i: (i, 0))
    once = lambda a: pl.BlockSpec(a.shape, lambda *_: (0,) * a.ndim, pipeline_mode=pl.Buffered(1))
    in_specs = [row(d)] + [row(a.shape[1]) for a in a_list] + [once(w) for w in w_list]
    args = [res, *a_list, *w_list]
    if gate is not None:
        in_specs.append(row(gate.shape[1]))
        args.append(gate)
    consts = [g.reshape(1, d), wg, wu, wd] + ([final_g.reshape(1, d)] if final_g is not None else [])
    return pl.pallas_call(
        functools.partial(_mix_ffn_kernel, len(a_list), gate is not None, final_g is not None),
        grid=(m // ROW_TILE,),
        in_specs=in_specs + [once(c) for c in consts],
        out_specs=row(d),
        out_shape=jax.ShapeDtypeStruct((m, d), F32),
        compiler_params=_cparams(1),
        name="mix_out_ffn",
    )(*args, *consts)


def _compress_kernel(x_ref, p_ref, w1_ref, w2_ref, rows_ref, cols_ref):
    n = x_ref.shape[1] // CMP_STRIDE
    a = jnp.zeros((n, x_ref.shape[2]), F32)
    b = jnp.zeros((n, x_ref.shape[2]), F32)
    for r in range(CMP_STRIDE):
        x = x_ref[0, pl.ds(r, n, stride=CMP_STRIDE), :]
        a = a + _dot((x + p_ref[0, r]).astype(BF16), w1_ref[0, r])
        b = b + _dot((x + p_ref[0, CMP_STRIDE + r]).astype(BF16), w1_ref[0, CMP_STRIDE + r])
    h = a + pltpu.roll(b, n - 1, 0)
    out = _dot(_silu(h).astype(BF16), w2_ref[0])
    live = lax.broadcasted_iota(jnp.int32, out.shape, 0) < n - 1
    out = jnp.where(live, out, 0.0)
    rows_ref[0, 0] = out.astype(rows_ref.dtype)
    cols_ref[0, 0] = out.T.astype(cols_ref.dtype)


def _compress(x, pos, w1, w2):
    b, t, two_gd = x.shape
    gd = two_gd // 2
    n = t // CMP_STRIDE
    return pl.pallas_call(
        _compress_kernel,
        grid=(2, b),
        in_specs=[
            pl.BlockSpec((1, t, gd), lambda s, i: (i, 0, s)),
            pl.BlockSpec((1,) + pos.shape[1:], lambda s, i: (s, 0, 0, 0)),
            pl.BlockSpec((1,) + w1.shape[1:], lambda s, i: (s, 0, 0, 0)),
            pl.BlockSpec((1,) + w2.shape[1:], lambda s, i: (s, 0, 0)),
        ],
        out_specs=[pl.BlockSpec((1, 1, n, gd), lambda s, i: (s, i, 0, 0)),
                   pl.BlockSpec((1, 1, gd, n), lambda s, i: (s, i, 0, 0))],
        out_shape=[jax.ShapeDtypeStruct((2, b, n, gd), BF16), jax.ShapeDtypeStruct((2, b, gd, n), BF16)],
        compiler_params=_cparams(2),
        name="compress_kv",
    )(x, pos, w1, w2)


def _stack_heads_t(q_ref, rep):
    qt = q_ref[0].astype(F32).T
    return jnp.concatenate([qt[r * HEAD_DIM:(r + 1) * HEAD_DIM] for r in range(rep)], axis=1).astype(BF16)


def _unstack_heads_t(o_t, rep):
    tq = o_t.shape[1] // rep
    return jnp.concatenate([o_t[:, r * tq:(r + 1) * tq] for r in range(rep)], axis=0).T


def _tile_lanes(x, rep):
    return jnp.concatenate([x] * rep, axis=1)


def _col_tree(x, pair, reduce):
    while x.shape[0] > SUBLANES:
        half = x.shape[0] // 2
        x = pair(x[:half], x[half:])
    return reduce(x, axis=0, keepdims=True)


def _attn_kernel(n_sel_blocks, sink_ref, qa_ref, qb_ref, gl_ref, gb_ref, kc_ref, vct_ref, selmap_ref,
                 ksd_ref, ks_ref, mask_ref, vsd_ref, vbig_ref, kw_ref, vw_ref, kbp_ref, kbc_ref, vbp_ref, vbc_ref,
                 eye_ref, cmask_ref, dmask_ref, wmask_ref, bmask_ref, oa_ref, ob_ref, m_sc, acc_sc, s_sc):
    n = pl.program_id(1)
    rep = NSA_REP
    n_groups = ksd_ref.shape[2] // HEAD_DIM
    gcols = rep * Q_BLOCK
    groups = range(n_groups)
    tq_row = n * Q_BLOCK + lax.broadcasted_iota(jnp.int32, (1, Q_BLOCK), 1)

    def pad_q(q_t):
        zero = jnp.zeros((HEAD_DIM, gcols), q_t.dtype)
        return jnp.concatenate([jnp.concatenate([q_t[:, i * gcols:(i + 1) * gcols] if i == j else zero
                                                 for i in groups], axis=1) for j in groups], axis=0)

    def per_group(fn):
        return jnp.concatenate([fn(i) for i in groups], axis=1)

    def apply_values(v_t, e):
        return per_group(lambda i: _dot(with_ones(v_t[i * HEAD_DIM:(i + 1) * HEAD_DIM]),
                                        e[:, i * gcols:(i + 1) * gcols]))

    def masked_logits(keys, mask, q_rows):
        return _dot(jnp.concatenate([keys, mask], axis=1), jnp.concatenate([q_rows, eye_ref[...]], axis=0))

    def with_ones(v_t):
        return jnp.concatenate([v_t, jnp.ones((SUBLANES, v_t.shape[1]), v_t.dtype)], axis=0)

    q_t = pad_q(_stack_heads_t(qa_ref, n_groups * rep))
    kc = kc_ref[0, 0]
    n_cmp = kc.shape[0]
    odd = lax.rem(n, 2)
    shift = Q_BLOCK // CMP_STRIDE
    start = pl.multiple_of(n_cmp - shift * (n + odd), 2 * SUBLANES)
    s_c = masked_logits(kc, cmask_ref[odd, pl.ds(start, n_cmp), :], q_t)
    q_b = pad_q(_stack_heads_t(qb_ref, n_groups * SWA_REP))
    k_b = jnp.concatenate([kbp_ref[0], kbc_ref[0]], axis=0)
    s_b = masked_logits(k_b, bmask_ref[jnp.minimum(n, bmask_ref.shape[0] - 1)], q_b)
    n_win_tiles = NSA_WINDOW // Q_BLOCK + 1
    tiles = [jnp.maximum(n - (n_win_tiles - 1) + i, 0) for i in range(n_win_tiles)]
    k_w = jnp.concatenate([kw_ref[0, pl.ds(pl.multiple_of(j * Q_BLOCK, Q_BLOCK), Q_BLOCK), :] for j in tiles], axis=0)
    s_w = masked_logits(k_w, wmask_ref[jnp.minimum(n, wmask_ref.shape[0] - 1)], q_t)
    s_d = masked_logits(ksd_ref[0], dmask_ref[...], q_t)

    m = jnp.max(s_c, axis=0, keepdims=True)
    e_c = jnp.exp2(s_c - m).astype(BF16)
    vct = vct_ref[0, 0]
    oc_imp = per_group(lambda i: _dot(
        jnp.concatenate([with_ones(vct[i * HEAD_DIM:(i + 1) * HEAD_DIM]), selmap_ref[...]], axis=0),
        e_c[:, i * gcols:(i + 1) * gcols]))
    den = oc_imp[HEAD_DIM:HEAD_DIM + 1]
    seen = m > 0.5 * NEG_INF
    oc_imp = jnp.where(seen, oc_imp * (1.0 / jnp.where(seen, den, 1.0)), 0.0)
    o_c = oc_imp[:HEAD_DIM]
    imp = oc_imp[HEAD_DIM + SUBLANES:]
    imp = per_group(lambda i: sum(imp[:, (i * rep + r) * Q_BLOCK:(i * rep + r + 1) * Q_BLOCK]
                                  for r in range(rep)))

    sink_row = jnp.concatenate([jnp.full((1, Q_BLOCK), sink_ref[h] * LOG2_E, F32)
                                for h in range(n_groups * SWA_REP)], axis=1)
    v_tiles_b = [vbp_ref[0, 0, 0], vbc_ref[0, 0, 0]]
    v_tiles_w = [vw_ref[0, 0, j] for j in tiles]
    v_tiles_d = [vsd_ref[0, 0, 0]]
    maxes, partial = {}, {"b": [], "w": [], "d": []}

    def col_max(name, s, extra=None):
        def run():
            m = jnp.max(s, axis=0, keepdims=True)
            maxes[name] = m if extra is None else jnp.maximum(m, extra)
        return run

    def pv_tile(name, s, v_tiles, i):
        def run():
            rows = slice(i * Q_BLOCK, (i + 1) * Q_BLOCK)
            partial[name].append(apply_values(v_tiles[i], jnp.exp2(s[rows] - maxes[name]).astype(BF16)))
        return run

    pieces = ([col_max("w", s_w), col_max("b", s_b, sink_row), col_max("d", s_d)]
              + [pv_tile("w", s_w, v_tiles_w, i) for i in range(len(v_tiles_w))]
              + [pv_tile("b", s_b, v_tiles_b, i) for i in range(len(v_tiles_b))]
              + [pv_tile("d", s_d, v_tiles_d, 0)])

    ns_pad = imp.shape[0]
    blk = lax.broadcasted_iota(jnp.int32, (ns_pad, n_groups * Q_BLOCK), 0)
    blk_f = blk.astype(F32)
    tq_row = _tile_lanes(tq_row, n_groups)
    cur = jnp.right_shift(tq_row, SEL_SHIFT)
    forced = (blk == 0) | (blk == cur) | (blk == cur - 1)
    valid = jnp.left_shift(blk, SEL_SHIFT) <= tq_row
    score = jnp.where(valid, imp, NEG_INF)
    score = jnp.where(forced | (blk >= n_sel_blocks), BELOW_NEG_INF, score)
    sel = forced
    for _ in range(min(SEL_TOPK, n_sel_blocks) - 3):
        if pieces:
            pieces.pop(0)()
        top = _col_tree(score, jnp.maximum, jnp.max)
        first_hit = _col_tree(jnp.where(score == top, blk_f, float(ns_pad)), jnp.minimum, jnp.min)
        pick = blk_f == first_hit
        sel = sel | pick
        score = jnp.where(pick, BELOW_NEG_INF, score)
    n_diag_blk = n * (Q_BLOCK // SEL_LEN)
    unselected = jnp.where(sel & (blk < n_diag_blk), 0.0, 1.0).astype(BF16)

    while pieces:
        pieces.pop(0)()

    def finish(name, sink=None):
        acc = sum(partial[name][1:], partial[name][0])
        den = acc[HEAD_DIM:HEAD_DIM + 1]
        if sink is not None:
            den = den + jnp.exp2(sink - maxes[name])
        return acc[:HEAD_DIM] * (1.0 / den)

    m_sc[...] = maxes["d"]
    acc_sc[...] = partial["d"][0]

    q_aug = jnp.concatenate([per_group(lambda i: _tile_lanes(unselected[:, i * Q_BLOCK:(i + 1) * Q_BLOCK], rep)),
                             q_t], axis=0)
    tk = s_sc.shape[1]
    n_tiles = ks_ref.shape[1] // tk

    def logits(kt):
        keys = pl.ds(pl.multiple_of(kt * tk, tk), tk)
        return _dot(jnp.concatenate([mask_ref[keys, :], ks_ref[0, keys, :]], axis=1), q_aug)

    def absorb(kt, s):
        m_old = m_sc[...]
        m_new = jnp.maximum(m_old, jnp.max(s, axis=0, keepdims=True))
        pv = apply_values(vbig_ref[0, kt], jnp.exp2(s - m_new).astype(BF16))
        acc_sc[...] = jnp.exp2(m_old - m_new) * acc_sc[...] + pv
        m_sc[...] = m_new

    s_sc[0] = logits(0)
    ob_ref[0] = _unstack_heads_t(finish("b", sink_row), n_groups * SWA_REP).astype(ob_ref.dtype)
    o_w = finish("w")

    def tile_pair(j, carry):
        s_sc[1] = logits(2 * j + 1)
        absorb(2 * j, s_sc[0])
        s_sc[0] = logits(jnp.minimum(2 * j + 2, n_tiles - 1))
        absorb(2 * j + 1, s_sc[1])
        return carry

    lax.fori_loop(0, lax.div(n * Q_BLOCK + 2 * tk - 1, 2 * tk), tile_pair, 0)
    acc = acc_sc[...]
    o_s = acc[:HEAD_DIM] * (1.0 / acc[HEAD_DIM:HEAD_DIM + 1])

    gates = _sigmoid(gl_ref[0, 0] + gb_ref[...])
    gate = lambda c: jnp.concatenate([gates[16 * i + 3 * r + c:16 * i + 3 * r + c + 1]
                                      for i in groups for r in range(rep)], axis=1)
    o = gate(0) * o_c + gate(1) * o_s + gate(2) * o_w
    oa_ref[0] = _unstack_heads_t(o, n_groups * rep).astype(oa_ref.dtype)


def _selection_map(t, nc_pad, ns_pad):
    nc = (t - CMP_LEN) // CMP_STRIDE + 1
    ns = t // SEL_LEN
    cs = np.arange(nc, dtype=np.int64)[:, None] * CMP_STRIDE
    ss = np.arange(ns, dtype=np.int64)[None, :] * SEL_LEN
    out = np.zeros((nc_pad, ns_pad), np.float32)
    out[:nc, :ns] = (cs < ss + SEL_LEN) & (cs + CMP_LEN > ss)
    return out


def _block_mask_cols(t, ns_pad):
    hit = np.arange(ns_pad, dtype=np.int64)[None, :] == (np.arange(t, dtype=np.int64) // SEL_LEN)[:, None]
    return np.where(hit, NEG_INF, 0.0).astype(np.float32)


def _band_masks(n_rows, window):
    n_front = n_rows // Q_BLOCK - 1
    r = np.arange(n_rows)[None, :, None]
    tok = np.arange(Q_BLOCK)[None, None, :]
    v = np.arange(n_front + 1)[:, None, None]
    dist = n_front * Q_BLOCK + tok - r
    ok = (dist >= 0) & (dist < window) & (r >= (n_front - v) * Q_BLOCK)
    return np.where(ok, 0.0, NEG_INF).astype(np.float32)


def _compressed_masks(n_cmp):
    shift = Q_BLOCK // CMP_STRIDE
    u = np.arange(2 * n_cmp)[None, :, None] + shift * np.arange(2)[:, None, None]
    tok = np.arange(Q_BLOCK)[None, None, :]
    ok = CMP_STRIDE * (u - n_cmp) + CMP_LEN - 1 <= tok
    return np.where(ok, 0.0, NEG_INF).astype(np.float32)


def _attention(q, k_all, v_big, v_small, gate_t, gate_bias, cmp_rows, cmp_cols, sel_map_t, mask_cols, sinks,
               n_sel_blocks):
    b, t = q.shape[:2]
    gd = k_all.shape[2] // 3
    g = gd // HEAD_DIM
    tk = v_big.shape[3]
    assert (t // tk) % 2 == 0, "selected-branch key tiles are consumed in pairs"
    width = g * NSA_REP * HEAD_DIM
    cols = g * NSA_REP * Q_BLOCK
    gate_cols = gate_t.shape[3] // Q_BLOCK
    prev = lambda n: jnp.maximum(n - 1, 0)
    cur = lambda n: n
    k_tile = lambda blk, at: pl.BlockSpec((1, Q_BLOCK, gd), lambda i, n: (i, at(n), blk))
    k_all_rows = lambda blk: pl.BlockSpec((1, t, gd), lambda i, n: (i, 0, blk))
    v_tile = lambda a, at: pl.BlockSpec((1, 1, 1, gd, Q_BLOCK), lambda i, n: (a, i, at(n), 0, 0))
    out_spec = pl.BlockSpec((1, Q_BLOCK, width), lambda i, n: (i, n, 0))
    out_shape = jax.ShapeDtypeStruct((b, t, width), BF16)
    tri = np.where(np.arange(Q_BLOCK)[:, None] <= np.arange(Q_BLOCK)[None, :], 0.0, NEG_INF)
    consts = [jnp.asarray(c, BF16) for c in (
        np.tile(np.eye(Q_BLOCK, dtype=np.float32), (1, g * NSA_REP)),
        _compressed_masks(cmp_rows.shape[2]), tri,
        _band_masks(NSA_WINDOW + Q_BLOCK, NSA_WINDOW), _band_masks(SWA_WINDOW + Q_BLOCK, SWA_WINDOW))]
    return pl.pallas_call(
        functools.partial(_attn_kernel, n_sel_blocks),
        grid=(b, t // Q_BLOCK),
        in_specs=[
            pl.BlockSpec(memory_space=pltpu.SMEM),
            pl.BlockSpec((1, Q_BLOCK, width), lambda i, n: (i, n, 0)),
            pl.BlockSpec((1, Q_BLOCK, width), lambda i, n: (i, n, 1)),
            pl.BlockSpec((1, 1, 16 * g, Q_BLOCK), lambda i, n: (i, n // gate_cols, 0, n % gate_cols)),
            _resident(gate_bias.shape),
            pl.BlockSpec((1, 1) + cmp_rows.shape[2:], lambda i, n: (0, i, 0, 0)),
            pl.BlockSpec((1, 1) + cmp_cols.shape[2:], lambda i, n: (1, i, 0, 0)),
            _resident(sel_map_t.shape),
            k_tile(0, cur), k_all_rows(0), _resident(mask_cols.shape),
            v_tile(0, cur), pl.BlockSpec((1,) + v_big.shape[1:], lambda i, n: (i, 0, 0, 0)),
            k_all_rows(1), pl.BlockSpec((1, 1) + v_small.shape[2:], lambda i, n: (1, i, 0, 0, 0)),
            k_tile(2, prev), k_tile(2, cur), v_tile(2, prev), v_tile(2, cur),
        ] + [_resident(c.shape) for c in consts],
        out_specs=[out_spec, out_spec],
        out_shape=[out_shape, out_shape],
        scratch_shapes=[
            pltpu.VMEM((1, cols), F32),
            pltpu.VMEM((HEAD_DIM + SUBLANES, cols), F32),
            pltpu.VMEM((2, tk, cols), F32),
        ],
        compiler_params=_cparams(2),
        name="nsa_swa_attention",
    )(sinks, q, q, gate_t, gate_bias, cmp_rows, cmp_cols, sel_map_t, k_all, k_all, mask_cols, v_small, v_big,
      k_all, v_small, k_all, k_all, v_small, v_small, *consts)


def _conv_silu_kernel(n_q_cols, x_ref, prev_ref, w_ref, b_ref, o_ref, buf):
    i = pl.program_id(1)
    rows = x_ref.shape[1]
    halo = prev_ref.shape[1]
    buf[:halo, :] = jnp.where(i > 0, prev_ref[0].astype(F32), 0.0)
    buf[halo:, :] = x_ref[0].astype(F32)
    y = b_ref[...]
    for j in range(CONV_WIDTH):
        start = halo - (CONV_WIDTH - 1) + j
        y = y + w_ref[j:j + 1, :] * buf[start:start + rows, :]
    y = _silu(y)
    col = lax.broadcasted_iota(jnp.int32, y.shape, 1)
    o_ref[0] = jnp.where(col >= n_q_cols, y * (MLSTM_QK_DIM ** -0.5), y).astype(o_ref.dtype)


def _conv_silu(x, w, bias):
    b, t, c = x.shape
    halo = 2 * SUBLANES
    ratio = ROW_TILE // halo
    return pl.pallas_call(
        functools.partial(_conv_silu_kernel, c // 2),
        grid=(b, t // ROW_TILE),
        in_specs=[
            pl.BlockSpec((1, ROW_TILE, c), lambda i, j: (i, j, 0)),
            pl.BlockSpec((1, halo, c), lambda i, j: (i, jnp.maximum(j * ratio - 1, 0), 0)),
            _resident(w.shape),
            _resident((1, c)),
        ],
        out_specs=pl.BlockSpec((1, ROW_TILE, c), lambda i, j: (i, j, 0)),
        out_shape=jax.ShapeDtypeStruct((b, t, c), BF16),
        scratch_shapes=[pltpu.VMEM((ROW_TILE + halo, c), F32)],
        compiler_params=_cparams(2),
        name="conv_silu",
    )(x, x, w, bias.reshape(1, c))


def _mlstm_kernel(bias_ref, q_ref, k_ref, v_ref, ig_ref, fg_ref, o_ref, state, m_state):
    c = pl.program_id(1)
    L = SCAN_CHUNK
    dqk, dv = MLSTM_QK_DIM, MLSTM_V_DIM

    @pl.when(c == 0)
    def _():
        state[...] = jnp.zeros(state.shape, F32)
        m_state[...] = jnp.zeros(m_state.shape, F32)

    t_idx = lax.broadcasted_iota(jnp.int32, (L, L), 0)
    s_idx = lax.broadcasted_iota(jnp.int32, (L, L), 1)
    causal = s_idx <= t_idx
    eye = s_idx == t_idx
    ones_col = (lax.broadcasted_iota(jnp.int32, (L, LANES), 1) == 0).astype(BF16)

    def to_col(row):
        return jnp.sum(jnp.where(eye, row, 0.0), axis=1, keepdims=True)

    heads = range(MLSTM_HEADS)
    for j in range(SCAN_ROWS // SCAN_CHUNK):
        rows = slice(j * L, (j + 1) * L)
        qc = [q_ref[0, rows, h * dqk:(h + 1) * dqk] for h in heads]
        kc = [k_ref[0, rows, h * dqk:(h + 1) * dqk] for h in heads]
        va = [jnp.concatenate([v_ref[0, rows, h * dv:(h + 1) * dv], ones_col], axis=1) for h in heads]
        qk = [lax.dot_general(qc[h], kc[h], (((1,), (1,)), ((), ())), preferred_element_type=F32) for h in heads]
        i_row, b_col, b_row = [], [], []
        for h in heads:
            i_row.append(ig_ref[0, h:h + 1, rows] + bias_ref[0, h])
            f_pre = fg_ref[0, h:h + 1, rows] + bias_ref[1, h]
            f_row = jnp.minimum(f_pre, 0.0) - jnp.log(1.0 + jnp.exp(-jnp.abs(f_pre)))
            f_col = to_col(f_row)
            b_col.append(jnp.sum(jnp.where(causal, f_row, 0.0), axis=1, keepdims=True))
            b_row.append(jnp.sum(jnp.where(t_idx <= s_idx, f_col, 0.0), axis=0, keepdims=True))
        m_old = [m_state[h:h + 1, 0:1] for h in heads]
        st = [state[h] for h in heads]
        mt, w_inter, sm = [], [], []
        for h in heads:
            a = b_col[h] + m_old[h]
            d = jnp.where(causal, b_col[h] - b_row[h] + i_row[h], -jnp.inf)
            mt.append(jnp.maximum(a, jnp.max(d, axis=1, keepdims=True)))
            w_inter.append(jnp.exp(a - mt[h]))
            sm.append((qk[h] * jnp.exp(d - mt[h])).astype(BF16))
        for h in heads:
            nd = w_inter[h] * _dot(qc[h], st[h].astype(BF16)) + _dot(sm[h], va[h])
            hc = nd[:, :dv] * (1.0 / jnp.maximum(jnp.abs(nd[:, dv:dv + 1]), jnp.exp(-mt[h])))
            o_ref[0, rows, h * dv:(h + 1) * dv] = hc.astype(o_ref.dtype)
        for h in heads:
            b_last = b_row[h][:, L - 1:L]
            wl = b_last - b_row[h] + i_row[h]
            m_new = jnp.maximum(b_last + m_old[h], jnp.max(wl, axis=1, keepdims=True))
            decay = jnp.exp(b_last + m_old[h] - m_new)
            w_col = to_col(jnp.exp(wl - m_new))
            kw = (kc[h].astype(F32) * w_col).astype(BF16)
            upd = lax.dot_general(kw, va[h], (((0,), (0,)), ((), ())), preferred_element_type=F32)
            state[h] = decay * st[h] + upd
            m_state[h:h + 1, 0:1] = m_new


def _mlstm(qk, v, ig, fg, gate_bias):
    b, t = qk.shape[:2]
    nh = MLSTM_HEADS
    rows = SCAN_ROWS
    gate_spec = pl.BlockSpec((1, nh, rows), lambda i, c: (i, 0, c))
    return pl.pallas_call(
        _mlstm_kernel,
        grid=(b, t // rows),
        in_specs=[
            pl.BlockSpec(memory_space=pltpu.SMEM),
            pl.BlockSpec((1, rows, nh * MLSTM_QK_DIM), lambda i, c: (i, c, 0)),
            pl.BlockSpec((1, rows, nh * MLSTM_QK_DIM), lambda i, c: (i, c, 1)),
            pl.BlockSpec((1, rows, nh * MLSTM_V_DIM), lambda i, c: (i, c, 0)),
            gate_spec, gate_spec,
        ],
        out_specs=pl.BlockSpec((1, rows, nh * MLSTM_V_DIM), lambda i, c: (i, c, 0)),
        out_shape=jax.ShapeDtypeStruct((b, t, nh * MLSTM_V_DIM), BF16),
        scratch_shapes=[
            pltpu.VMEM((nh, MLSTM_QK_DIM, MLSTM_V_DIM + LANES), F32),
            pltpu.VMEM((SUBLANES, LANES), F32),
        ],
        compiler_params=_cparams(2),
        name="mlstm_scan",
    )(gate_bias, qk, qk, v, ig, fg)


def _block_diag(w, n):
    zero = jnp.zeros_like(w)
    return jnp.concatenate([jnp.concatenate([w if i == j else zero for i in range(n)], axis=-1)
                            for j in range(n)], axis=-2)


def _nsa_swa_layer(h, norm_g, w_in, gate_bias, pos_k, pos_v, ck_w1, ck_w2, cv_w1, cv_w2, sinks, w_out):
    b, t, d = h.shape
    m = b * t
    ng = NSA_KV_HEADS
    gk = ng * HEAD_DIM
    nq_a, nq_b = NSA_HEADS * HEAD_DIM, SWA_HEADS * HEAD_DIM
    ngate = NSA_HEADS * 3
    o_kc, o_vc, o_ks, o_vs, o_kw, o_vw = (nq_a + i * gk for i in range(6))
    o_g = nq_a + 6 * gk
    o_qb = o_g + ngate
    o_kb, o_vb = o_qb + nq_b, o_qb + nq_b + gk
    cols = lambda o, n: w_in[:, o:o + n]
    q_scale = HEAD_DIM ** -0.5 * LOG2_E
    gate_pad = jnp.zeros((d, 16 - NSA_REP * 3), w_in.dtype)
    w_gate = jnp.concatenate([p for i in range(ng) for p in (cols(o_g + i * NSA_REP * 3, NSA_REP * 3), gate_pad)]
                             + [jnp.zeros((d, LANES - 16 * ng), w_in.dtype)], axis=1)
    w_all = jnp.concatenate([cols(0, nq_a) * q_scale, cols(o_qb, nq_b) * q_scale,
                             cols(o_ks, gk), cols(o_kw, gk), cols(o_kb, gk),
                             cols(o_vs, gk), cols(o_vw, gk), cols(o_vb, gk),
                             cols(o_kc, gk), cols(o_vc, gk), w_gate], axis=1).astype(BF16)
    q_all, k_all, cmp_in, v_big, v_small, gate_t = _proj0(
        h.reshape(m, d), norm_g, w_all, [nq_a + nq_b, 3 * gk, 3 * gk, 2 * gk + LANES], 2 * gk, 16 * ng)
    tk = ROW_TILE
    q_all = q_all.reshape(b, t, nq_a + nq_b)
    k_all = k_all.reshape(b, t, 3 * gk)
    v_big = v_big.reshape(b, t // tk, gk, tk)
    v_small = v_small.reshape(3, b, t // Q_BLOCK, gk, Q_BLOCK)
    gate_t = gate_t.reshape(b, t // ROW_TILE, 16 * ng, ROW_TILE)
    gb = jnp.pad(gate_bias.reshape(ng, NSA_REP * 3), ((0, 0), (0, 16 - NSA_REP * 3))).reshape(ng * 16, 1)

    tile2 = lambda p: jnp.tile(p[:, None, :], (1, 1, ng))
    pos = jnp.stack([tile2(pos_k), tile2(pos_v)])
    per_pos = lambda w: w.reshape(CMP_LEN, HEAD_DIM, HEAD_DIM)
    w1 = _block_diag(jnp.stack([per_pos(ck_w1), per_pos(cv_w1)]), ng).astype(BF16)
    w2 = _block_diag(jnp.stack([ck_w2, cv_w2]), ng).astype(BF16)
    cmp_rows, cmp_cols = _compress(cmp_in.reshape(b, t, 2 * gk), pos, w1, w2)

    n_sel = t // SEL_LEN
    ns_pad = -(-n_sel // LANES) * LANES
    sel_map_t = jnp.asarray(_selection_map(t, t // CMP_STRIDE, ns_pad).T, BF16)
    mask_cols = jnp.asarray(_block_mask_cols(t, ns_pad), BF16)
    o_a, o_b = _attention(q_all, k_all, v_big, v_small, gate_t, gb, cmp_rows, cmp_cols, sel_map_t, mask_cols,
                          sinks, n_sel)

    w_o = w_out.astype(BF16)
    return [o_a.reshape(m, nq_a), o_b.reshape(m, nq_b)], [w_o[:nq_a], w_o[nq_a:]], None


def _mlstm_layer(h, norm_g, w_in, conv_w, conv_b, igate_bias, fgate_bias, w_out):
    b, t, d = h.shape
    m = b * t
    nqk = 2 * MLSTM_HEADS * MLSTM_QK_DIM
    nv = MLSTM_HEADS * MLSTM_V_DIM
    w = jnp.concatenate([w_in, jnp.zeros((d, LANES - 2 * MLSTM_HEADS), w_in.dtype)], axis=1).astype(BF16)
    qk_raw, v, og, gates = _norm_proj(h.reshape(m, d), norm_g, w, [nqk, nv, nv, LANES], [BF16, BF16, BF16, F32])
    qk = _conv_silu(qk_raw.reshape(b, t, nqk), conv_w, conv_b)
    gates = gates.reshape(b, t, -1)
    ig = gates[..., :MLSTM_HEADS].transpose(0, 2, 1)
    fg = gates[..., MLSTM_HEADS:2 * MLSTM_HEADS].transpose(0, 2, 1)
    hh = _mlstm(qk, v.reshape(b, t, nv), ig, fg, jnp.stack([igate_bias, fgate_bias]))
    return [hh.reshape(m, nv)], [w_out.astype(BF16)], og


def kernel(x, norm_mix, norm_ffn, ffn_w_gate, ffn_w_up, ffn_w_down, ab_w_in, ab_gate_bias, nsa_pos_k, nsa_pos_v, nsa_cmp_k_w1, nsa_cmp_k_w2, nsa_cmp_v_w1, nsa_cmp_v_w2, swa_sinks, ab_w_out, c_w_in, c_conv_w, c_conv_b, c_igate_bias, c_fgate_bias, c_w_out, final_norm):
    depth = norm_mix.shape[0]
    b, t, d = x.shape
    h = x
    for layer in range(depth):
        j = layer // 2
        if layer % 2 == 0:
            mixed = _nsa_swa_layer(h, norm_mix[layer], ab_w_in[j], ab_gate_bias[j], nsa_pos_k[j], nsa_pos_v[j],
                               nsa_cmp_k_w1[j], nsa_cmp_k_w2[j], nsa_cmp_v_w1[j], nsa_cmp_v_w2[j],
                               swa_sinks[j], ab_w_out[j])
        else:
            mixed = _mlstm_layer(h, norm_mix[layer], c_w_in[j], c_conv_w[j], c_conv_b[j], c_igate_bias[j],
                             c_fgate_bias[j], c_w_out[j])
        last = layer == depth - 1
        h = _mix_ffn(h.reshape(b * t, d), *mixed, norm_ffn[layer], ffn_w_gate[layer].astype(BF16),
                     ffn_w_up[layer].astype(BF16), ffn_w_down[layer].astype(BF16),
                     final_norm if last else None).reshape(b, t, d)
    if depth == 0:
        raise ValueError("depth must be positive")
    return h
```

```python
import functools

import numpy as np
import jax
import jax.numpy as jnp
from jax import lax
from jax.experimental import pallas as pl
from jax.experimental.pallas import tpu as pltpu

HEAD_DIM = 64
NSA_HEADS = 8
NSA_KV_HEADS = 2
NSA_REP = NSA_HEADS // NSA_KV_HEADS
CMP_LEN = 32
CMP_STRIDE = 16
SEL_LEN = 64
SEL_TOPK = 16
NSA_WINDOW = 512
SWA_HEADS = 8
SWA_KV_HEADS = 2
SWA_REP = SWA_HEADS // SWA_KV_HEADS
SWA_WINDOW = 128
Q_BLOCK = 128
MLSTM_HEADS = 4
MLSTM_QK_DIM = 128
MLSTM_V_DIM = 256
MLSTM_CHUNK = 64
CONV_WIDTH = 4
RMS_EPS = 1e-6
NEG_INF = -1e30
FORCE_SCORE = 1e9
BELOW_NEG_INF = -3e38
LOG2_E = 1.4426950408889634
SEL_SHIFT = SEL_LEN.bit_length() - 1
assert 1 << SEL_SHIFT == SEL_LEN

LANES = 128
SUBLANES = 8
VMEM_LIMIT_BYTES = 56 * 1024 * 1024

ROW_TILE = 512
SCAN_CHUNK = 512
SCAN_ROWS = 512

F32 = jnp.float32
BF16 = jnp.bfloat16


def _cparams(n_axes):
    return pltpu.CompilerParams(
        dimension_semantics=("arbitrary",) * n_axes,
        vmem_limit_bytes=VMEM_LIMIT_BYTES,
    )


def _resident(shape):
    nd = len(shape)
    return pl.BlockSpec(shape, lambda *_: (0,) * nd)


def _dot(a, b):
    return jnp.dot(a, b, preferred_element_type=F32)


def _sigmoid(x):
    return 1.0 / (1.0 + jnp.exp(-x))


def _silu(x):
    return x * _sigmoid(x)


def _rmsnorm(x, g):
    y = x * lax.rsqrt(jnp.mean(x * x, axis=-1, keepdims=True) + RMS_EPS)
    return y * g


def _col_ranges(widths):
    ends = np.cumsum(widths).tolist()
    return list(zip([0] + ends[:-1], ends))


def _norm_proj_kernel(widths, x_ref, g_ref, w_ref, *o_refs):
    ranges = _col_ranges(widths)
    xn = _rmsnorm(x_ref[...], g_ref[...]).astype(BF16)
    for (lo, hi), o_ref in zip(ranges[:-2], o_refs[:-2]):
        o_ref[...] = _dot(xn, w_ref[:, lo:hi]).astype(o_ref.dtype)
    (lo, mid), (_, hi) = ranges[-2:]
    both = _dot(xn, w_ref[:, lo:hi])
    o_refs[-2][...] = both[:, :mid - lo].astype(o_refs[-2].dtype)
    o_refs[-1][0] = both[:, mid - lo:].T[:o_refs[-1].shape[1]]


def _norm_proj(x, g, w, widths, out_dtypes, t_rows):
    m, d = x.shape
    steps = m // ROW_TILE
    row = lambda n: pl.BlockSpec((ROW_TILE, n), lambda i: (i, 0))
    return pl.pallas_call(
        functools.partial(_norm_proj_kernel, tuple(widths)),
        grid=(steps,),
        in_specs=[row(d), _resident((1, d)), _resident(w.shape)],
        out_specs=[row(n) for n in widths[:-1]] + [pl.BlockSpec((1, t_rows, ROW_TILE), lambda i: (i, 0, 0))],
        out_shape=[jax.ShapeDtypeStruct((m, n), dt) for n, dt in zip(widths[:-1], out_dtypes)]
        + [jax.ShapeDtypeStruct((steps, t_rows, ROW_TILE), F32)],
        compiler_params=_cparams(1),
        name="norm_proj",
    )(x, g.reshape(1, d), w)


def _proj0_kernel(widths, x_ref, g_ref, w_ref, q_ref, k_ref, c_ref, vbig_ref, vsmall_ref, gt_ref):
    (q0, q1), (k0, k1), (v0, v1), (c0, c1) = _col_ranges(widths)
    xn = _rmsnorm(x_ref[...], g_ref[...]).astype(BF16)
    q = _dot(xn, w_ref[:, q0:q1])
    half = (q1 - q0) // 2
    q_ref[0, :half] = q[:, :half].T.astype(q_ref.dtype)
    q_ref[0, half:] = q[:, half:].T.astype(q_ref.dtype)
    k_ref[...] = _dot(xn, w_ref[:, k0:k1]).astype(k_ref.dtype)
    cg = _dot(xn, w_ref[:, c0:c1])
    c_ref[...] = cg[:, :c_ref.shape[1]]
    gt_ref[0] = cg[:, c_ref.shape[1]:].T[:gt_ref.shape[1]]
    v = _dot(xn, w_ref[:, v0:v1])
    width = v.shape[1] // 3
    for a in range(3):
        v_t = v[:, a * width:(a + 1) * width].T.astype(BF16)
        if a == 0:
            vbig_ref[0] = v_t
        for j in range(x_ref.shape[0] // Q_BLOCK):
            vsmall_ref[a, j] = v_t[:, j * Q_BLOCK:(j + 1) * Q_BLOCK]


def _proj0(x, g, w, widths, n_cmp_cols, gate_rows):
    m, d = x.shape
    steps = m // ROW_TILE
    tiles = ROW_TILE // Q_BLOCK
    nq, nk, nv, _ = widths
    gd = nv // 3
    row = lambda n: pl.BlockSpec((ROW_TILE, n), lambda i: (i, 0))
    return pl.pallas_call(
        functools.partial(_proj0_kernel, tuple(widths)),
        grid=(steps,),
        in_specs=[row(d), _resident((1, d)), _resident(w.shape)],
        out_specs=[
            pl.BlockSpec((1, nq, ROW_TILE), lambda i: (i, 0, 0)), row(nk), row(n_cmp_cols),
            pl.BlockSpec((1, gd, ROW_TILE), lambda i: (i, 0, 0)),
            pl.BlockSpec((3, tiles, gd, Q_BLOCK), lambda i: (0, i, 0, 0)),
            pl.BlockSpec((1, gate_rows, ROW_TILE), lambda i: (i, 0, 0)),
        ],
        out_shape=[
            jax.ShapeDtypeStruct((steps, nq, ROW_TILE), BF16),
            jax.ShapeDtypeStruct((m, nk), BF16),
            jax.ShapeDtypeStruct((m, n_cmp_cols), F32),
            jax.ShapeDtypeStruct((steps, gd, ROW_TILE), BF16),
            jax.ShapeDtypeStruct((3, steps * tiles, gd, Q_BLOCK), BF16),
            jax.ShapeDtypeStruct((steps, gate_rows, ROW_TILE), F32),
        ],
        compiler_params=_cparams(1),
        name="norm_proj_attn",
    )(x, g.reshape(1, d), w)


def _mix_ffn_kernel(n_in, gated, final, res_ref, *refs):
    a_refs, w_refs = refs[:n_in], refs[n_in:2 * n_in]
    rest = list(refs[2 * n_in:])
    o_ref = rest.pop()
    gate_ref = rest.pop(0) if gated else None
    g_ref, wg_ref, wu_ref, wd_ref = rest[:4]
    h = res_ref[...]
    for idx, (a_ref, w_ref) in enumerate(zip(a_refs, w_refs)):
        a = a_ref[...]
        if gated and idx == 0:
            a = (a.astype(F32) * _sigmoid(gate_ref[...].astype(F32))).astype(BF16)
        h = h + _dot(a, w_ref[...])
    xn = _rmsnorm(h, g_ref[...]).astype(BF16)
    act = (_silu(_dot(xn, wg_ref[...])) * _dot(xn, wu_ref[...])).astype(BF16)
    out = h + _dot(act, wd_ref[...])
    if final:
        out = _rmsnorm(out, rest[4][...])
    o_ref[...] = out


def _mix_ffn(res, a_list, w_list, gate, g, wg, wu, wd, final_g=None):
    m, d = res.shape
    row = lambda w: pl.BlockSpec((ROW_TILE, w), lambda i: (i, 0))
    once = lambda a: pl.BlockSpec(a.shape, lambda *_: (0,) * a.ndim, pipeline_mode=pl.Buffered(1))
    in_specs = [row(d)] + [row(a.shape[1]) for a in a_list] + [once(w) for w in w_list]
    args = [res, *a_list, *w_list]
    if gate is not None:
        in_specs.append(row(gate.shape[1]))
        args.append(gate)
    consts = [g.reshape(1, d), wg, wu, wd] + ([final_g.reshape(1, d)] if final_g is not None else [])
    return pl.pallas_call(
        functools.partial(_mix_ffn_kernel, len(a_list), gate is not None, final_g is not None),
        grid=(m // ROW_TILE,),
        in_specs=in_specs + [once(c) for c in consts],
        out_specs=row(d),
        out_shape=jax.ShapeDtypeStruct((m, d), F32),
        compiler_params=_cparams(1),
        name="mix_out_ffn",
    )(*args, *consts)


def _compress_kernel(x_ref, p_ref, w1_ref, w2_ref, rows_ref, cols_ref):
    n = x_ref.shape[1] // CMP_STRIDE
    a = jnp.zeros((n, x_ref.shape[2]), F32)
    b = jnp.zeros((n, x_ref.shape[2]), F32)
    for r in range(CMP_STRIDE):
        x = x_ref[0, pl.ds(r, n, stride=CMP_STRIDE), :]
        a = a + _dot((x + p_ref[0, r]).astype(BF16), w1_ref[0, r])
        b = b + _dot((x + p_ref[0, CMP_STRIDE + r]).astype(BF16), w1_ref[0, CMP_STRIDE + r])
    h = a + pltpu.roll(b, n - 1, 0)
    out = _dot(_silu(h).astype(BF16), w2_ref[0])
    live = lax.broadcasted_iota(jnp.int32, out.shape, 0) < n - 1
    out = jnp.where(live, out, 0.0)
    rows_ref[0, 0] = out.astype(rows_ref.dtype)
    cols_ref[0, 0] = out.T.astype(cols_ref.dtype)


def _compress(x, pos, w1, w2):
    b, t, two_gd = x.shape
    gd = two_gd // 2
    n = t // CMP_STRIDE
    return pl.pallas_call(
        _compress_kernel,
        grid=(2, b),
        in_specs=[
            pl.BlockSpec((1, t, gd), lambda s, i: (i, 0, s)),
            pl.BlockSpec((1,) + pos.shape[1:], lambda s, i: (s, 0, 0, 0)),
            pl.BlockSpec((1,) + w1.shape[1:], lambda s, i: (s, 0, 0, 0)),
            pl.BlockSpec((1,) + w2.shape[1:], lambda s, i: (s, 0, 0)),
        ],
        out_specs=[pl.BlockSpec((1, 1, n, gd), lambda s, i: (s, i, 0, 0)),
                   pl.BlockSpec((1, 1, gd, n), lambda s, i: (s, i, 0, 0))],
        out_shape=[jax.ShapeDtypeStruct((2, b, n, gd), BF16), jax.ShapeDtypeStruct((2, b, gd, n), BF16)],
        compiler_params=_cparams(2),
        name="compress_kv",
    )(x, pos, w1, w2)


def _stack_heads_t(q_ref, rep):
    qt = q_ref[0, 0]
    return jnp.concatenate([qt[r * HEAD_DIM:(r + 1) * HEAD_DIM] for r in range(rep)], axis=1)


def _unstack_heads_t(o_t, rep):
    tq = o_t.shape[1] // rep
    return jnp.concatenate([o_t[:, r * tq:(r + 1) * tq] for r in range(rep)], axis=0).T


def _tile_lanes(x, rep):
    return jnp.concatenate([x] * rep, axis=1)


def _col_tree(x, pair, reduce):
    while x.shape[0] > SUBLANES:
        half = x.shape[0] // 2
        x = pair(x[:half], x[half:])
    return reduce(x, axis=0, keepdims=True)


def _attn_kernel(n_sel_blocks, sink_ref, qa_ref, qb_ref, gl_ref, gb_ref, kc_ref, vct_ref, selmap_ref,
                 ksd_ref, ks_ref, mask_ref, vsd_ref, vbig_ref, kw_ref, vw_ref, kbp_ref, kbc_ref, vbp_ref, vbc_ref,
                 eye_ref, cmask_ref, dmask_ref, wmask_ref, bmask_ref, oa_ref, ob_ref, m_sc, acc_sc, s_sc):
    n = pl.program_id(1)
    rep = NSA_REP
    n_groups = ksd_ref.shape[2] // HEAD_DIM
    gcols = rep * Q_BLOCK
    groups = range(n_groups)
    tq_row = n * Q_BLOCK + lax.broadcasted_iota(jnp.int32, (1, Q_BLOCK), 1)

    def pad_q(q_t):
        zero = jnp.zeros((HEAD_DIM, gcols), q_t.dtype)
        return jnp.concatenate([jnp.concatenate([q_t[:, i * gcols:(i + 1) * gcols] if i == j else zero
                                                 for i in groups], axis=1) for j in groups], axis=0)

    def per_group(fn):
        return jnp.concatenate([fn(i) for i in groups], axis=1)

    def apply_values(v_t, e):
        return per_group(lambda i: _dot(with_ones(v_t[i * HEAD_DIM:(i + 1) * HEAD_DIM]),
                                        e[:, i * gcols:(i + 1) * gcols]))

    def masked_logits(keys, mask, q_rows):
        return _dot(jnp.concatenate([keys, mask], axis=1), jnp.concatenate([q_rows, eye_ref[...]], axis=0))

    def with_ones(v_t):
        return jnp.concatenate([v_t, jnp.ones((SUBLANES, v_t.shape[1]), v_t.dtype)], axis=0)

    q_t = pad_q(_stack_heads_t(qa_ref, n_groups * rep))
    kc = kc_ref[0, 0]
    n_cmp = kc.shape[0]
    odd = lax.rem(n, 2)
    shift = Q_BLOCK // CMP_STRIDE
    start = pl.multiple_of(n_cmp - shift * (n + odd), 2 * SUBLANES)
    s_c = masked_logits(kc, cmask_ref[odd, pl.ds(start, n_cmp), :], q_t)
    q_b = pad_q(_stack_heads_t(qb_ref, n_groups * SWA_REP))
    k_b = jnp.concatenate([kbp_ref[0], kbc_ref[0]], axis=0)
    s_b = masked_logits(k_b, bmask_ref[jnp.minimum(n, bmask_ref.shape[0] - 1)], q_b)
    n_win_tiles = NSA_WINDOW // Q_BLOCK + 1
    tiles = [jnp.maximum(n - (n_win_tiles - 1) + i, 0) for i in range(n_win_tiles)]
    k_w = jnp.concatenate([kw_ref[0, pl.ds(pl.multiple_of(j * Q_BLOCK, Q_BLOCK), Q_BLOCK), :] for j in tiles], axis=0)
    s_w = masked_logits(k_w, wmask_ref[jnp.minimum(n, wmask_ref.shape[0] - 1)], q_t)
    s_d = masked_logits(ksd_ref[0], dmask_ref[...], q_t)

    m = jnp.max(s_c, axis=0, keepdims=True)
    e_c = jnp.exp2(s_c - m).astype(BF16)
    vct = vct_ref[0, 0]
    oc_imp = per_group(lambda i: _dot(
        jnp.concatenate([with_ones(vct[i * HEAD_DIM:(i + 1) * HEAD_DIM]), selmap_ref[...]], axis=0),
        e_c[:, i * gcols:(i + 1) * gcols]))
    den = oc_imp[HEAD_DIM:HEAD_DIM + 1]
    seen = m > 0.5 * NEG_INF
    oc_imp = jnp.where(seen, oc_imp * (1.0 / jnp.where(seen, den, 1.0)), 0.0)
    o_c = oc_imp[:HEAD_DIM]
    imp = oc_imp[HEAD_DIM + SUBLANES:]
    imp = per_group(lambda i: sum(imp[:, (i * rep + r) * Q_BLOCK:(i * rep + r + 1) * Q_BLOCK]
                                  for r in range(rep)))

    sink_row = jnp.concatenate([jnp.full((1, Q_BLOCK), sink_ref[h] * LOG2_E, F32)
                                for h in range(n_groups * SWA_REP)], axis=1)
    v_tiles_b = [vbp_ref[0, 0, 0], vbc_ref[0, 0, 0]]
    v_tiles_w = [vw_ref[0, 0, j] for j in tiles]
    v_tiles_d = [vsd_ref[0, 0, 0]]
    maxes, partial = {}, {"b": [], "w": [], "d": []}

    def col_max(name, s, extra=None):
        def run():
            m = jnp.max(s, axis=0, keepdims=True)
            maxes[name] = m if extra is None else jnp.maximum(m, extra)
        return run

    def pv_tile(name, s, v_tiles, i):
        def run():
            rows = slice(i * Q_BLOCK, (i + 1) * Q_BLOCK)
            partial[name].append(apply_values(v_tiles[i], jnp.exp2(s[rows] - maxes[name]).astype(BF16)))
        return run

    pieces = ([col_max("w", s_w), col_max("b", s_b, sink_row), col_max("d", s_d)]
              + [pv_tile("w", s_w, v_tiles_w, i) for i in range(len(v_tiles_w))]
              + [pv_tile("b", s_b, v_tiles_b, i) for i in range(len(v_tiles_b))]
              + [pv_tile("d", s_d, v_tiles_d, 0)])

    ns_pad = imp.shape[0]
    blk = lax.broadcasted_iota(jnp.int32, (ns_pad, n_groups * Q_BLOCK), 0)
    blk_f = blk.astype(F32)
    tq_row = _tile_lanes(tq_row, n_groups)
    cur = jnp.right_shift(tq_row, SEL_SHIFT)
    forced = (blk == 0) | (blk == cur) | (blk == cur - 1)
    valid = jnp.left_shift(blk, SEL_SHIFT) <= tq_row
    score = jnp.where(valid, imp, NEG_INF)
    score = jnp.where(forced | (blk >= n_sel_blocks), BELOW_NEG_INF, score)
    sel = forced
    for _ in range(min(SEL_TOPK, n_sel_blocks) - 3):
        if pieces:
            pieces.pop(0)()
        top = _col_tree(score, jnp.maximum, jnp.max)
        first_hit = _col_tree(jnp.where(score == top, blk_f, float(ns_pad)), jnp.minimum, jnp.min)
        pick = blk_f == first_hit
        sel = sel | pick
        score = jnp.where(pick, BELOW_NEG_INF, score)
    n_diag_blk = n * (Q_BLOCK // SEL_LEN)
    unselected = jnp.where(sel & (blk < n_diag_blk), 0.0, 1.0).astype(BF16)

    while pieces:
        pieces.pop(0)()

    def finish(name, sink=None):
        acc = sum(partial[name][1:], partial[name][0])
        den = acc[HEAD_DIM:HEAD_DIM + 1]
        if sink is not None:
            den = den + jnp.exp2(sink - maxes[name])
        return acc[:HEAD_DIM] * (1.0 / den)

    m_sc[...] = maxes["d"]
    acc_sc[...] = partial["d"][0]

    q_aug = jnp.concatenate([per_group(lambda i: _tile_lanes(unselected[:, i * Q_BLOCK:(i + 1) * Q_BLOCK], rep)),
                             q_t], axis=0)
    tk = s_sc.shape[1]
    n_tiles = ks_ref.shape[1] // tk

    def logits(kt):
        keys = pl.ds(pl.multiple_of(kt * tk, tk), tk)
        return _dot(jnp.concatenate([mask_ref[keys, :], ks_ref[0, keys, :]], axis=1), q_aug)

    def absorb(kt, s):
        m_old = m_sc[...]
        m_new = jnp.maximum(m_old, jnp.max(s, axis=0, keepdims=True))
        pv = apply_values(vbig_ref[0, kt], jnp.exp2(s - m_new).astype(BF16))
        acc_sc[...] = jnp.exp2(m_old - m_new) * acc_sc[...] + pv
        m_sc[...] = m_new

    s_sc[0] = logits(0)
    ob_ref[0] = _unstack_heads_t(finish("b", sink_row), n_groups * SWA_REP).astype(ob_ref.dtype)
    o_w = finish("w")

    def tile_pair(j, carry):
        s_sc[1] = logits(2 * j + 1)
        absorb(2 * j, s_sc[0])
        s_sc[0] = logits(jnp.minimum(2 * j + 2, n_tiles - 1))
        absorb(2 * j + 1, s_sc[1])
        return carry

    lax.fori_loop(0, lax.div(n * Q_BLOCK + 2 * tk - 1, 2 * tk), tile_pair, 0)
    acc = acc_sc[...]
    o_s = acc[:HEAD_DIM] * (1.0 / acc[HEAD_DIM:HEAD_DIM + 1])

    gates = _sigmoid(gl_ref[0, 0] + gb_ref[...])
    gate = lambda c: jnp.concatenate([gates[16 * i + 3 * r + c:16 * i + 3 * r + c + 1]
                                      for i in groups for r in range(rep)], axis=1)
    o = gate(0) * o_c + gate(1) * o_s + gate(2) * o_w
    oa_ref[0] = _unstack_heads_t(o, n_groups * rep).astype(oa_ref.dtype)


def _selection_map(t, nc_pad, ns_pad):
    nc = (t - CMP_LEN) // CMP_STRIDE + 1
    ns = t // SEL_LEN
    cs = np.arange(nc, dtype=np.int64)[:, None] * CMP_STRIDE
    ss = np.arange(ns, dtype=np.int64)[None, :] * SEL_LEN
    out = np.zeros((nc_pad, ns_pad), np.float32)
    out[:nc, :ns] = (cs < ss + SEL_LEN) & (cs + CMP_LEN > ss)
    return out


def _block_mask_cols(t, ns_pad):
    hit = np.arange(ns_pad, dtype=np.int64)[None, :] == (np.arange(t, dtype=np.int64) // SEL_LEN)[:, None]
    return np.where(hit, NEG_INF, 0.0).astype(np.float32)


def _band_masks(n_rows, window):
    n_front = n_rows // Q_BLOCK - 1
    r = np.arange(n_rows)[None, :, None]
    tok = np.arange(Q_BLOCK)[None, None, :]
    v = np.arange(n_front + 1)[:, None, None]
    dist = n_front * Q_BLOCK + tok - r
    ok = (dist >= 0) & (dist < window) & (r >= (n_front - v) * Q_BLOCK)
    return np.where(ok, 0.0, NEG_INF).astype(np.float32)


def _compressed_masks(n_cmp):
    shift = Q_BLOCK // CMP_STRIDE
    u = np.arange(2 * n_cmp)[None, :, None] + shift * np.arange(2)[:, None, None]
    tok = np.arange(Q_BLOCK)[None, None, :]
    ok = CMP_STRIDE * (u - n_cmp) + CMP_LEN - 1 <= tok
    return np.where(ok, 0.0, NEG_INF).astype(np.float32)


def _attention(q, k_all, v_big, v_small, gate_t, gate_bias, cmp_rows, cmp_cols, sel_map_t, mask_cols, sinks,
               n_sel_blocks):
    b, t = k_all.shape[:2]
    gd = k_all.shape[2] // 3
    g = gd // HEAD_DIM
    tk = v_big.shape[3]
    assert (t // tk) % 2 == 0, "selected-branch key tiles are consumed in pairs"
    width = g * NSA_REP * HEAD_DIM
    cols = g * NSA_REP * Q_BLOCK
    gate_cols = gate_t.shape[3] // Q_BLOCK
    prev = lambda n: jnp.maximum(n - 1, 0)
    cur = lambda n: n
    k_tile = lambda blk, at: pl.BlockSpec((1, Q_BLOCK, gd), lambda i, n: (i, at(n), blk))
    k_all_rows = lambda blk: pl.BlockSpec((1, t, gd), lambda i, n: (i, 0, blk))
    v_tile = lambda a, at: pl.BlockSpec((1, 1, 1, gd, Q_BLOCK), lambda i, n: (a, i, at(n), 0, 0))
    out_spec = pl.BlockSpec((1, Q_BLOCK, width), lambda i, n: (i, n, 0))
    out_shape = jax.ShapeDtypeStruct((b, t, width), BF16)
    tri = np.where(np.arange(Q_BLOCK)[:, None] <= np.arange(Q_BLOCK)[None, :], 0.0, NEG_INF)
    consts = [jnp.asarray(c, BF16) for c in (
        np.tile(np.eye(Q_BLOCK, dtype=np.float32), (1, g * NSA_REP)),
        _compressed_masks(cmp_rows.shape[2]), tri,
        _band_masks(NSA_WINDOW + Q_BLOCK, NSA_WINDOW), _band_masks(SWA_WINDOW + Q_BLOCK, SWA_WINDOW))]
    return pl.pallas_call(
        functools.partial(_attn_kernel, n_sel_blocks),
        grid=(b, t // Q_BLOCK),
        in_specs=[
            pl.BlockSpec(memory_space=pltpu.SMEM),
            pl.BlockSpec((1, 1, width, Q_BLOCK), lambda i, n: (i, n // gate_cols, 0, n % gate_cols)),
            pl.BlockSpec((1, 1, width, Q_BLOCK), lambda i, n: (i, n // gate_cols, 1, n % gate_cols)),
            pl.BlockSpec((1, 1, 16 * g, Q_BLOCK), lambda i, n: (i, n // gate_cols, 0, n % gate_cols)),
            _resident(gate_bias.shape),
            pl.BlockSpec((1, 1) + cmp_rows.shape[2:], lambda i, n: (0, i, 0, 0)),
            pl.BlockSpec((1, 1) + cmp_cols.shape[2:], lambda i, n: (1, i, 0, 0)),
            _resident(sel_map_t.shape),
            k_tile(0, cur), k_all_rows(0), _resident(mask_cols.shape),
            v_tile(0, cur), pl.BlockSpec((1,) + v_big.shape[1:], lambda i, n: (i, 0, 0, 0)),
            k_all_rows(1), pl.BlockSpec((1, 1) + v_small.shape[2:], lambda i, n: (1, i, 0, 0, 0)),
            k_tile(2, prev), k_tile(2, cur), v_tile(2, prev), v_tile(2, cur),
        ] + [_resident(c.shape) for c in consts],
        out_specs=[out_spec, out_spec],
        out_shape=[out_shape, out_shape],
        scratch_shapes=[
            pltpu.VMEM((1, cols), F32),
            pltpu.VMEM((HEAD_DIM + SUBLANES, cols), F32),
            pltpu.VMEM((2, tk, cols), F32),
        ],
        compiler_params=_cparams(2),
        name="nsa_swa_attention",
    )(sinks, q, q, gate_t, gate_bias, cmp_rows, cmp_cols, sel_map_t, k_all, k_all, mask_cols, v_small, v_big,
      k_all, v_small, k_all, k_all, v_small, v_small, *consts)


def _conv_silu_kernel(n_q_cols, x_ref, prev_ref, shift_ref, w_ref, b_ref, o_ref, buf):
    i = pl.program_id(1)
    rows = x_ref.shape[1]
    halo = prev_ref.shape[1]
    taps = CONV_WIDTH - 1
    buf[:halo, :] = jnp.where(i > 0, prev_ref[0], jnp.zeros_like(prev_ref[0]))
    buf[halo:, :] = x_ref[0]
    col = lax.broadcasted_iota(jnp.int32, (Q_BLOCK, x_ref.shape[2]), 1)
    for r in range(rows // Q_BLOCK):
        win = buf[r * Q_BLOCK:(r + 1) * Q_BLOCK + halo, :]
        back = _dot(shift_ref[...], win)
        y = b_ref[...] + w_ref[taps:taps + 1, :] * win[halo:].astype(F32)
        for j in range(taps):
            y = y + w_ref[j:j + 1, :] * back[j * Q_BLOCK:(j + 1) * Q_BLOCK]
        y = _silu(y)
        y = jnp.where(col >= n_q_cols, y * (MLSTM_QK_DIM ** -0.5), y)
        o_ref[0, r * Q_BLOCK:(r + 1) * Q_BLOCK, :] = y.astype(o_ref.dtype)


def _conv_silu(x, w, bias):
    b, t, c = x.shape
    halo = 2 * SUBLANES
    ratio = ROW_TILE // halo
    taps = CONV_WIDTH - 1
    shift = np.zeros((taps * Q_BLOCK, Q_BLOCK + halo), np.float32)
    for j in range(taps):
        shift[j * Q_BLOCK + np.arange(Q_BLOCK), np.arange(Q_BLOCK) + halo - taps + j] = 1.0
    return pl.pallas_call(
        functools.partial(_conv_silu_kernel, c // 2),
        grid=(b, t // ROW_TILE),
        in_specs=[
            pl.BlockSpec((1, ROW_TILE, c), lambda i, j: (i, j, 0)),
            pl.BlockSpec((1, halo, c), lambda i, j: (i, jnp.maximum(j * ratio - 1, 0), 0)),
            _resident(shift.shape),
            _resident(w.shape),
            _resident((1, c)),
        ],
        out_specs=pl.BlockSpec((1, ROW_TILE, c), lambda i, j: (i, j, 0)),
        out_shape=jax.ShapeDtypeStruct((b, t, c), BF16),
        scratch_shapes=[pltpu.VMEM((ROW_TILE + halo, c), BF16)],
        compiler_params=_cparams(2),
        name="conv_silu",
    )(x, x, jnp.asarray(shift, BF16), w, bias.reshape(1, c))


def _mlstm_kernel(bias_ref, q_ref, k_ref, v_ref, g_ref, o_ref, state, m_state):
    c = pl.program_id(1)
    L = SCAN_CHUNK
    dqk, dv = MLSTM_QK_DIM, MLSTM_V_DIM

    @pl.when(c == 0)
    def _():
        state[...] = jnp.zeros(state.shape, F32)
        m_state[...] = jnp.zeros(m_state.shape, F32)

    t_idx = lax.broadcasted_iota(jnp.int32, (L, L), 0)
    s_idx = lax.broadcasted_iota(jnp.int32, (L, L), 1)
    causal = s_idx <= t_idx
    eye = s_idx == t_idx
    ones_col = (lax.broadcasted_iota(jnp.int32, (L, LANES), 1) == 0).astype(BF16)

    def to_col(row):
        return jnp.sum(jnp.where(eye, row, 0.0), axis=1, keepdims=True)

    heads = range(MLSTM_HEADS)
    for j in range(SCAN_ROWS // SCAN_CHUNK):
        rows = slice(j * L, (j + 1) * L)
        qc = [q_ref[0, rows, h * dqk:(h + 1) * dqk] for h in heads]
        kc = [k_ref[0, rows, h * dqk:(h + 1) * dqk] for h in heads]
        va = [jnp.concatenate([v_ref[0, rows, h * dv:(h + 1) * dv], ones_col], axis=1) for h in heads]
        qk = [lax.dot_general(qc[h], kc[h], (((1,), (1,)), ((), ())), preferred_element_type=F32) for h in heads]
        i_row, b_col, b_row = [], [], []
        for h in heads:
            i_row.append(g_ref[0, 0, h:h + 1, rows] + bias_ref[0, h])
            f_pre = g_ref[0, 0, MLSTM_HEADS + h:MLSTM_HEADS + h + 1, rows] + bias_ref[1, h]
            f_row = jnp.minimum(f_pre, 0.0) - jnp.log(1.0 + jnp.exp(-jnp.abs(f_pre)))
            f_col = to_col(f_row)
            b_col.append(jnp.sum(jnp.where(causal, f_row, 0.0), axis=1, keepdims=True))
            b_row.append(jnp.sum(jnp.where(t_idx <= s_idx, f_col, 0.0), axis=0, keepdims=True))
        m_old = [m_state[h:h + 1, 0:1] for h in heads]
        st = [state[h] for h in heads]
        mt, w_inter, sm = [], [], []
        for h in heads:
            a = b_col[h] + m_old[h]
            d = jnp.where(causal, b_col[h] - b_row[h] + i_row[h], -jnp.inf)
            mt.append(jnp.maximum(a, jnp.max(d, axis=1, keepdims=True)))
            w_inter.append(jnp.exp(a - mt[h]))
            sm.append((qk[h] * jnp.exp(d - mt[h])).astype(BF16))
        for h in heads:
            nd = w_inter[h] * _dot(qc[h], st[h].astype(BF16)) + _dot(sm[h], va[h])
            hc = nd[:, :dv] * (1.0 / jnp.maximum(jnp.abs(nd[:, dv:dv + 1]), jnp.exp(-mt[h])))
            o_ref[0, rows, h * dv:(h + 1) * dv] = hc.astype(o_ref.dtype)
        for h in heads:
            b_last = b_row[h][:, L - 1:L]
            wl = b_last - b_row[h] + i_row[h]
            m_new = jnp.maximum(b_last + m_old[h], jnp.max(wl, axis=1, keepdims=True))
            decay = jnp.exp(b_last + m_old[h] - m_new)
            w_col = to_col(jnp.exp(wl - m_new))
            kw = (kc[h].astype(F32) * w_col).astype(BF16)
            upd = lax.dot_general(kw, va[h], (((0,), (0,)), ((), ())), preferred_element_type=F32)
            state[h] = decay * st[h] + upd
            m_state[h:h + 1, 0:1] = m_new


def _mlstm(qk, v, gates_t, gate_bias):
    b, t = qk.shape[:2]
    nh = MLSTM_HEADS
    rows = SCAN_ROWS
    assert gates_t.shape[3] == rows
    return pl.pallas_call(
        _mlstm_kernel,
        grid=(b, t // rows),
        in_specs=[
            pl.BlockSpec(memory_space=pltpu.SMEM),
            pl.BlockSpec((1, rows, nh * MLSTM_QK_DIM), lambda i, c: (i, c, 0)),
            pl.BlockSpec((1, rows, nh * MLSTM_QK_DIM), lambda i, c: (i, c, 1)),
            pl.BlockSpec((1, rows, nh * MLSTM_V_DIM), lambda i, c: (i, c, 0)),
            pl.BlockSpec((1, 1) + gates_t.shape[2:], lambda i, c: (i, c, 0, 0)),
        ],
        out_specs=pl.BlockSpec((1, rows, nh * MLSTM_V_DIM), lambda i, c: (i, c, 0)),
        out_shape=jax.ShapeDtypeStruct((b, t, nh * MLSTM_V_DIM), BF16),
        scratch_shapes=[
            pltpu.VMEM((nh, MLSTM_QK_DIM, MLSTM_V_DIM + LANES), F32),
            pltpu.VMEM((SUBLANES, LANES), F32),
        ],
        compiler_params=_cparams(2),
        name="mlstm_scan",
    )(gate_bias, qk, qk, v, gates_t)


def _block_diag(w, n):
    zero = jnp.zeros_like(w)
    return jnp.concatenate([jnp.concatenate([w if i == j else zero for i in range(n)], axis=-1)
                            for j in range(n)], axis=-2)


def _nsa_swa_layer(h, norm_g, w_in, gate_bias, pos_k, pos_v, ck_w1, ck_w2, cv_w1, cv_w2, sinks, w_out):
    b, t, d = h.shape
    m = b * t
    ng = NSA_KV_HEADS
    gk = ng * HEAD_DIM
    nq_a, nq_b = NSA_HEADS * HEAD_DIM, SWA_HEADS * HEAD_DIM
    ngate = NSA_HEADS * 3
    o_kc, o_vc, o_ks, o_vs, o_kw, o_vw = (nq_a + i * gk for i in range(6))
    o_g = nq_a + 6 * gk
    o_qb = o_g + ngate
    o_kb, o_vb = o_qb + nq_b, o_qb + nq_b + gk
    cols = lambda o, n: w_in[:, o:o + n]
    q_scale = HEAD_DIM ** -0.5 * LOG2_E
    gate_pad = jnp.zeros((d, 16 - NSA_REP * 3), w_in.dtype)
    w_gate = jnp.concatenate([p for i in range(ng) for p in (cols(o_g + i * NSA_REP * 3, NSA_REP * 3), gate_pad)]
                             + [jnp.zeros((d, LANES - 16 * ng), w_in.dtype)], axis=1)
    w_all = jnp.concatenate([cols(0, nq_a) * q_scale, cols(o_qb, nq_b) * q_scale,
                             cols(o_ks, gk), cols(o_kw, gk), cols(o_kb, gk),
                             cols(o_vs, gk), cols(o_vw, gk), cols(o_vb, gk),
                             cols(o_kc, gk), cols(o_vc, gk), w_gate], axis=1).astype(BF16)
    q_all, k_all, cmp_in, v_big, v_small, gate_t = _proj0(
        h.reshape(m, d), norm_g, w_all, [nq_a + nq_b, 3 * gk, 3 * gk, 2 * gk + LANES], 2 * gk, 16 * ng)
    tk = ROW_TILE
    q_all = q_all.reshape(b, t // ROW_TILE, nq_a + nq_b, ROW_TILE)
    k_all = k_all.reshape(b, t, 3 * gk)
    v_big = v_big.reshape(b, t // tk, gk, tk)
    v_small = v_small.reshape(3, b, t // Q_BLOCK, gk, Q_BLOCK)
    gate_t = gate_t.reshape(b, t // ROW_TILE, 16 * ng, ROW_TILE)
    gb = jnp.pad(gate_bias.reshape(ng, NSA_REP * 3), ((0, 0), (0, 16 - NSA_REP * 3))).reshape(ng * 16, 1)

    tile2 = lambda p: jnp.tile(p[:, None, :], (1, 1, ng))
    pos = jnp.stack([tile2(pos_k), tile2(pos_v)])
    per_pos = lambda w: w.reshape(CMP_LEN, HEAD_DIM, HEAD_DIM)
    w1 = _block_diag(jnp.stack([per_pos(ck_w1), per_pos(cv_w1)]), ng).astype(BF16)
    w2 = _block_diag(jnp.stack([ck_w2, cv_w2]), ng).astype(BF16)
    cmp_rows, cmp_cols = _compress(cmp_in.reshape(b, t, 2 * gk), pos, w1, w2)

    n_sel = t // SEL_LEN
    ns_pad = -(-n_sel // LANES) * LANES
    sel_map_t = jnp.asarray(_selection_map(t, t // CMP_STRIDE, ns_pad).T, BF16)
    mask_cols = jnp.asarray(_block_mask_cols(t, ns_pad), BF16)
    o_a, o_b = _attention(q_all, k_all, v_big, v_small, gate_t, gb, cmp_rows, cmp_cols, sel_map_t, mask_cols,
                          sinks, n_sel)

    w_o = w_out.astype(BF16)
    return [o_a.reshape(m, nq_a), o_b.reshape(m, nq_b)], [w_o[:nq_a], w_o[nq_a:]], None


def _mlstm_layer(h, norm_g, w_in, conv_w, conv_b, igate_bias, fgate_bias, w_out):
    b, t, d = h.shape
    m = b * t
    nqk = 2 * MLSTM_HEADS * MLSTM_QK_DIM
    nv = MLSTM_HEADS * MLSTM_V_DIM
    w = jnp.concatenate([w_in, jnp.zeros((d, LANES - 2 * MLSTM_HEADS), w_in.dtype)], axis=1).astype(BF16)
    qk_raw, v, og, gates_t = _norm_proj(h.reshape(m, d), norm_g, w, [nqk, nv, nv, LANES], [BF16, BF16, BF16],
                                        2 * MLSTM_HEADS)
    qk = _conv_silu(qk_raw.reshape(b, t, nqk), conv_w, conv_b)
    gates_t = gates_t.reshape(b, t // ROW_TILE, 2 * MLSTM_HEADS, ROW_TILE)
    hh = _mlstm(qk, v.reshape(b, t, nv), gates_t, jnp.stack([igate_bias, fgate_bias]))
    return [hh.reshape(m, nv)], [w_out.astype(BF16)], og


def kernel(x, norm_mix, norm_ffn, ffn_w_gate, ffn_w_up, ffn_w_down, ab_w_in, ab_gate_bias, nsa_pos_k, nsa_pos_v, nsa_cmp_k_w1, nsa_cmp_k_w2, nsa_cmp_v_w1, nsa_cmp_v_w2, swa_sinks, ab_w_out, c_w_in, c_conv_w, c_conv_b, c_igate_bias, c_fgate_bias, c_w_out, final_norm):
    depth = norm_mix.shape[0]
    b, t, d = x.shape
    h = x
    for layer in range(depth):
        j = layer // 2
        if layer % 2 == 0:
            mixed = _nsa_swa_layer(h, norm_mix[layer], ab_w_in[j], ab_gate_bias[j], nsa_pos_k[j], nsa_pos_v[j],
                               nsa_cmp_k_w1[j], nsa_cmp_k_w2[j], nsa_cmp_v_w1[j], nsa_cmp_v_w2[j],
                               swa_sinks[j], ab_w_out[j])
        else:
            mixed = _mlstm_layer(h, norm_mix[layer], c_w_in[j], c_conv_w[j], c_conv_b[j], c_igate_bias[j],
                             c_fgate_bias[j], c_w_out[j])
        last = layer == depth - 1
        h = _mix_ffn(h.reshape(b * t, d), *mixed, norm_ffn[layer], ffn_w_gate[layer].astype(BF16),
                     ffn_w_up[layer].astype(BF16), ffn_w_down[layer].astype(BF16),
                     final_norm if last else None).reshape(b, t, d)
    if depth == 0:
        raise ValueError("depth must be positive")
    return h
```

```python
import functools

import numpy as np
import jax
import jax.numpy as jnp
from jax import lax
from jax.experimental import pallas as pl
from jax.experimental.pallas import tpu as pltpu

HEAD_DIM = 64
NSA_HEADS = 8
NSA_KV_HEADS = 2
NSA_REP = NSA_HEADS // NSA_KV_HEADS
CMP_LEN = 32
CMP_STRIDE = 16
SEL_LEN = 64
SEL_TOPK = 16
NSA_WINDOW = 512
SWA_HEADS = 8
SWA_KV_HEADS = 2
SWA_REP = SWA_HEADS // SWA_KV_HEADS
SWA_WINDOW = 128
Q_BLOCK = 128
MLSTM_HEADS = 4
MLSTM_QK_DIM = 128
MLSTM_V_DIM = 256
MLSTM_CHUNK = 64
CONV_WIDTH = 4
RMS_EPS = 1e-6
NEG_INF = -1e30
FORCE_SCORE = 1e9
BELOW_NEG_INF = -3e38
LOG2_E = 1.4426950408889634
SEL_SHIFT = SEL_LEN.bit_length() - 1
assert 1 << SEL_SHIFT == SEL_LEN

LANES = 128
SUBLANES = 8
VMEM_LIMIT_BYTES = 56 * 1024 * 1024

ROW_TILE = 512
SCAN_CHUNK = 512
SCAN_ROWS = 512

F32 = jnp.float32
BF16 = jnp.bfloat16


def _cparams(n_axes):
    return pltpu.CompilerParams(
        dimension_semantics=("arbitrary",) * n_axes,
        vmem_limit_bytes=VMEM_LIMIT_BYTES,
    )


def _resident(shape):
    nd = len(shape)
    return pl.BlockSpec(shape, lambda *_: (0,) * nd)


def _dot(a, b):
    return jnp.dot(a, b, preferred_element_type=F32)


def _sigmoid(x):
    return 1.0 / (1.0 + jnp.exp(-x))


def _silu(x):
    return x * _sigmoid(x)


def _rmsnorm(x, g):
    y = x * lax.rsqrt(jnp.mean(x * x, axis=-1, keepdims=True) + RMS_EPS)
    return y * g


def _col_ranges(widths):
    ends = np.cumsum(widths).tolist()
    return list(zip([0] + ends[:-1], ends))


def _norm_proj_kernel(widths, x_ref, g_ref, w_ref, *o_refs):
    ranges = _col_ranges(widths)
    xn = _rmsnorm(x_ref[...], g_ref[...]).astype(BF16)
    for (lo, hi), o_ref in zip(ranges[:-2], o_refs[:-2]):
        o_ref[...] = _dot(xn, w_ref[:, lo:hi]).astype(o_ref.dtype)
    (lo, mid), (_, hi) = ranges[-2:]
    both = _dot(xn, w_ref[:, lo:hi])
    o_refs[-2][...] = both[:, :mid - lo].astype(o_refs[-2].dtype)
    o_refs[-1][0] = both[:, mid - lo:].T[:o_refs[-1].shape[1]]


def _norm_proj(x, g, w, widths, out_dtypes, t_rows):
    m, d = x.shape
    steps = m // ROW_TILE
    row = lambda n: pl.BlockSpec((ROW_TILE, n), lambda i: (i, 0))
    return pl.pallas_call(
        functools.partial(_norm_proj_kernel, tuple(widths)),
        grid=(steps,),
        in_specs=[row(d), _resident((1, d)), _resident(w.shape)],
        out_specs=[row(n) for n in widths[:-1]] + [pl.BlockSpec((1, t_rows, ROW_TILE), lambda i: (i, 0, 0))],
        out_shape=[jax.ShapeDtypeStruct((m, n), dt) for n, dt in zip(widths[:-1], out_dtypes)]
        + [jax.ShapeDtypeStruct((steps, t_rows, ROW_TILE), F32)],
        compiler_params=_cparams(1),
        name="norm_proj",
    )(x, g.reshape(1, d), w)


def _proj0_kernel(widths, x_ref, g_ref, w_ref, q_ref, k_ref, c_ref, vbig_ref, vsmall_ref, gt_ref):
    (q0, q1), (k0, k1), (v0, v1), (c0, c1) = _col_ranges(widths)
    xn = _rmsnorm(x_ref[...], g_ref[...]).astype(BF16)
    q = _dot(xn, w_ref[:, q0:q1])
    half = (q1 - q0) // 2
    q_ref[0, :half] = q[:, :half].T.astype(q_ref.dtype)
    q_ref[0, half:] = q[:, half:].T.astype(q_ref.dtype)
    k_ref[...] = _dot(xn, w_ref[:, k0:k1]).astype(k_ref.dtype)
    cg = _dot(xn, w_ref[:, c0:c1])
    c_ref[...] = cg[:, :c_ref.shape[1]]
    gt_ref[0] = cg[:, c_ref.shape[1]:].T[:gt_ref.shape[1]]
    v = _dot(xn, w_ref[:, v0:v1])
    width = v.shape[1] // 3
    for a in range(3):
        v_t = v[:, a * width:(a + 1) * width].T.astype(BF16)
        if a == 0:
            vbig_ref[0] = v_t
        for j in range(x_ref.shape[0] // Q_BLOCK):
            vsmall_ref[a, j] = v_t[:, j * Q_BLOCK:(j + 1) * Q_BLOCK]


def _proj0(x, g, w, widths, n_cmp_cols, gate_rows):
    m, d = x.shape
    steps = m // ROW_TILE
    tiles = ROW_TILE // Q_BLOCK
    nq, nk, nv, _ = widths
    gd = nv // 3
    row = lambda n: pl.BlockSpec((ROW_TILE, n), lambda i: (i, 0))
    return pl.pallas_call(
        functools.partial(_proj0_kernel, tuple(widths)),
        grid=(steps,),
        in_specs=[row(d), _resident((1, d)), _resident(w.shape)],
        out_specs=[
            pl.BlockSpec((1, nq, ROW_TILE), lambda i: (i, 0, 0)), row(nk), row(n_cmp_cols),
            pl.BlockSpec((1, gd, ROW_TILE), lambda i: (i, 0, 0)),
            pl.BlockSpec((3, tiles, gd, Q_BLOCK), lambda i: (0, i, 0, 0)),
            pl.BlockSpec((1, gate_rows, ROW_TILE), lambda i: (i, 0, 0)),
        ],
        out_shape=[
            jax.ShapeDtypeStruct((steps, nq, ROW_TILE), BF16),
            jax.ShapeDtypeStruct((m, nk), BF16),
            jax.ShapeDtypeStruct((m, n_cmp_cols), F32),
            jax.ShapeDtypeStruct((steps, gd, ROW_TILE), BF16),
            jax.ShapeDtypeStruct((3, steps * tiles, gd, Q_BLOCK), BF16),
            jax.ShapeDtypeStruct((steps, gate_rows, ROW_TILE), F32),
        ],
        compiler_params=_cparams(1),
        name="norm_proj_attn",
    )(x, g.reshape(1, d), w)


def _mix_ffn_kernel(n_in, gated, final, res_ref, *refs):
    a_refs, w_refs = refs[:n_in], refs[n_in:2 * n_in]
    rest = list(refs[2 * n_in:])
    o_ref = rest.pop()
    gate_ref = rest.pop(0) if gated else None
    g_ref, wg_ref, wu_ref, wd_ref = rest[:4]
    h = res_ref[...]
    for idx, (a_ref, w_ref) in enumerate(zip(a_refs, w_refs)):
        a = a_ref[...]
        if gated and idx == 0:
            a = (a.astype(F32) * _sigmoid(gate_ref[...].astype(F32))).astype(BF16)
        h = h + _dot(a, w_ref[...])
    xn = _rmsnorm(h, g_ref[...]).astype(BF16)
    act = (_silu(_dot(xn, wg_ref[0])) * _dot(xn, wu_ref[0])).astype(BF16)
    out = h + _dot(act, wd_ref[0])
    if final:
        out = _rmsnorm(out, rest[4][...])
    o_ref[...] = out


def _mix_ffn(res, a_list, w_list, gate, g, layer, wg, wu, wd, final_g=None):
    m, d = res.shape
    row = lambda w: pl.BlockSpec((ROW_TILE, w), lambda i: (i, 0))
    once = lambda a: pl.BlockSpec(a.shape, lambda *_: (0,) * a.ndim, pipeline_mode=pl.Buffered(1))
    in_specs = [row(d)] + [row(a.shape[1]) for a in a_list] + [once(w) for w in w_list]
    args = [res, *a_list, *w_list]
    if gate is not None:
        in_specs.append(row(gate.shape[1]))
        args.append(gate)
    slab = lambda a: pl.BlockSpec((1,) + a.shape[1:], lambda *_: (layer, 0, 0), pipeline_mode=pl.Buffered(1))
    tail = [final_g.reshape(1, d)] if final_g is not None else []
    return pl.pallas_call(
        functools.partial(_mix_ffn_kernel, len(a_list), gate is not None, final_g is not None),
        grid=(m // ROW_TILE,),
        in_specs=in_specs + [once(g.reshape(1, d)), slab(wg), slab(wu), slab(wd)] + [once(c) for c in tail],
        out_specs=row(d),
        out_shape=jax.ShapeDtypeStruct((m, d), F32),
        compiler_params=_cparams(1),
        name="mix_out_ffn",
    )(*args, g.reshape(1, d), wg, wu, wd, *tail)


def _compress_kernel(x_ref, p_ref, w1_ref, w2_ref, rows_ref, cols_ref):
    n = x_ref.shape[1] // CMP_STRIDE
    a = jnp.zeros((n, x_ref.shape[2]), F32)
    b = jnp.zeros((n, x_ref.shape[2]), F32)
    for r in range(CMP_STRIDE):
        x = x_ref[0, pl.ds(r, n, stride=CMP_STRIDE), :]
        a = a + _dot((x + p_ref[0, r]).astype(BF16), w1_ref[0, r])
        b = b + _dot((x + p_ref[0, CMP_STRIDE + r]).astype(BF16), w1_ref[0, CMP_STRIDE + r])
    h = a + pltpu.roll(b, n - 1, 0)
    out = _dot(_silu(h).astype(BF16), w2_ref[0])
    live = lax.broadcasted_iota(jnp.int32, out.shape, 0) < n - 1
    out = jnp.where(live, out, 0.0)
    rows_ref[0, 0] = out.astype(rows_ref.dtype)
    cols_ref[0, 0] = out.T.astype(cols_ref.dtype)


def _compress(x, pos, w1, w2):
    b, t, two_gd = x.shape
    gd = two_gd // 2
    n = t // CMP_STRIDE
    return pl.pallas_call(
        _compress_kernel,
        grid=(2, b),
        in_specs=[
            pl.BlockSpec((1, t, gd), lambda s, i: (i, 0, s)),
            pl.BlockSpec((1,) + pos.shape[1:], lambda s, i: (s, 0, 0, 0)),
            pl.BlockSpec((1,) + w1.shape[1:], lambda s, i: (s, 0, 0, 0)),
            pl.BlockSpec((1,) + w2.shape[1:], lambda s, i: (s, 0, 0)),
        ],
        out_specs=[pl.BlockSpec((1, 1, n, gd), lambda s, i: (s, i, 0, 0)),
                   pl.BlockSpec((1, 1, gd, n), lambda s, i: (s, i, 0, 0))],
        out_shape=[jax.ShapeDtypeStruct((2, b, n, gd), BF16), jax.ShapeDtypeStruct((2, b, gd, n), BF16)],
        compiler_params=_cparams(2),
        name="compress_kv",
    )(x, pos, w1, w2)


def _stack_heads_t(q_ref, rep):
    qt = q_ref[0, 0]
    return jnp.concatenate([qt[r * HEAD_DIM:(r + 1) * HEAD_DIM] for r in range(rep)], axis=1)


def _unstack_heads_t(o_t, rep):
    tq = o_t.shape[1] // rep
    return jnp.concatenate([o_t[:, r * tq:(r + 1) * tq] for r in range(rep)], axis=0).T


def _tile_lanes(x, rep):
    return jnp.concatenate([x] * rep, axis=1)


def _col_tree(x, pair, reduce):
    while x.shape[0] > SUBLANES:
        half = x.shape[0] // 2
        x = pair(x[:half], x[half:])
    return reduce(x, axis=0, keepdims=True)


def _attn_kernel(n_sel_blocks, sink_ref, qa_ref, qb_ref, gl_ref, gb_ref, kc_ref, vct_ref, selmap_ref,
                 ksd_ref, ks_ref, mask_ref, vsd_ref, vbig_ref, kw_ref, vw_ref, kbp_ref, kbc_ref, vbp_ref, vbc_ref,
                 eye_ref, cmask_ref, dmask_ref, wmask_ref, bmask_ref, oa_ref, ob_ref, m_sc, acc_sc, s_sc):
    n = pl.program_id(1)
    rep = NSA_REP
    n_groups = ksd_ref.shape[2] // HEAD_DIM
    gcols = rep * Q_BLOCK
    groups = range(n_groups)
    tq_row = n * Q_BLOCK + lax.broadcasted_iota(jnp.int32, (1, Q_BLOCK), 1)

    def pad_q(q_t):
        zero = jnp.zeros((HEAD_DIM, gcols), q_t.dtype)
        return jnp.concatenate([jnp.concatenate([q_t[:, i * gcols:(i + 1) * gcols] if i == j else zero
                                                 for i in groups], axis=1) for j in groups], axis=0)

    def per_group(fn):
        return jnp.concatenate([fn(i) for i in groups], axis=1)

    def apply_values(v_t, e):
        return per_group(lambda i: _dot(with_ones(v_t[i * HEAD_DIM:(i + 1) * HEAD_DIM]),
                                        e[:, i * gcols:(i + 1) * gcols]))

    def masked_logits(keys, mask, q_rows):
        return _dot(jnp.concatenate([keys, mask], axis=1), jnp.concatenate([q_rows, eye_ref[...]], axis=0))

    def with_ones(v_t):
        return jnp.concatenate([v_t, jnp.ones((SUBLANES, v_t.shape[1]), v_t.dtype)], axis=0)

    q_t = pad_q(_stack_heads_t(qa_ref, n_groups * rep))
    kc = kc_ref[0, 0]
    n_cmp = kc.shape[0]
    odd = lax.rem(n, 2)
    shift = Q_BLOCK // CMP_STRIDE
    start = pl.multiple_of(n_cmp - shift * (n + odd), 2 * SUBLANES)
    s_c = masked_logits(kc, cmask_ref[odd, pl.ds(start, n_cmp), :], q_t)
    q_b = pad_q(_stack_heads_t(qb_ref, n_groups * SWA_REP))
    k_b = jnp.concatenate([kbp_ref[0], kbc_ref[0]], axis=0)
    s_b = masked_logits(k_b, bmask_ref[jnp.minimum(n, bmask_ref.shape[0] - 1)], q_b)
    n_win_tiles = NSA_WINDOW // Q_BLOCK + 1
    tiles = [jnp.maximum(n - (n_win_tiles - 1) + i, 0) for i in range(n_win_tiles)]
    k_w = jnp.concatenate([kw_ref[0, pl.ds(pl.multiple_of(j * Q_BLOCK, Q_BLOCK), Q_BLOCK), :] for j in tiles], axis=0)
    s_w = masked_logits(k_w, wmask_ref[jnp.minimum(n, wmask_ref.shape[0] - 1)], q_t)
    s_d = masked_logits(ksd_ref[0], dmask_ref[...], q_t)

    m = jnp.max(s_c, axis=0, keepdims=True)
    e_c = jnp.exp2(s_c - m).astype(BF16)
    vct = vct_ref[0, 0]
    oc_imp = per_group(lambda i: _dot(
        jnp.concatenate([with_ones(vct[i * HEAD_DIM:(i + 1) * HEAD_DIM]), selmap_ref[...]], axis=0),
        e_c[:, i * gcols:(i + 1) * gcols]))
    den = oc_imp[HEAD_DIM:HEAD_DIM + 1]
    seen = m > 0.5 * NEG_INF
    oc_imp = jnp.where(seen, oc_imp * (1.0 / jnp.where(seen, den, 1.0)), 0.0)
    o_c = oc_imp[:HEAD_DIM]
    imp = oc_imp[HEAD_DIM + SUBLANES:]
    imp = per_group(lambda i: sum(imp[:, (i * rep + r) * Q_BLOCK:(i * rep + r + 1) * Q_BLOCK]
                                  for r in range(rep)))

    sink_row = jnp.concatenate([jnp.full((1, Q_BLOCK), sink_ref[h] * LOG2_E, F32)
                                for h in range(n_groups * SWA_REP)], axis=1)
    v_tiles_b = [vbp_ref[0, 0, 0], vbc_ref[0, 0, 0]]
    v_tiles_w = [vw_ref[0, 0, j] for j in tiles]
    v_tiles_d = [vsd_ref[0, 0, 0]]
    maxes, partial = {}, {"b": [], "w": [], "d": []}

    def col_max(name, s, extra=None):
        def run():
            m = jnp.max(s, axis=0, keepdims=True)
            maxes[name] = m if extra is None else jnp.maximum(m, extra)
        return run

    def pv_tile(name, s, v_tiles, i):
        def run():
            rows = slice(i * Q_BLOCK, (i + 1) * Q_BLOCK)
            partial[name].append(apply_values(v_tiles[i], jnp.exp2(s[rows] - maxes[name]).astype(BF16)))
        return run

    pieces = ([col_max("w", s_w), col_max("b", s_b, sink_row), col_max("d", s_d)]
              + [pv_tile("w", s_w, v_tiles_w, i) for i in range(len(v_tiles_w))]
              + [pv_tile("b", s_b, v_tiles_b, i) for i in range(len(v_tiles_b))]
              + [pv_tile("d", s_d, v_tiles_d, 0)])

    ns_pad = imp.shape[0]
    blk = lax.broadcasted_iota(jnp.int32, (ns_pad, n_groups * Q_BLOCK), 0)
    blk_f = blk.astype(F32)
    tq_row = _tile_lanes(tq_row, n_groups)
    cur = jnp.right_shift(tq_row, SEL_SHIFT)
    forced = (blk == 0) | (blk == cur) | (blk == cur - 1)
    valid = jnp.left_shift(blk, SEL_SHIFT) <= tq_row
    score = jnp.where(valid, imp, NEG_INF)
    score = jnp.where(forced | (blk >= n_sel_blocks), BELOW_NEG_INF, score)
    sel = forced
    for _ in range(min(SEL_TOPK, n_sel_blocks) - 3):
        if pieces:
            pieces.pop(0)()
        top = _col_tree(score, jnp.maximum, jnp.max)
        first_hit = _col_tree(jnp.where(score == top, blk_f, float(ns_pad)), jnp.minimum, jnp.min)
        pick = blk_f == first_hit
        sel = sel | pick
        score = jnp.where(pick, BELOW_NEG_INF, score)
    n_diag_blk = n * (Q_BLOCK // SEL_LEN)
    unselected = jnp.where(sel & (blk < n_diag_blk), 0.0, 1.0).astype(BF16)

    while pieces:
        pieces.pop(0)()

    def finish(name, sink=None):
        acc = sum(partial[name][1:], partial[name][0])
        den = acc[HEAD_DIM:HEAD_DIM + 1]
        if sink is not None:
            den = den + jnp.exp2(sink - maxes[name])
        return acc[:HEAD_DIM] * (1.0 / den)

    m_sc[...] = maxes["d"]
    acc_sc[...] = partial["d"][0]

    q_aug = jnp.concatenate([per_group(lambda i: _tile_lanes(unselected[:, i * Q_BLOCK:(i + 1) * Q_BLOCK], rep)),
                             q_t], axis=0)
    tk = s_sc.shape[1]
    n_tiles = ks_ref.shape[1] // tk

    def logits(kt):
        keys = pl.ds(pl.multiple_of(kt * tk, tk), tk)
        return _dot(jnp.concatenate([mask_ref[keys, :], ks_ref[0, keys, :]], axis=1), q_aug)

    def absorb(kt, s):
        m_old = m_sc[...]
        m_new = jnp.maximum(m_old, jnp.max(s, axis=0, keepdims=True))
        pv = apply_values(vbig_ref[0, kt], jnp.exp2(s - m_new).astype(BF16))
        acc_sc[...] = jnp.exp2(m_old - m_new) * acc_sc[...] + pv
        m_sc[...] = m_new

    s_sc[0] = logits(0)
    ob_ref[0] = _unstack_heads_t(finish("b", sink_row), n_groups * SWA_REP).astype(ob_ref.dtype)
    o_w = finish("w")

    def tile_pair(j, carry):
        s_sc[1] = logits(2 * j + 1)
        absorb(2 * j, s_sc[0])
        s_sc[0] = logits(jnp.minimum(2 * j + 2, n_tiles - 1))
        absorb(2 * j + 1, s_sc[1])
        return carry

    lax.fori_loop(0, lax.div(n * Q_BLOCK + 2 * tk - 1, 2 * tk), tile_pair, 0)
    acc = acc_sc[...]
    o_s = acc[:HEAD_DIM] * (1.0 / acc[HEAD_DIM:HEAD_DIM + 1])

    gates = _sigmoid(gl_ref[0, 0] + gb_ref[...])
    gate = lambda c: jnp.concatenate([gates[16 * i + 3 * r + c:16 * i + 3 * r + c + 1]
                                      for i in groups for r in range(rep)], axis=1)
    o = gate(0) * o_c + gate(1) * o_s + gate(2) * o_w
    oa_ref[0] = _unstack_heads_t(o, n_groups * rep).astype(oa_ref.dtype)


def _selection_map(t, nc_pad, ns_pad):
    nc = (t - CMP_LEN) // CMP_STRIDE + 1
    ns = t // SEL_LEN
    cs = np.arange(nc, dtype=np.int64)[:, None] * CMP_STRIDE
    ss = np.arange(ns, dtype=np.int64)[None, :] * SEL_LEN
    out = np.zeros((nc_pad, ns_pad), np.float32)
    out[:nc, :ns] = (cs < ss + SEL_LEN) & (cs + CMP_LEN > ss)
    return out


def _block_mask_cols(t, ns_pad):
    hit = np.arange(ns_pad, dtype=np.int64)[None, :] == (np.arange(t, dtype=np.int64) // SEL_LEN)[:, None]
    return np.where(hit, NEG_INF, 0.0).astype(np.float32)


def _band_masks(n_rows, window):
    n_front = n_rows // Q_BLOCK - 1
    r = np.arange(n_rows)[None, :, None]
    tok = np.arange(Q_BLOCK)[None, None, :]
    v = np.arange(n_front + 1)[:, None, None]
    dist = n_front * Q_BLOCK + tok - r
    ok = (dist >= 0) & (dist < window) & (r >= (n_front - v) * Q_BLOCK)
    return np.where(ok, 0.0, NEG_INF).astype(np.float32)


def _compressed_masks(n_cmp):
    shift = Q_BLOCK // CMP_STRIDE
    u = np.arange(2 * n_cmp)[None, :, None] + shift * np.arange(2)[:, None, None]
    tok = np.arange(Q_BLOCK)[None, None, :]
    ok = CMP_STRIDE * (u - n_cmp) + CMP_LEN - 1 <= tok
    return np.where(ok, 0.0, NEG_INF).astype(np.float32)


def _attention(q, k_all, v_big, v_small, gate_t, gate_bias, cmp_rows, cmp_cols, sel_map_t, mask_cols, sinks,
               n_sel_blocks):
    b, t = k_all.shape[:2]
    gd = k_all.shape[2] // 3
    g = gd // HEAD_DIM
    tk = v_big.shape[3]
    assert (t // tk) % 2 == 0, "selected-branch key tiles are consumed in pairs"
    width = g * NSA_REP * HEAD_DIM
    cols = g * NSA_REP * Q_BLOCK
    gate_cols = gate_t.shape[3] // Q_BLOCK
    prev = lambda n: jnp.maximum(n - 1, 0)
    cur = lambda n: n
    k_tile = lambda blk, at: pl.BlockSpec((1, Q_BLOCK, gd), lambda i, n: (i, at(n), blk))
    k_all_rows = lambda blk: pl.BlockSpec((1, t, gd), lambda i, n: (i, 0, blk))
    v_tile = lambda a, at: pl.BlockSpec((1, 1, 1, gd, Q_BLOCK), lambda i, n: (a, i, at(n), 0, 0))
    out_spec = pl.BlockSpec((1, Q_BLOCK, width), lambda i, n: (i, n, 0))
    out_shape = jax.ShapeDtypeStruct((b, t, width), BF16)
    tri = np.where(np.arange(Q_BLOCK)[:, None] <= np.arange(Q_BLOCK)[None, :], 0.0, NEG_INF)
    consts = [jnp.asarray(c, BF16) for c in (
        np.tile(np.eye(Q_BLOCK, dtype=np.float32), (1, g * NSA_REP)),
        _compressed_masks(cmp_rows.shape[2]), tri,
        _band_masks(NSA_WINDOW + Q_BLOCK, NSA_WINDOW), _band_masks(SWA_WINDOW + Q_BLOCK, SWA_WINDOW))]
    return pl.pallas_call(
        functools.partial(_attn_kernel, n_sel_blocks),
        grid=(b, t // Q_BLOCK),
        in_specs=[
            pl.BlockSpec(memory_space=pltpu.SMEM),
            pl.BlockSpec((1, 1, width, Q_BLOCK), lambda i, n: (i, n // gate_cols, 0, n % gate_cols)),
            pl.BlockSpec((1, 1, width, Q_BLOCK), lambda i, n: (i, n // gate_cols, 1, n % gate_cols)),
            pl.BlockSpec((1, 1, 16 * g, Q_BLOCK), lambda i, n: (i, n // gate_cols, 0, n % gate_cols)),
            _resident(gate_bias.shape),
            pl.BlockSpec((1, 1) + cmp_rows.shape[2:], lambda i, n: (0, i, 0, 0)),
            pl.BlockSpec((1, 1) + cmp_cols.shape[2:], lambda i, n: (1, i, 0, 0)),
            _resident(sel_map_t.shape),
            k_tile(0, cur), k_all_rows(0), _resident(mask_cols.shape),
            v_tile(0, cur), pl.BlockSpec((1,) + v_big.shape[1:], lambda i, n: (i, 0, 0, 0)),
            k_all_rows(1), pl.BlockSpec((1, 1) + v_small.shape[2:], lambda i, n: (1, i, 0, 0, 0)),
            k_tile(2, prev), k_tile(2, cur), v_tile(2, prev), v_tile(2, cur),
        ] + [_resident(c.shape) for c in consts],
        out_specs=[out_spec, out_spec],
        out_shape=[out_shape, out_shape],
        scratch_shapes=[
            pltpu.VMEM((1, cols), F32),
            pltpu.VMEM((HEAD_DIM + SUBLANES, cols), F32),
            pltpu.VMEM((2, tk, cols), F32),
        ],
        compiler_params=_cparams(2),
        name="nsa_swa_attention",
    )(sinks, q, q, gate_t, gate_bias, cmp_rows, cmp_cols, sel_map_t, k_all, k_all, mask_cols, v_small, v_big,
      k_all, v_small, k_all, k_all, v_small, v_small, *consts)


def _conv_silu_kernel(n_q_cols, x_ref, prev_ref, shift_ref, w_ref, b_ref, o_ref, buf):
    i = pl.program_id(1)
    rows = x_ref.shape[1]
    halo = prev_ref.shape[1]
    taps = CONV_WIDTH - 1
    buf[:halo, :] = jnp.where(i > 0, prev_ref[0], jnp.zeros_like(prev_ref[0]))
    buf[halo:, :] = x_ref[0]
    col = lax.broadcasted_iota(jnp.int32, (Q_BLOCK, x_ref.shape[2]), 1)
    for r in range(rows // Q_BLOCK):
        win = buf[r * Q_BLOCK:(r + 1) * Q_BLOCK + halo, :]
        back = _dot(shift_ref[...], win)
        y = b_ref[...] + w_ref[taps:taps + 1, :] * win[halo:].astype(F32)
        for j in range(taps):
            y = y + w_ref[j:j + 1, :] * back[j * Q_BLOCK:(j + 1) * Q_BLOCK]
        y = _silu(y)
        y = jnp.where(col >= n_q_cols, y * (MLSTM_QK_DIM ** -0.5), y)
        o_ref[0, r * Q_BLOCK:(r + 1) * Q_BLOCK, :] = y.astype(o_ref.dtype)


def _conv_silu(x, w, bias):
    b, t, c = x.shape
    halo = 2 * SUBLANES
    ratio = ROW_TILE // halo
    taps = CONV_WIDTH - 1
    shift = np.zeros((taps * Q_BLOCK, Q_BLOCK + halo), np.float32)
    for j in range(taps):
        shift[j * Q_BLOCK + np.arange(Q_BLOCK), np.arange(Q_BLOCK) + halo - taps + j] = 1.0
    return pl.pallas_call(
        functools.partial(_conv_silu_kernel, c // 2),
        grid=(b, t // ROW_TILE),
        in_specs=[
            pl.BlockSpec((1, ROW_TILE, c), lambda i, j: (i, j, 0)),
            pl.BlockSpec((1, halo, c), lambda i, j: (i, jnp.maximum(j * ratio - 1, 0), 0)),
            _resident(shift.shape),
            _resident(w.shape),
            _resident((1, c)),
        ],
        out_specs=pl.BlockSpec((1, ROW_TILE, c), lambda i, j: (i, j, 0)),
        out_shape=jax.ShapeDtypeStruct((b, t, c), BF16),
        scratch_shapes=[pltpu.VMEM((ROW_TILE + halo, c), BF16)],
        compiler_params=_cparams(2),
        name="conv_silu",
    )(x, x, jnp.asarray(shift, BF16), w, bias.reshape(1, c))


def _mlstm_kernel(bias_ref, q_ref, k_ref, v_ref, g_ref, o_ref, state, m_state):
    c = pl.program_id(1)
    L = SCAN_CHUNK
    dqk, dv = MLSTM_QK_DIM, MLSTM_V_DIM

    @pl.when(c == 0)
    def _():
        state[...] = jnp.zeros(state.shape, F32)
        m_state[...] = jnp.zeros(m_state.shape, F32)

    t_idx = lax.broadcasted_iota(jnp.int32, (L, L), 0)
    s_idx = lax.broadcasted_iota(jnp.int32, (L, L), 1)
    causal = s_idx <= t_idx
    eye = s_idx == t_idx
    ones_col = (lax.broadcasted_iota(jnp.int32, (L, LANES), 1) == 0).astype(BF16)

    def to_col(row):
        return jnp.sum(jnp.where(eye, row, 0.0), axis=1, keepdims=True)

    heads = range(MLSTM_HEADS)
    for j in range(SCAN_ROWS // SCAN_CHUNK):
        rows = slice(j * L, (j + 1) * L)
        qc = [q_ref[0, rows, h * dqk:(h + 1) * dqk] for h in heads]
        kc = [k_ref[0, rows, h * dqk:(h + 1) * dqk] for h in heads]
        va = [jnp.concatenate([v_ref[0, rows, h * dv:(h + 1) * dv], ones_col], axis=1) for h in heads]
        qk = [lax.dot_general(qc[h], kc[h], (((1,), (1,)), ((), ())), preferred_element_type=F32) for h in heads]
        i_row, b_col, b_row = [], [], []
        for h in heads:
            i_row.append(g_ref[0, 0, h:h + 1, rows] + bias_ref[0, h])
            f_pre = g_ref[0, 0, MLSTM_HEADS + h:MLSTM_HEADS + h + 1, rows] + bias_ref[1, h]
            f_row = jnp.minimum(f_pre, 0.0) - jnp.log(1.0 + jnp.exp(-jnp.abs(f_pre)))
            f_col = to_col(f_row)
            b_col.append(jnp.sum(jnp.where(causal, f_row, 0.0), axis=1, keepdims=True))
            b_row.append(jnp.sum(jnp.where(t_idx <= s_idx, f_col, 0.0), axis=0, keepdims=True))
        m_old = [m_state[h:h + 1, 0:1] for h in heads]
        st = [state[h] for h in heads]
        mt, w_inter, sm = [], [], []
        for h in heads:
            a = b_col[h] + m_old[h]
            d = jnp.where(causal, b_col[h] - b_row[h] + i_row[h], -jnp.inf)
            mt.append(jnp.maximum(a, jnp.max(d, axis=1, keepdims=True)))
            w_inter.append(jnp.exp(a - mt[h]))
            sm.append((qk[h] * jnp.exp(d - mt[h])).astype(BF16))
        for h in heads:
            nd = w_inter[h] * _dot(qc[h], st[h].astype(BF16)) + _dot(sm[h], va[h])
            hc = nd[:, :dv] * (1.0 / jnp.maximum(jnp.abs(nd[:, dv:dv + 1]), jnp.exp(-mt[h])))
            o_ref[0, rows, h * dv:(h + 1) * dv] = hc.astype(o_ref.dtype)
        for h in heads:
            b_last = b_row[h][:, L - 1:L]
            wl = b_last - b_row[h] + i_row[h]
            m_new = jnp.maximum(b_last + m_old[h], jnp.max(wl, axis=1, keepdims=True))
            decay = jnp.exp(b_last + m_old[h] - m_new)
            w_col = to_col(jnp.exp(wl - m_new))
            kw = (kc[h].astype(F32) * w_col).astype(BF16)
            upd = lax.dot_general(kw, va[h], (((0,), (0,)), ((), ())), preferred_element_type=F32)
            state[h] = decay * st[h] + upd
            m_state[h:h + 1, 0:1] = m_new


def _mlstm(qk, v, gates_t, gate_bias):
    b, t = qk.shape[:2]
    nh = MLSTM_HEADS
    rows = SCAN_ROWS
    assert gates_t.shape[3] == rows
    return pl.pallas_call(
        _mlstm_kernel,
        grid=(b, t // rows),
        in_specs=[
            pl.BlockSpec(memory_space=pltpu.SMEM),
            pl.BlockSpec((1, rows, nh * MLSTM_QK_DIM), lambda i, c: (i, c, 0)),
            pl.BlockSpec((1, rows, nh * MLSTM_QK_DIM), lambda i, c: (i, c, 1)),
            pl.BlockSpec((1, rows, nh * MLSTM_V_DIM), lambda i, c: (i, c, 0)),
            pl.BlockSpec((1, 1) + gates_t.shape[2:], lambda i, c: (i, c, 0, 0)),
        ],
        out_specs=pl.BlockSpec((1, rows, nh * MLSTM_V_DIM), lambda i, c: (i, c, 0)),
        out_shape=jax.ShapeDtypeStruct((b, t, nh * MLSTM_V_DIM), BF16),
        scratch_shapes=[
            pltpu.VMEM((nh, MLSTM_QK_DIM, MLSTM_V_DIM + LANES), F32),
            pltpu.VMEM((SUBLANES, LANES), F32),
        ],
        compiler_params=_cparams(2),
        name="mlstm_scan",
    )(gate_bias, qk, qk, v, gates_t)


def _block_diag(w, n):
    zero = jnp.zeros_like(w)
    return jnp.concatenate([jnp.concatenate([w if i == j else zero for i in range(n)], axis=-1)
                            for j in range(n)], axis=-2)


def _nsa_swa_layer(h, norm_g, w_in, gate_bias, pos_k, pos_v, ck_w1, ck_w2, cv_w1, cv_w2, sinks, w_out):
    b, t, d = h.shape
    m = b * t
    ng = NSA_KV_HEADS
    gk = ng * HEAD_DIM
    nq_a, nq_b = NSA_HEADS * HEAD_DIM, SWA_HEADS * HEAD_DIM
    ngate = NSA_HEADS * 3
    o_kc, o_vc, o_ks, o_vs, o_kw, o_vw = (nq_a + i * gk for i in range(6))
    o_g = nq_a + 6 * gk
    o_qb = o_g + ngate
    o_kb, o_vb = o_qb + nq_b, o_qb + nq_b + gk
    cols = lambda o, n: w_in[:, o:o + n]
    q_scale = HEAD_DIM ** -0.5 * LOG2_E
    gate_pad = jnp.zeros((d, 16 - NSA_REP * 3), w_in.dtype)
    w_gate = jnp.concatenate([p for i in range(ng) for p in (cols(o_g + i * NSA_REP * 3, NSA_REP * 3), gate_pad)]
                             + [jnp.zeros((d, LANES - 16 * ng), w_in.dtype)], axis=1)
    w_all = jnp.concatenate([cols(0, nq_a) * q_scale, cols(o_qb, nq_b) * q_scale,
                             cols(o_ks, gk), cols(o_kw, gk), cols(o_kb, gk),
                             cols(o_vs, gk), cols(o_vw, gk), cols(o_vb, gk),
                             cols(o_kc, gk), cols(o_vc, gk), w_gate], axis=1).astype(BF16)
    q_all, k_all, cmp_in, v_big, v_small, gate_t = _proj0(
        h.reshape(m, d), norm_g, w_all, [nq_a + nq_b, 3 * gk, 3 * gk, 2 * gk + LANES], 2 * gk, 16 * ng)
    tk = ROW_TILE
    q_all = q_all.reshape(b, t // ROW_TILE, nq_a + nq_b, ROW_TILE)
    k_all = k_all.reshape(b, t, 3 * gk)
    v_big = v_big.reshape(b, t // tk, gk, tk)
    v_small = v_small.reshape(3, b, t // Q_BLOCK, gk, Q_BLOCK)
    gate_t = gate_t.reshape(b, t // ROW_TILE, 16 * ng, ROW_TILE)
    gb = jnp.pad(gate_bias.reshape(ng, NSA_REP * 3), ((0, 0), (0, 16 - NSA_REP * 3))).reshape(ng * 16, 1)

    tile2 = lambda p: jnp.tile(p[:, None, :], (1, 1, ng))
    pos = jnp.stack([tile2(pos_k), tile2(pos_v)])
    per_pos = lambda w: w.reshape(CMP_LEN, HEAD_DIM, HEAD_DIM)
    w1 = _block_diag(jnp.stack([per_pos(ck_w1), per_pos(cv_w1)]), ng).astype(BF16)
    w2 = _block_diag(jnp.stack([ck_w2, cv_w2]), ng).astype(BF16)
    cmp_rows, cmp_cols = _compress(cmp_in.reshape(b, t, 2 * gk), pos, w1, w2)

    n_sel = t // SEL_LEN
    ns_pad = -(-n_sel // LANES) * LANES
    sel_map_t = jnp.asarray(_selection_map(t, t // CMP_STRIDE, ns_pad).T, BF16)
    mask_cols = jnp.asarray(_block_mask_cols(t, ns_pad), BF16)
    o_a, o_b = _attention(q_all, k_all, v_big, v_small, gate_t, gb, cmp_rows, cmp_cols, sel_map_t, mask_cols,
                          sinks, n_sel)

    w_o = w_out.astype(BF16)
    return [o_a.reshape(m, nq_a), o_b.reshape(m, nq_b)], [w_o[:nq_a], w_o[nq_a:]], None


def _mlstm_layer(h, norm_g, w_in, conv_w, conv_b, igate_bias, fgate_bias, w_out):
    b, t, d = h.shape
    m = b * t
    nqk = 2 * MLSTM_HEADS * MLSTM_QK_DIM
    nv = MLSTM_HEADS * MLSTM_V_DIM
    w = jnp.concatenate([w_in, jnp.zeros((d, LANES - 2 * MLSTM_HEADS), w_in.dtype)], axis=1).astype(BF16)
    qk_raw, v, og, gates_t = _norm_proj(h.reshape(m, d), norm_g, w, [nqk, nv, nv, LANES], [BF16, BF16, BF16],
                                        2 * MLSTM_HEADS)
    qk = _conv_silu(qk_raw.reshape(b, t, nqk), conv_w, conv_b)
    gates_t = gates_t.reshape(b, t // ROW_TILE, 2 * MLSTM_HEADS, ROW_TILE)
    hh = _mlstm(qk, v.reshape(b, t, nv), gates_t, jnp.stack([igate_bias, fgate_bias]))
    return [hh.reshape(m, nv)], [w_out.astype(BF16)], og


def kernel(x, norm_mix, norm_ffn, ffn_w_gate, ffn_w_up, ffn_w_down, ab_w_in, ab_gate_bias, nsa_pos_k, nsa_pos_v, nsa_cmp_k_w1, nsa_cmp_k_w2, nsa_cmp_v_w1, nsa_cmp_v_w2, swa_sinks, ab_w_out, c_w_in, c_conv_w, c_conv_b, c_igate_bias, c_fgate_bias, c_w_out, final_norm):
    depth = norm_mix.shape[0]
    b, t, d = x.shape
    h = x
    ffn_w = [w.astype(BF16) for w in (ffn_w_gate, ffn_w_up, ffn_w_down)]
    for layer in range(depth):
        j = layer // 2
        if layer % 2 == 0:
            mixed = _nsa_swa_layer(h, norm_mix[layer], ab_w_in[j], ab_gate_bias[j], nsa_pos_k[j], nsa_pos_v[j],
                               nsa_cmp_k_w1[j], nsa_cmp_k_w2[j], nsa_cmp_v_w1[j], nsa_cmp_v_w2[j],
                               swa_sinks[j], ab_w_out[j])
        else:
            mixed = _mlstm_layer(h, norm_mix[layer], c_w_in[j], c_conv_w[j], c_conv_b[j], c_igate_bias[j],
                             c_fgate_bias[j], c_w_out[j])
        last = layer == depth - 1
        h = _mix_ffn(h.reshape(b * t, d), *mixed, norm_ffn[layer], layer, *ffn_w,
                     final_norm if last else None).reshape(b, t, d)
    if depth == 0:
        raise ValueError("depth must be positive")
    return h
```

```python
import functools

import numpy as np
import jax
import jax.numpy as jnp
from jax import lax
from jax.experimental import pallas as pl
from jax.experimental.pallas import tpu as pltpu

HEAD_DIM = 64
NSA_HEADS = 8
NSA_KV_HEADS = 2
NSA_REP = NSA_HEADS // NSA_KV_HEADS
CMP_LEN = 32
CMP_STRIDE = 16
SEL_LEN = 64
SEL_TOPK = 16
NSA_WINDOW = 512
SWA_HEADS = 8
SWA_KV_HEADS = 2
SWA_REP = SWA_HEADS // SWA_KV_HEADS
SWA_WINDOW = 128
Q_BLOCK = 128
MLSTM_HEADS = 4
MLSTM_QK_DIM = 128
MLSTM_V_DIM = 256
MLSTM_CHUNK = 64
CONV_WIDTH = 4
RMS_EPS = 1e-6
NEG_INF = -1e30
FORCE_SCORE = 1e9
BELOW_NEG_INF = -3e38
LOG2_E = 1.4426950408889634
SEL_SHIFT = SEL_LEN.bit_length() - 1
assert 1 << SEL_SHIFT == SEL_LEN

LANES = 128
SUBLANES = 8
VMEM_LIMIT_BYTES = 56 * 1024 * 1024

ROW_TILE = 512
SCAN_CHUNK = 512
SCAN_ROWS = 512

F32 = jnp.float32
BF16 = jnp.bfloat16


def _cparams(n_axes):
    return pltpu.CompilerParams(
        dimension_semantics=("arbitrary",) * n_axes,
        vmem_limit_bytes=VMEM_LIMIT_BYTES,
    )


def _resident(shape):
    nd = len(shape)
    return pl.BlockSpec(shape, lambda *_: (0,) * nd)


def _dot(a, b):
    return jnp.dot(a, b, preferred_element_type=F32)


def _sigmoid(x):
    return 1.0 / (1.0 + jnp.exp(-x))


def _silu(x):
    return x * _sigmoid(x)


def _rmsnorm(x, g):
    y = x * lax.rsqrt(jnp.mean(x * x, axis=-1, keepdims=True) + RMS_EPS)
    return y * g


def _col_ranges(widths):
    ends = np.cumsum(widths).tolist()
    return list(zip([0] + ends[:-1], ends))


def _norm_proj_kernel(widths, x_ref, g_ref, w_ref, *o_refs):
    ranges = _col_ranges(widths)
    xn = _rmsnorm(x_ref[...], g_ref[...]).astype(BF16)
    for (lo, hi), o_ref in zip(ranges[:-2], o_refs[:-2]):
        o_ref[...] = _dot(xn, w_ref[:, lo:hi]).astype(o_ref.dtype)
    (lo, mid), (_, hi) = ranges[-2:]
    both = _dot(xn, w_ref[:, lo:hi])
    o_refs[-2][...] = both[:, :mid - lo].astype(o_refs[-2].dtype)
    o_refs[-1][0] = both[:, mid - lo:].T[:o_refs[-1].shape[1]]


def _norm_proj(x, g, w, widths, out_dtypes, t_rows):
    m, d = x.shape
    steps = m // ROW_TILE
    row = lambda n: pl.BlockSpec((ROW_TILE, n), lambda i: (i, 0))
    return pl.pallas_call(
        functools.partial(_norm_proj_kernel, tuple(widths)),
        grid=(steps,),
        in_specs=[row(d), _resident((1, d)), _resident(w.shape)],
        out_specs=[row(n) for n in widths[:-1]] + [pl.BlockSpec((1, t_rows, ROW_TILE), lambda i: (i, 0, 0))],
        out_shape=[jax.ShapeDtypeStruct((m, n), dt) for n, dt in zip(widths[:-1], out_dtypes)]
        + [jax.ShapeDtypeStruct((steps, t_rows, ROW_TILE), F32)],
        compiler_params=_cparams(1),
        name="norm_proj",
    )(x, g.reshape(1, d), w)


def _proj0_kernel(widths, x_ref, g_ref, w_ref, q_ref, k_ref, c_ref, vbig_ref, vsmall_ref, gt_ref):
    (q0, q1), (k0, k1), (v0, v1), (c0, c1) = _col_ranges(widths)
    xn = _rmsnorm(x_ref[...], g_ref[...]).astype(BF16)
    q = _dot(xn, w_ref[:, q0:q1])
    half = (q1 - q0) // 2
    q_ref[0, :half] = q[:, :half].T.astype(q_ref.dtype)
    q_ref[0, half:] = q[:, half:].T.astype(q_ref.dtype)
    k_ref[...] = _dot(xn, w_ref[:, k0:k1]).astype(k_ref.dtype)
    cg = _dot(xn, w_ref[:, c0:c1])
    c_ref[...] = cg[:, :c_ref.shape[1]]
    gt_ref[0] = cg[:, c_ref.shape[1]:].T[:gt_ref.shape[1]]
    v = _dot(xn, w_ref[:, v0:v1])
    width = v.shape[1] // 3
    for a in range(3):
        v_t = v[:, a * width:(a + 1) * width].T.astype(BF16)
        if a == 0:
            vbig_ref[0] = v_t
        for j in range(x_ref.shape[0] // Q_BLOCK):
            vsmall_ref[a, j] = v_t[:, j * Q_BLOCK:(j + 1) * Q_BLOCK]


def _proj0(x, g, w, widths, n_cmp_cols, gate_rows):
    m, d = x.shape
    steps = m // ROW_TILE
    tiles = ROW_TILE // Q_BLOCK
    nq, nk, nv, _ = widths
    gd = nv // 3
    row = lambda n: pl.BlockSpec((ROW_TILE, n), lambda i: (i, 0))
    return pl.pallas_call(
        functools.partial(_proj0_kernel, tuple(widths)),
        grid=(steps,),
        in_specs=[row(d), _resident((1, d)), _resident(w.shape)],
        out_specs=[
            pl.BlockSpec((1, nq, ROW_TILE), lambda i: (i, 0, 0)), row(nk), row(n_cmp_cols),
            pl.BlockSpec((1, gd, ROW_TILE), lambda i: (i, 0, 0)),
            pl.BlockSpec((3, tiles, gd, Q_BLOCK), lambda i: (0, i, 0, 0)),
            pl.BlockSpec((1, gate_rows, ROW_TILE), lambda i: (i, 0, 0)),
        ],
        out_shape=[
            jax.ShapeDtypeStruct((steps, nq, ROW_TILE), BF16),
            jax.ShapeDtypeStruct((m, nk), BF16),
            jax.ShapeDtypeStruct((m, n_cmp_cols), F32),
            jax.ShapeDtypeStruct((steps, gd, ROW_TILE), BF16),
            jax.ShapeDtypeStruct((3, steps * tiles, gd, Q_BLOCK), BF16),
            jax.ShapeDtypeStruct((steps, gate_rows, ROW_TILE), F32),
        ],
        compiler_params=_cparams(1),
        name="norm_proj_attn",
    )(x, g.reshape(1, d), w)


def _mix_ffn_kernel(n_in, gated, final, res_ref, *refs):
    a_refs, w_refs = refs[:n_in], refs[n_in:2 * n_in]
    rest = list(refs[2 * n_in:])
    o_ref = rest.pop()
    gate_ref = rest.pop(0) if gated else None
    g_ref, wg_ref, wu_ref, wd_ref = rest[:4]
    h = res_ref[...]
    for idx, (a_ref, w_ref) in enumerate(zip(a_refs, w_refs)):
        a = a_ref[...]
        if gated and idx == 0:
            a = (a.astype(F32) * _sigmoid(gate_ref[...].astype(F32))).astype(BF16)
        h = h + _dot(a, w_ref[...])
    xn = _rmsnorm(h, g_ref[...]).astype(BF16)
    act = (_silu(_dot(xn, wg_ref[0])) * _dot(xn, wu_ref[0])).astype(BF16)
    out = h + _dot(act, wd_ref[0])
    if final:
        out = _rmsnorm(out, rest[4][...])
    o_ref[...] = out


def _mix_ffn(res, a_list, w_list, gate, g, layer, wg, wu, wd, final_g=None):
    m, d = res.shape
    row = lambda w: pl.BlockSpec((ROW_TILE, w), lambda i: (i, 0))
    once = lambda a: pl.BlockSpec(a.shape, lambda *_: (0,) * a.ndim, pipeline_mode=pl.Buffered(1))
    in_specs = [row(d)] + [row(a.shape[1]) for a in a_list] + [once(w) for w in w_list]
    args = [res, *a_list, *w_list]
    if gate is not None:
        in_specs.append(row(gate.shape[1]))
        args.append(gate)
    slab = lambda a: pl.BlockSpec((1,) + a.shape[1:], lambda *_: (layer, 0, 0), pipeline_mode=pl.Buffered(1))
    tail = [final_g.reshape(1, d)] if final_g is not None else []
    return pl.pallas_call(
        functools.partial(_mix_ffn_kernel, len(a_list), gate is not None, final_g is not None),
        grid=(m // ROW_TILE,),
        in_specs=in_specs + [once(g.reshape(1, d)), slab(wg), slab(wu), slab(wd)] + [once(c) for c in tail],
        out_specs=row(d),
        out_shape=jax.ShapeDtypeStruct((m, d), F32),
        compiler_params=_cparams(1),
        name="mix_out_ffn",
    )(*args, g.reshape(1, d), wg, wu, wd, *tail)


def _compress_kernel(x_ref, p_ref, w1_ref, w2_ref, rows_ref, cols_ref):
    n = x_ref.shape[1] // CMP_STRIDE
    a = jnp.zeros((n, x_ref.shape[2]), F32)
    b = jnp.zeros((n, x_ref.shape[2]), F32)
    for r in range(CMP_STRIDE):
        x = x_ref[0, pl.ds(r, n, stride=CMP_STRIDE), :]
        a = a + _dot((x + p_ref[0, r]).astype(BF16), w1_ref[0, r])
        b = b + _dot((x + p_ref[0, CMP_STRIDE + r]).astype(BF16), w1_ref[0, CMP_STRIDE + r])
    h = a + pltpu.roll(b, n - 1, 0)
    out = _dot(_silu(h).astype(BF16), w2_ref[0])
    live = lax.broadcasted_iota(jnp.int32, out.shape, 0) < n - 1
    out = jnp.where(live, out, 0.0)
    rows_ref[0, 0] = out.astype(rows_ref.dtype)
    cols_ref[0, 0] = out.T.astype(cols_ref.dtype)


def _compress(x, pos, w1, w2):
    b, t, two_gd = x.shape
    gd = two_gd // 2
    n = t // CMP_STRIDE
    return pl.pallas_call(
        _compress_kernel,
        grid=(2, b),
        in_specs=[
            pl.BlockSpec((1, t, gd), lambda s, i: (i, 0, s)),
            pl.BlockSpec((1,) + pos.shape[1:], lambda s, i: (s, 0, 0, 0)),
            pl.BlockSpec((1,) + w1.shape[1:], lambda s, i: (s, 0, 0, 0)),
            pl.BlockSpec((1,) + w2.shape[1:], lambda s, i: (s, 0, 0)),
        ],
        out_specs=[pl.BlockSpec((1, 1, n, gd), lambda s, i: (s, i, 0, 0)),
                   pl.BlockSpec((1, 1, gd, n), lambda s, i: (s, i, 0, 0))],
        out_shape=[jax.ShapeDtypeStruct((2, b, n, gd), BF16), jax.ShapeDtypeStruct((2, b, gd, n), BF16)],
        compiler_params=_cparams(2),
        name="compress_kv",
    )(x, pos, w1, w2)


def _stack_heads_t(q_ref, rep):
    qt = q_ref[0, 0]
    return jnp.concatenate([qt[r * HEAD_DIM:(r + 1) * HEAD_DIM] for r in range(rep)], axis=1)


def _unstack_heads_t(o_t, rep):
    tq = o_t.shape[1] // rep
    return jnp.concatenate([o_t[:, r * tq:(r + 1) * tq] for r in range(rep)], axis=0).T


def _tile_lanes(x, rep):
    return jnp.concatenate([x] * rep, axis=1)


def _col_tree(x, pair, reduce):
    while x.shape[0] > SUBLANES:
        half = x.shape[0] // 2
        x = pair(x[:half], x[half:])
    return reduce(x, axis=0, keepdims=True)


def _attn_kernel(n_sel_blocks, sink_ref, qa_ref, qb_ref, gl_ref, gb_ref, kc_ref, vct_ref, selmap_ref,
                 ksd_ref, ks_ref, mask_ref, vsd_ref, vbig_ref, kw_ref, vw_ref, kbp_ref, kbc_ref, vbp_ref, vbc_ref,
                 eye_ref, cmask_ref, dmask_ref, wmask_ref, bmask_ref, oa_ref, ob_ref, m_sc, acc_sc, s_sc):
    n = pl.program_id(1)
    rep = NSA_REP
    n_groups = ksd_ref.shape[2] // HEAD_DIM
    gcols = rep * Q_BLOCK
    groups = range(n_groups)
    tq_row = n * Q_BLOCK + lax.broadcasted_iota(jnp.int32, (1, Q_BLOCK), 1)

    def pad_q(q_t):
        zero = jnp.zeros((HEAD_DIM, gcols), q_t.dtype)
        return jnp.concatenate([jnp.concatenate([q_t[:, i * gcols:(i + 1) * gcols] if i == j else zero
                                                 for i in groups], axis=1) for j in groups], axis=0)

    def per_group(fn):
        return jnp.concatenate([fn(i) for i in groups], axis=1)

    def apply_values(v_t, e):
        return per_group(lambda i: _dot(with_ones(v_t[i * HEAD_DIM:(i + 1) * HEAD_DIM]),
                                        e[:, i * gcols:(i + 1) * gcols]))

    def masked_logits(keys, mask, q_rows):
        return _dot(jnp.concatenate([keys, mask], axis=1), jnp.concatenate([q_rows, eye_ref[...]], axis=0))

    def with_ones(v_t):
        return jnp.concatenate([v_t, jnp.ones((SUBLANES, v_t.shape[1]), v_t.dtype)], axis=0)

    q_t = pad_q(_stack_heads_t(qa_ref, n_groups * rep))
    kc = kc_ref[0, 0]
    n_cmp = kc.shape[0]
    odd = lax.rem(n, 2)
    shift = Q_BLOCK // CMP_STRIDE
    start = pl.multiple_of(n_cmp - shift * (n + odd), 2 * SUBLANES)
    s_c = masked_logits(kc, cmask_ref[odd, pl.ds(start, n_cmp), :], q_t)
    q_b = pad_q(_stack_heads_t(qb_ref, n_groups * SWA_REP))
    k_b = jnp.concatenate([kbp_ref[0], kbc_ref[0]], axis=0)
    s_b = masked_logits(k_b, bmask_ref[jnp.minimum(n, bmask_ref.shape[0] - 1)], q_b)
    n_win_tiles = NSA_WINDOW // Q_BLOCK + 1
    tiles = [jnp.maximum(n - (n_win_tiles - 1) + i, 0) for i in range(n_win_tiles)]
    k_w = jnp.concatenate([kw_ref[0, pl.ds(pl.multiple_of(j * Q_BLOCK, Q_BLOCK), Q_BLOCK), :] for j in tiles], axis=0)
    s_w = masked_logits(k_w, wmask_ref[jnp.minimum(n, wmask_ref.shape[0] - 1)], q_t)
    s_d = masked_logits(ksd_ref[0], dmask_ref[...], q_t)

    m = jnp.max(s_c, axis=0, keepdims=True)
    e_c = jnp.exp2(s_c - m).astype(BF16)
    vct = vct_ref[0, 0]
    oc_imp = per_group(lambda i: _dot(
        jnp.concatenate([with_ones(vct[i * HEAD_DIM:(i + 1) * HEAD_DIM]), selmap_ref[...]], axis=0),
        e_c[:, i * gcols:(i + 1) * gcols]))
    den = oc_imp[HEAD_DIM:HEAD_DIM + 1]
    seen = m > 0.5 * NEG_INF
    oc_imp = jnp.where(seen, oc_imp * (1.0 / jnp.where(seen, den, 1.0)), 0.0)
    o_c = oc_imp[:HEAD_DIM]
    imp = oc_imp[HEAD_DIM + SUBLANES:]
    imp = per_group(lambda i: sum(imp[:, (i * rep + r) * Q_BLOCK:(i * rep + r + 1) * Q_BLOCK]
                                  for r in range(rep)))

    sink_row = jnp.concatenate([jnp.full((1, Q_BLOCK), sink_ref[h] * LOG2_E, F32)
                                for h in range(n_groups * SWA_REP)], axis=1)
    v_tiles_b = [vbp_ref[0, 0, 0], vbc_ref[0, 0, 0]]
    v_tiles_w = [vw_ref[0, 0, j] for j in tiles]
    v_tiles_d = [vsd_ref[0, 0, 0]]
    maxes, partial = {}, {"b": [], "w": [], "d": []}

    def col_max(name, s, extra=None):
        def run():
            m = jnp.max(s, axis=0, keepdims=True)
            maxes[name] = m if extra is None else jnp.maximum(m, extra)
        return run

    def pv_tile(name, s, v_tiles, i):
        def run():
            rows = slice(i * Q_BLOCK, (i + 1) * Q_BLOCK)
            partial[name].append(apply_values(v_tiles[i], jnp.exp2(s[rows] - maxes[name]).astype(BF16)))
        return run

    pieces = ([col_max("w", s_w), col_max("b", s_b, sink_row), col_max("d", s_d)]
              + [pv_tile("w", s_w, v_tiles_w, i) for i in range(len(v_tiles_w))]
              + [pv_tile("b", s_b, v_tiles_b, i) for i in range(len(v_tiles_b))]
              + [pv_tile("d", s_d, v_tiles_d, 0)])

    ns_pad = imp.shape[0]
    blk = lax.broadcasted_iota(jnp.int32, (ns_pad, n_groups * Q_BLOCK), 0)
    blk_f = blk.astype(F32)
    tq_row = _tile_lanes(tq_row, n_groups)
    cur = jnp.right_shift(tq_row, SEL_SHIFT)
    forced = (blk == 0) | (blk == cur) | (blk == cur - 1)
    valid = jnp.left_shift(blk, SEL_SHIFT) <= tq_row
    score = jnp.where(valid, imp, NEG_INF)
    score = jnp.where(forced | (blk >= n_sel_blocks), BELOW_NEG_INF, score)
    sel = forced
    for _ in range(min(SEL_TOPK, n_sel_blocks) - 3):
        if pieces:
            pieces.pop(0)()
        top = _col_tree(score, jnp.maximum, jnp.max)
        first_hit = _col_tree(jnp.where(score == top, blk_f, float(ns_pad)), jnp.minimum, jnp.min)
        pick = blk_f == first_hit
        sel = sel | pick
        score = jnp.where(pick, BELOW_NEG_INF, score)
    n_diag_blk = n * (Q_BLOCK // SEL_LEN)
    unselected = jnp.where(sel & (blk < n_diag_blk), 0.0, 1.0).astype(BF16)

    while pieces:
        pieces.pop(0)()

    def finish(name, sink=None):
        acc = sum(partial[name][1:], partial[name][0])
        den = acc[HEAD_DIM:HEAD_DIM + 1]
        if sink is not None:
            den = den + jnp.exp2(sink - maxes[name])
        return acc[:HEAD_DIM] * (1.0 / den)

    m_sc[...] = maxes["d"]
    acc_sc[...] = partial["d"][0]

    q_aug = jnp.concatenate([per_group(lambda i: _tile_lanes(unselected[:, i * Q_BLOCK:(i + 1) * Q_BLOCK], rep)),
                             q_t], axis=0)
    tk = s_sc.shape[1]
    n_tiles = ks_ref.shape[1] // tk

    def logits(kt):
        keys = pl.ds(pl.multiple_of(kt * tk, tk), tk)
        return _dot(jnp.concatenate([mask_ref[keys, :], ks_ref[0, keys, :]], axis=1), q_aug)

    def absorb(kt, s):
        m_old = m_sc[...]
        m_new = jnp.maximum(m_old, jnp.max(s, axis=0, keepdims=True))
        pv = apply_values(vbig_ref[0, kt], jnp.exp2(s - m_new).astype(BF16))
        acc_sc[...] = jnp.exp2(m_old - m_new) * acc_sc[...] + pv
        m_sc[...] = m_new

    s_sc[0] = logits(0)
    ob_ref[0] = _unstack_heads_t(finish("b", sink_row), n_groups * SWA_REP).astype(ob_ref.dtype)
    o_w = finish("w")

    def tile_pair(j, carry):
        s_sc[1] = logits(2 * j + 1)
        absorb(2 * j, s_sc[0])
        s_sc[0] = logits(jnp.minimum(2 * j + 2, n_tiles - 1))
        absorb(2 * j + 1, s_sc[1])
        return carry

    n_needed = lax.div(n * Q_BLOCK + tk - 1, tk)
    lax.fori_loop(0, lax.div(n_needed, 2), tile_pair, 0)

    @pl.when(lax.rem(n_needed, 2) == 1)
    def _():
        absorb(n_needed - 1, s_sc[0])

    acc = acc_sc[...]
    o_s = acc[:HEAD_DIM] * (1.0 / acc[HEAD_DIM:HEAD_DIM + 1])

    gates = _sigmoid(gl_ref[0, 0] + gb_ref[...])
    gate = lambda c: jnp.concatenate([gates[16 * i + 3 * r + c:16 * i + 3 * r + c + 1]
                                      for i in groups for r in range(rep)], axis=1)
    o = gate(0) * o_c + gate(1) * o_s + gate(2) * o_w
    oa_ref[0] = _unstack_heads_t(o, n_groups * rep).astype(oa_ref.dtype)


def _selection_map(t, nc_pad, ns_pad):
    nc = (t - CMP_LEN) // CMP_STRIDE + 1
    ns = t // SEL_LEN
    cs = np.arange(nc, dtype=np.int64)[:, None] * CMP_STRIDE
    ss = np.arange(ns, dtype=np.int64)[None, :] * SEL_LEN
    out = np.zeros((nc_pad, ns_pad), np.float32)
    out[:nc, :ns] = (cs < ss + SEL_LEN) & (cs + CMP_LEN > ss)
    return out


def _block_mask_cols(t, ns_pad):
    hit = np.arange(ns_pad, dtype=np.int64)[None, :] == (np.arange(t, dtype=np.int64) // SEL_LEN)[:, None]
    return np.where(hit, NEG_INF, 0.0).astype(np.float32)


def _band_masks(n_rows, window):
    n_front = n_rows // Q_BLOCK - 1
    r = np.arange(n_rows)[None, :, None]
    tok = np.arange(Q_BLOCK)[None, None, :]
    v = np.arange(n_front + 1)[:, None, None]
    dist = n_front * Q_BLOCK + tok - r
    ok = (dist >= 0) & (dist < window) & (r >= (n_front - v) * Q_BLOCK)
    return np.where(ok, 0.0, NEG_INF).astype(np.float32)


def _compressed_masks(n_cmp):
    shift = Q_BLOCK // CMP_STRIDE
    u = np.arange(2 * n_cmp)[None, :, None] + shift * np.arange(2)[:, None, None]
    tok = np.arange(Q_BLOCK)[None, None, :]
    ok = CMP_STRIDE * (u - n_cmp) + CMP_LEN - 1 <= tok
    return np.where(ok, 0.0, NEG_INF).astype(np.float32)


def _attention(q, k_all, v_big, v_small, gate_t, gate_bias, cmp_rows, cmp_cols, sel_map_t, mask_cols, sinks,
               n_sel_blocks):
    b, t = k_all.shape[:2]
    gd = k_all.shape[2] // 3
    g = gd // HEAD_DIM
    tk = v_big.shape[3]
    width = g * NSA_REP * HEAD_DIM
    cols = g * NSA_REP * Q_BLOCK
    gate_cols = gate_t.shape[3] // Q_BLOCK
    prev = lambda n: jnp.maximum(n - 1, 0)
    cur = lambda n: n
    k_tile = lambda blk, at: pl.BlockSpec((1, Q_BLOCK, gd), lambda i, n: (i, at(n), blk))
    k_all_rows = lambda blk: pl.BlockSpec((1, t, gd), lambda i, n: (i, 0, blk))
    v_tile = lambda a, at: pl.BlockSpec((1, 1, 1, gd, Q_BLOCK), lambda i, n: (a, i, at(n), 0, 0))
    out_spec = pl.BlockSpec((1, Q_BLOCK, width), lambda i, n: (i, n, 0))
    out_shape = jax.ShapeDtypeStruct((b, t, width), BF16)
    tri = np.where(np.arange(Q_BLOCK)[:, None] <= np.arange(Q_BLOCK)[None, :], 0.0, NEG_INF)
    consts = [jnp.asarray(c, BF16) for c in (
        np.tile(np.eye(Q_BLOCK, dtype=np.float32), (1, g * NSA_REP)),
        _compressed_masks(cmp_rows.shape[2]), tri,
        _band_masks(NSA_WINDOW + Q_BLOCK, NSA_WINDOW), _band_masks(SWA_WINDOW + Q_BLOCK, SWA_WINDOW))]
    return pl.pallas_call(
        functools.partial(_attn_kernel, n_sel_blocks),
        grid=(b, t // Q_BLOCK),
        in_specs=[
            pl.BlockSpec(memory_space=pltpu.SMEM),
            pl.BlockSpec((1, 1, width, Q_BLOCK), lambda i, n: (i, n // gate_cols, 0, n % gate_cols)),
            pl.BlockSpec((1, 1, width, Q_BLOCK), lambda i, n: (i, n // gate_cols, 1, n % gate_cols)),
            pl.BlockSpec((1, 1, 16 * g, Q_BLOCK), lambda i, n: (i, n // gate_cols, 0, n % gate_cols)),
            _resident(gate_bias.shape),
            pl.BlockSpec((1, 1) + cmp_rows.shape[2:], lambda i, n: (0, i, 0, 0)),
            pl.BlockSpec((1, 1) + cmp_cols.shape[2:], lambda i, n: (1, i, 0, 0)),
            _resident(sel_map_t.shape),
            k_tile(0, cur), k_all_rows(0), _resident(mask_cols.shape),
            v_tile(0, cur), pl.BlockSpec((1,) + v_big.shape[1:], lambda i, n: (i, 0, 0, 0)),
            k_all_rows(1), pl.BlockSpec((1, 1) + v_small.shape[2:], lambda i, n: (1, i, 0, 0, 0)),
            k_tile(2, prev), k_tile(2, cur), v_tile(2, prev), v_tile(2, cur),
        ] + [_resident(c.shape) for c in consts],
        out_specs=[out_spec, out_spec],
        out_shape=[out_shape, out_shape],
        scratch_shapes=[
            pltpu.VMEM((1, cols), F32),
            pltpu.VMEM((HEAD_DIM + SUBLANES, cols), F32),
            pltpu.VMEM((2, tk, cols), F32),
        ],
        compiler_params=_cparams(2),
        name="nsa_swa_attention",
    )(sinks, q, q, gate_t, gate_bias, cmp_rows, cmp_cols, sel_map_t, k_all, k_all, mask_cols, v_small, v_big,
      k_all, v_small, k_all, k_all, v_small, v_small, *consts)


def _conv_silu_kernel(n_q_cols, x_ref, prev_ref, shift_ref, w_ref, b_ref, o_ref, buf):
    i = pl.program_id(1)
    rows = x_ref.shape[1]
    halo = prev_ref.shape[1]
    taps = CONV_WIDTH - 1
    buf[:halo, :] = jnp.where(i > 0, prev_ref[0], jnp.zeros_like(prev_ref[0]))
    buf[halo:, :] = x_ref[0]
    col = lax.broadcasted_iota(jnp.int32, (Q_BLOCK, x_ref.shape[2]), 1)
    for r in range(rows // Q_BLOCK):
        win = buf[r * Q_BLOCK:(r + 1) * Q_BLOCK + halo, :]
        back = _dot(shift_ref[...], win)
        y = b_ref[...] + w_ref[taps:taps + 1, :] * win[halo:].astype(F32)
        for j in range(taps):
            y = y + w_ref[j:j + 1, :] * back[j * Q_BLOCK:(j + 1) * Q_BLOCK]
        y = _silu(y)
        y = jnp.where(col >= n_q_cols, y * (MLSTM_QK_DIM ** -0.5), y)
        o_ref[0, r * Q_BLOCK:(r + 1) * Q_BLOCK, :] = y.astype(o_ref.dtype)


def _conv_silu(x, w, bias):
    b, t, c = x.shape
    halo = 2 * SUBLANES
    ratio = ROW_TILE // halo
    taps = CONV_WIDTH - 1
    shift = np.zeros((taps * Q_BLOCK, Q_BLOCK + halo), np.float32)
    for j in range(taps):
        shift[j * Q_BLOCK + np.arange(Q_BLOCK), np.arange(Q_BLOCK) + halo - taps + j] = 1.0
    return pl.pallas_call(
        functools.partial(_conv_silu_kernel, c // 2),
        grid=(b, t // ROW_TILE),
        in_specs=[
            pl.BlockSpec((1, ROW_TILE, c), lambda i, j: (i, j, 0)),
            pl.BlockSpec((1, halo, c), lambda i, j: (i, jnp.maximum(j * ratio - 1, 0), 0)),
            _resident(shift.shape),
            _resident(w.shape),
            _resident((1, c)),
        ],
        out_specs=pl.BlockSpec((1, ROW_TILE, c), lambda i, j: (i, j, 0)),
        out_shape=jax.ShapeDtypeStruct((b, t, c), BF16),
        scratch_shapes=[pltpu.VMEM((ROW_TILE + halo, c), BF16)],
        compiler_params=_cparams(2),
        name="conv_silu",
    )(x, x, jnp.asarray(shift, BF16), w, bias.reshape(1, c))


def _mlstm_kernel(bias_ref, q_ref, k_ref, v_ref, g_ref, o_ref, state, m_state):
    c = pl.program_id(1)
    L = SCAN_CHUNK
    dqk, dv = MLSTM_QK_DIM, MLSTM_V_DIM

    @pl.when(c == 0)
    def _():
        state[...] = jnp.zeros(state.shape, F32)
        m_state[...] = jnp.zeros(m_state.shape, F32)

    t_idx = lax.broadcasted_iota(jnp.int32, (L, L), 0)
    s_idx = lax.broadcasted_iota(jnp.int32, (L, L), 1)
    causal = s_idx <= t_idx
    eye = s_idx == t_idx
    ones_col = (lax.broadcasted_iota(jnp.int32, (L, LANES), 1) == 0).astype(BF16)

    def to_col(row):
        return jnp.sum(jnp.where(eye, row, 0.0), axis=1, keepdims=True)

    heads = range(MLSTM_HEADS)
    for j in range(SCAN_ROWS // SCAN_CHUNK):
        rows = slice(j * L, (j + 1) * L)
        qc = [q_ref[0, rows, h * dqk:(h + 1) * dqk] for h in heads]
        kc = [k_ref[0, rows, h * dqk:(h + 1) * dqk] for h in heads]
        va = [jnp.concatenate([v_ref[0, rows, h * dv:(h + 1) * dv], ones_col], axis=1) for h in heads]
        qk = [lax.dot_general(qc[h], kc[h], (((1,), (1,)), ((), ())), preferred_element_type=F32) for h in heads]
        i_row, b_col, b_row = [], [], []
        for h in heads:
            i_row.append(g_ref[0, 0, h:h + 1, rows] + bias_ref[0, h])
            f_pre = g_ref[0, 0, MLSTM_HEADS + h:MLSTM_HEADS + h + 1, rows] + bias_ref[1, h]
            f_row = jnp.minimum(f_pre, 0.0) - jnp.log(1.0 + jnp.exp(-jnp.abs(f_pre)))
            f_col = to_col(f_row)
            b_col.append(jnp.sum(jnp.where(causal, f_row, 0.0), axis=1, keepdims=True))
            b_row.append(jnp.sum(jnp.where(t_idx <= s_idx, f_col, 0.0), axis=0, keepdims=True))
        m_old = [m_state[h:h + 1, 0:1] for h in heads]
        st = [state[h] for h in heads]
        mt, w_inter, sm = [], [], []
        for h in heads:
            a = b_col[h] + m_old[h]
            d = jnp.where(causal, b_col[h] - b_row[h] + i_row[h], -jnp.inf)
            mt.append(jnp.maximum(a, jnp.max(d, axis=1, keepdims=True)))
            w_inter.append(jnp.exp(a - mt[h]))
            sm.append((qk[h] * jnp.exp(d - mt[h])).astype(BF16))
        for h in heads:
            nd = w_inter[h] * _dot(qc[h], st[h].astype(BF16)) + _dot(sm[h], va[h])
            hc = nd[:, :dv] * (1.0 / jnp.maximum(jnp.abs(nd[:, dv:dv + 1]), jnp.exp(-mt[h])))
            o_ref[0, rows, h * dv:(h + 1) * dv] = hc.astype(o_ref.dtype)
        for h in heads:
            b_last = b_row[h][:, L - 1:L]
            wl = b_last - b_row[h] + i_row[h]
            m_new = jnp.maximum(b_last + m_old[h], jnp.max(wl, axis=1, keepdims=True))
            decay = jnp.exp(b_last + m_old[h] - m_new)
            w_col = to_col(jnp.exp(wl - m_new))
            kw = (kc[h].astype(F32) * w_col).astype(BF16)
            upd = lax.dot_general(kw, va[h], (((0,), (0,)), ((), ())), preferred_element_type=F32)
            state[h] = decay * st[h] + upd
            m_state[h:h + 1, 0:1] = m_new


def _mlstm(qk, v, gates_t, gate_bias):
    b, t = qk.shape[:2]
    nh = MLSTM_HEADS
    rows = SCAN_ROWS
    assert gates_t.shape[3] == rows
    return pl.pallas_call(
        _mlstm_kernel,
        grid=(b, t // rows),
        in_specs=[
            pl.BlockSpec(memory_space=pltpu.SMEM),
            pl.BlockSpec((1, rows, nh * MLSTM_QK_DIM), lambda i, c: (i, c, 0)),
            pl.BlockSpec((1, rows, nh * MLSTM_QK_DIM), lambda i, c: (i, c, 1)),
            pl.BlockSpec((1, rows, nh * MLSTM_V_DIM), lambda i, c: (i, c, 0)),
            pl.BlockSpec((1, 1) + gates_t.shape[2:], lambda i, c: (i, c, 0, 0)),
        ],
        out_specs=pl.BlockSpec((1, rows, nh * MLSTM_V_DIM), lambda i, c: (i, c, 0)),
        out_shape=jax.ShapeDtypeStruct((b, t, nh * MLSTM_V_DIM), BF16),
        scratch_shapes=[
            pltpu.VMEM((nh, MLSTM_QK_DIM, MLSTM_V_DIM + LANES), F32),
            pltpu.VMEM((SUBLANES, LANES), F32),
        ],
        compiler_params=_cparams(2),
        name="mlstm_scan",
    )(gate_bias, qk, qk, v, gates_t)


def _block_diag(w, n):
    zero = jnp.zeros_like(w)
    return jnp.concatenate([jnp.concatenate([w if i == j else zero for i in range(n)], axis=-1)
                            for j in range(n)], axis=-2)


def _nsa_swa_layer(h, norm_g, w_in, gate_bias, pos_k, pos_v, ck_w1, ck_w2, cv_w1, cv_w2, sinks, w_out):
    b, t, d = h.shape
    m = b * t
    ng = NSA_KV_HEADS
    gk = ng * HEAD_DIM
    nq_a, nq_b = NSA_HEADS * HEAD_DIM, SWA_HEADS * HEAD_DIM
    ngate = NSA_HEADS * 3
    o_kc, o_vc, o_ks, o_vs, o_kw, o_vw = (nq_a + i * gk for i in range(6))
    o_g = nq_a + 6 * gk
    o_qb = o_g + ngate
    o_kb, o_vb = o_qb + nq_b, o_qb + nq_b + gk
    cols = lambda o, n: w_in[:, o:o + n]
    q_scale = HEAD_DIM ** -0.5 * LOG2_E
    gate_pad = jnp.zeros((d, 16 - NSA_REP * 3), w_in.dtype)
    w_gate = jnp.concatenate([p for i in range(ng) for p in (cols(o_g + i * NSA_REP * 3, NSA_REP * 3), gate_pad)]
                             + [jnp.zeros((d, LANES - 16 * ng), w_in.dtype)], axis=1)
    w_all = jnp.concatenate([cols(0, nq_a) * q_scale, cols(o_qb, nq_b) * q_scale,
                             cols(o_ks, gk), cols(o_kw, gk), cols(o_kb, gk),
                             cols(o_vs, gk), cols(o_vw, gk), cols(o_vb, gk),
                             cols(o_kc, gk), cols(o_vc, gk), w_gate], axis=1).astype(BF16)
    q_all, k_all, cmp_in, v_big, v_small, gate_t = _proj0(
        h.reshape(m, d), norm_g, w_all, [nq_a + nq_b, 3 * gk, 3 * gk, 2 * gk + LANES], 2 * gk, 16 * ng)
    tk = ROW_TILE
    q_all = q_all.reshape(b, t // ROW_TILE, nq_a + nq_b, ROW_TILE)
    k_all = k_all.reshape(b, t, 3 * gk)
    v_big = v_big.reshape(b, t // tk, gk, tk)
    v_small = v_small.reshape(3, b, t // Q_BLOCK, gk, Q_BLOCK)
    gate_t = gate_t.reshape(b, t // ROW_TILE, 16 * ng, ROW_TILE)
    gb = jnp.pad(gate_bias.reshape(ng, NSA_REP * 3), ((0, 0), (0, 16 - NSA_REP * 3))).reshape(ng * 16, 1)

    tile2 = lambda p: jnp.tile(p[:, None, :], (1, 1, ng))
    pos = jnp.stack([tile2(pos_k), tile2(pos_v)])
    per_pos = lambda w: w.reshape(CMP_LEN, HEAD_DIM, HEAD_DIM)
    w1 = _block_diag(jnp.stack([per_pos(ck_w1), per_pos(cv_w1)]), ng).astype(BF16)
    w2 = _block_diag(jnp.stack([ck_w2, cv_w2]), ng).astype(BF16)
    cmp_rows, cmp_cols = _compress(cmp_in.reshape(b, t, 2 * gk), pos, w1, w2)

    n_sel = t // SEL_LEN
    ns_pad = -(-n_sel // LANES) * LANES
    sel_map_t = jnp.asarray(_selection_map(t, t // CMP_STRIDE, ns_pad).T, BF16)
    mask_cols = jnp.asarray(_block_mask_cols(t, ns_pad), BF16)
    o_a, o_b = _attention(q_all, k_all, v_big, v_small, gate_t, gb, cmp_rows, cmp_cols, sel_map_t, mask_cols,
                          sinks, n_sel)

    w_o = w_out.astype(BF16)
    return [o_a.reshape(m, nq_a), o_b.reshape(m, nq_b)], [w_o[:nq_a], w_o[nq_a:]], None


def _mlstm_layer(h, norm_g, w_in, conv_w, conv_b, igate_bias, fgate_bias, w_out):
    b, t, d = h.shape
    m = b * t
    nqk = 2 * MLSTM_HEADS * MLSTM_QK_DIM
    nv = MLSTM_HEADS * MLSTM_V_DIM
    w = jnp.concatenate([w_in, jnp.zeros((d, LANES - 2 * MLSTM_HEADS), w_in.dtype)], axis=1).astype(BF16)
    qk_raw, v, og, gates_t = _norm_proj(h.reshape(m, d), norm_g, w, [nqk, nv, nv, LANES], [BF16, BF16, BF16],
                                        2 * MLSTM_HEADS)
    qk = _conv_silu(qk_raw.reshape(b, t, nqk), conv_w, conv_b)
    gates_t = gates_t.reshape(b, t // ROW_TILE, 2 * MLSTM_HEADS, ROW_TILE)
    hh = _mlstm(qk, v.reshape(b, t, nv), gates_t, jnp.stack([igate_bias, fgate_bias]))
    return [hh.reshape(m, nv)], [w_out.astype(BF16)], og


def kernel(x, norm_mix, norm_ffn, ffn_w_gate, ffn_w_up, ffn_w_down, ab_w_in, ab_gate_bias, nsa_pos_k, nsa_pos_v, nsa_cmp_k_w1, nsa_cmp_k_w2, nsa_cmp_v_w1, nsa_cmp_v_w2, swa_sinks, ab_w_out, c_w_in, c_conv_w, c_conv_b, c_igate_bias, c_fgate_bias, c_w_out, final_norm):
    depth = norm_mix.shape[0]
    b, t, d = x.shape
    h = x
    ffn_w = [w.astype(BF16) for w in (ffn_w_gate, ffn_w_up, ffn_w_down)]
    for layer in range(depth):
        j = layer // 2
        if layer % 2 == 0:
            mixed = _nsa_swa_layer(h, norm_mix[layer], ab_w_in[j], ab_gate_bias[j], nsa_pos_k[j], nsa_pos_v[j],
                               nsa_cmp_k_w1[j], nsa_cmp_k_w2[j], nsa_cmp_v_w1[j], nsa_cmp_v_w2[j],
                               swa_sinks[j], ab_w_out[j])
        else:
            mixed = _mlstm_layer(h, norm_mix[layer], c_w_in[j], c_conv_w[j], c_conv_b[j], c_igate_bias[j],
                             c_fgate_bias[j], c_w_out[j])
        last = layer == depth - 1
        h = _mix_ffn(h.reshape(b * t, d), *mixed, norm_ffn[layer], layer, *ffn_w,
                     final_norm if last else None).reshape(b, t, d)
    if depth == 0:
        raise ValueError("depth must be positive")
    return h
```

```python
import functools

import numpy as np
import jax
import jax.numpy as jnp
from jax import lax
from jax.experimental import pallas as pl
from jax.experimental.pallas import tpu as pltpu

HEAD_DIM = 64
NSA_HEADS = 8
NSA_KV_HEADS = 2
NSA_REP = NSA_HEADS // NSA_KV_HEADS
CMP_LEN = 32
CMP_STRIDE = 16
SEL_LEN = 64
SEL_TOPK = 16
NSA_WINDOW = 512
SWA_HEADS = 8
SWA_KV_HEADS = 2
SWA_REP = SWA_HEADS // SWA_KV_HEADS
SWA_WINDOW = 128
Q_BLOCK = 128
MLSTM_HEADS = 4
MLSTM_QK_DIM = 128
MLSTM_V_DIM = 256
CONV_WIDTH = 4
RMS_EPS = 1e-6
NEG_INF = -1e30
BELOW_NEG_INF = -3e38
LOG2_E = 1.4426950408889634
SEL_SHIFT = SEL_LEN.bit_length() - 1
assert 1 << SEL_SHIFT == SEL_LEN

LANES = 128
SUBLANES = 8
VMEM_LIMIT_BYTES = 56 * 1024 * 1024

ROW_TILE = 512
GATE_ROWS = 2 * SUBLANES
SCAN_CHUNK = 512
SCAN_ROWS = 512

F32 = jnp.float32
BF16 = jnp.bfloat16


def _cparams(n_axes):
    return pltpu.CompilerParams(
        dimension_semantics=("arbitrary",) * n_axes,
        vmem_limit_bytes=VMEM_LIMIT_BYTES,
    )


def _resident(shape):
    nd = len(shape)
    return pl.BlockSpec(shape, lambda *_: (0,) * nd)


def _dot(a, b):
    return jnp.dot(a, b, preferred_element_type=F32)


def _sigmoid(x):
    return 1.0 / (1.0 + jnp.exp(-x))


def _silu(x):
    return x * _sigmoid(x)


def _rmsnorm(x, g):
    y = x * lax.rsqrt(jnp.mean(x * x, axis=-1, keepdims=True) + RMS_EPS)
    return y * g


def _col_ranges(widths):
    ends = np.cumsum(widths).tolist()
    return list(zip([0] + ends[:-1], ends))


def _norm_proj_kernel(widths, x_ref, g_ref, w_ref, *o_refs):
    ranges = _col_ranges(widths)
    xn = _rmsnorm(x_ref[...], g_ref[...]).astype(BF16)
    for (lo, hi), o_ref in zip(ranges[:-2], o_refs[:-2]):
        o_ref[...] = _dot(xn, w_ref[:, lo:hi]).astype(o_ref.dtype)
    (lo, mid), (_, hi) = ranges[-2:]
    both = _dot(xn, w_ref[:, lo:hi])
    o_refs[-2][...] = both[:, :mid - lo].astype(o_refs[-2].dtype)
    o_refs[-1][0] = both[:, mid - lo:].T[:o_refs[-1].shape[1]]


def _norm_proj(x, g, w, widths, out_dtypes, t_rows):
    m, d = x.shape
    steps = m // ROW_TILE
    row = lambda n: pl.BlockSpec((ROW_TILE, n), lambda i: (i, 0))
    return pl.pallas_call(
        functools.partial(_norm_proj_kernel, tuple(widths)),
        grid=(steps,),
        in_specs=[row(d), _resident((1, d)), _resident(w.shape)],
        out_specs=[row(n) for n in widths[:-1]] + [pl.BlockSpec((1, t_rows, ROW_TILE), lambda i: (i, 0, 0))],
        out_shape=[jax.ShapeDtypeStruct((m, n), dt) for n, dt in zip(widths[:-1], out_dtypes)]
        + [jax.ShapeDtypeStruct((steps, t_rows, ROW_TILE), F32)],
        compiler_params=_cparams(1),
        name="norm_proj",
    )(x, g.reshape(1, d), w)


def _proj0_kernel(widths, x_ref, g_ref, w_ref, q_ref, k_ref, c_ref, vbig_ref, vsmall_ref, gt_ref):
    (q0, q1), (k0, k1), (v0, v1), (c0, c1) = _col_ranges(widths)
    xn = _rmsnorm(x_ref[...], g_ref[...]).astype(BF16)
    q = _dot(xn, w_ref[:, q0:q1])
    half = (q1 - q0) // 2
    q_ref[0, :half] = q[:, :half].T.astype(q_ref.dtype)
    q_ref[0, half:] = q[:, half:].T.astype(q_ref.dtype)
    k_ref[...] = _dot(xn, w_ref[:, k0:k1]).astype(k_ref.dtype)
    cg = _dot(xn, w_ref[:, c0:c1])
    c_ref[...] = cg[:, :c_ref.shape[1]]
    gt_ref[0] = cg[:, c_ref.shape[1]:].T[:gt_ref.shape[1]]
    v = _dot(xn, w_ref[:, v0:v1])
    width = v.shape[1] // 3
    for a in range(3):
        v_t = v[:, a * width:(a + 1) * width].T.astype(BF16)
        if a == 0:
            vbig_ref[0] = v_t
        for j in range(x_ref.shape[0] // Q_BLOCK):
            vsmall_ref[a, j] = v_t[:, j * Q_BLOCK:(j + 1) * Q_BLOCK]


def _proj0(x, g, w, widths, n_cmp_cols, gate_rows):
    m, d = x.shape
    steps = m // ROW_TILE
    tiles = ROW_TILE // Q_BLOCK
    nq, nk, nv, _ = widths
    gd = nv // 3
    row = lambda n: pl.BlockSpec((ROW_TILE, n), lambda i: (i, 0))
    return pl.pallas_call(
        functools.partial(_proj0_kernel, tuple(widths)),
        grid=(steps,),
        in_specs=[row(d), _resident((1, d)), _resident(w.shape)],
        out_specs=[
            pl.BlockSpec((1, nq, ROW_TILE), lambda i: (i, 0, 0)), row(nk), row(n_cmp_cols),
            pl.BlockSpec((1, gd, ROW_TILE), lambda i: (i, 0, 0)),
            pl.BlockSpec((3, tiles, gd, Q_BLOCK), lambda i: (0, i, 0, 0)),
            pl.BlockSpec((1, gate_rows, ROW_TILE), lambda i: (i, 0, 0)),
        ],
        out_shape=[
            jax.ShapeDtypeStruct((steps, nq, ROW_TILE), BF16),
            jax.ShapeDtypeStruct((m, nk), BF16),
            jax.ShapeDtypeStruct((m, n_cmp_cols), F32),
            jax.ShapeDtypeStruct((steps, gd, ROW_TILE), BF16),
            jax.ShapeDtypeStruct((3, steps * tiles, gd, Q_BLOCK), BF16),
            jax.ShapeDtypeStruct((steps, gate_rows, ROW_TILE), F32),
        ],
        compiler_params=_cparams(1),
        name="norm_proj_attn",
    )(x, g.reshape(1, d), w)


def _mix_ffn_kernel(n_in, gated, final, res_ref, *refs):
    a_refs, w_refs = refs[:n_in], refs[n_in:2 * n_in]
    rest = list(refs[2 * n_in:])
    o_ref = rest.pop()
    gate_ref = rest.pop(0) if gated else None
    g_ref, wg_ref, wu_ref, wd_ref = rest[:4]
    h = res_ref[...]
    for idx, (a_ref, w_ref) in enumerate(zip(a_refs, w_refs)):
        a = a_ref[...]
        if gated and idx == 0:
            a = (a.astype(F32) * _sigmoid(gate_ref[...].astype(F32))).astype(BF16)
        h = h + _dot(a, w_ref[...])
    xn = _rmsnorm(h, g_ref[...]).astype(BF16)
    act = (_silu(_dot(xn, wg_ref[0])) * _dot(xn, wu_ref[0])).astype(BF16)
    out = h + _dot(act, wd_ref[0])
    if final:
        out = _rmsnorm(out, rest[4][...])
    o_ref[...] = out


def _mix_ffn(res, a_list, w_list, gate, g, layer, wg, wu, wd, final_g=None):
    m, d = res.shape
    row = lambda w: pl.BlockSpec((ROW_TILE, w), lambda i: (i, 0))
    once = lambda a: pl.BlockSpec(a.shape, lambda *_: (0,) * a.ndim, pipeline_mode=pl.Buffered(1))
    in_specs = [row(d)] + [row(a.shape[1]) for a in a_list] + [once(w) for w in w_list]
    args = [res, *a_list, *w_list]
    if gate is not None:
        in_specs.append(row(gate.shape[1]))
        args.append(gate)
    slab = lambda a: pl.BlockSpec((1,) + a.shape[1:], lambda *_: (layer, 0, 0), pipeline_mode=pl.Buffered(1))
    tail = [final_g.reshape(1, d)] if final_g is not None else []
    return pl.pallas_call(
        functools.partial(_mix_ffn_kernel, len(a_list), gate is not None, final_g is not None),
        grid=(m // ROW_TILE,),
        in_specs=in_specs + [once(g.reshape(1, d)), slab(wg), slab(wu), slab(wd)] + [once(c) for c in tail],
        out_specs=row(d),
        out_shape=jax.ShapeDtypeStruct((m, d), F32),
        compiler_params=_cparams(1),
        name="mix_out_ffn",
    )(*args, g.reshape(1, d), wg, wu, wd, *tail)


def _compress_kernel(x_ref, p_ref, w1_ref, w2_ref, rows_ref, cols_ref):
    n = x_ref.shape[1] // CMP_STRIDE
    a = jnp.zeros((n, x_ref.shape[2]), F32)
    b = jnp.zeros((n, x_ref.shape[2]), F32)
    for r in range(CMP_STRIDE):
        x = x_ref[0, pl.ds(r, n, stride=CMP_STRIDE), :]
        a = a + _dot((x + p_ref[0, r]).astype(BF16), w1_ref[0, r])
        b = b + _dot((x + p_ref[0, CMP_STRIDE + r]).astype(BF16), w1_ref[0, CMP_STRIDE + r])
    h = a + pltpu.roll(b, n - 1, 0)
    out = _dot(_silu(h).astype(BF16), w2_ref[0])
    live = lax.broadcasted_iota(jnp.int32, out.shape, 0) < n - 1
    out = jnp.where(live, out, 0.0)
    rows_ref[0, 0] = out.astype(rows_ref.dtype)
    cols_ref[0, 0] = out.T.astype(cols_ref.dtype)


def _compress(x, pos, w1, w2):
    b, t, two_gd = x.shape
    gd = two_gd // 2
    n = t // CMP_STRIDE
    return pl.pallas_call(
        _compress_kernel,
        grid=(2, b),
        in_specs=[
            pl.BlockSpec((1, t, gd), lambda s, i: (i, 0, s)),
            pl.BlockSpec((1,) + pos.shape[1:], lambda s, i: (s, 0, 0, 0)),
            pl.BlockSpec((1,) + w1.shape[1:], lambda s, i: (s, 0, 0, 0)),
            pl.BlockSpec((1,) + w2.shape[1:], lambda s, i: (s, 0, 0)),
        ],
        out_specs=[pl.BlockSpec((1, 1, n, gd), lambda s, i: (s, i, 0, 0)),
                   pl.BlockSpec((1, 1, gd, n), lambda s, i: (s, i, 0, 0))],
        out_shape=[jax.ShapeDtypeStruct((2, b, n, gd), BF16), jax.ShapeDtypeStruct((2, b, gd, n), BF16)],
        compiler_params=_cparams(2),
        name="compress_kv",
    )(x, pos, w1, w2)


def _stack_heads_t(q_ref, rep):
    qt = q_ref[0, 0]
    return jnp.concatenate([qt[r * HEAD_DIM:(r + 1) * HEAD_DIM] for r in range(rep)], axis=1)


def _unstack_heads_t(o_t, rep):
    tq = o_t.shape[1] // rep
    return jnp.concatenate([o_t[:, r * tq:(r + 1) * tq] for r in range(rep)], axis=0).T


def _tile_lanes(x, rep):
    return jnp.concatenate([x] * rep, axis=1)


def _col_tree(x, pair, reduce):
    while x.shape[0] > SUBLANES:
        half = x.shape[0] // 2
        x = pair(x[:half], x[half:])
    return reduce(x, axis=0, keepdims=True)


def _attn_kernel(n_sel_blocks, sink_ref, qa_ref, qb_ref, gl_ref, gb_ref, kc_ref, vct_ref, selmap_ref,
                 ksd_ref, ks_ref, mask_ref, vsd_ref, vbig_ref, kw_ref, vw_ref, kbp_ref, kbc_ref, vbp_ref, vbc_ref,
                 eye_ref, cmask_ref, dmask_ref, wmask_ref, bmask_ref, oa_ref, ob_ref, m_sc, acc_sc, s_sc):
    n = pl.program_id(1)
    rep = NSA_REP
    n_groups = ksd_ref.shape[2] // HEAD_DIM
    gcols = rep * Q_BLOCK
    groups = range(n_groups)
    tq_row = n * Q_BLOCK + lax.broadcasted_iota(jnp.int32, (1, Q_BLOCK), 1)

    def pad_q(q_t):
        zero = jnp.zeros((HEAD_DIM, gcols), q_t.dtype)
        return jnp.concatenate([jnp.concatenate([q_t[:, i * gcols:(i + 1) * gcols] if i == j else zero
                                                 for i in groups], axis=1) for j in groups], axis=0)

    def per_group(fn):
        return jnp.concatenate([fn(i) for i in groups], axis=1)

    def apply_values(v_t, e):
        return per_group(lambda i: _dot(with_ones(v_t[i * HEAD_DIM:(i + 1) * HEAD_DIM]),
                                        e[:, i * gcols:(i + 1) * gcols]))

    def masked_logits(keys, mask, q_rows):
        return _dot(jnp.concatenate([keys, mask], axis=1), jnp.concatenate([q_rows, eye_ref[...]], axis=0))

    def with_ones(v_t):
        return jnp.concatenate([v_t, jnp.ones((SUBLANES, v_t.shape[1]), v_t.dtype)], axis=0)

    q_t = pad_q(_stack_heads_t(qa_ref, n_groups * rep))
    kc = kc_ref[0, 0]
    n_cmp = kc.shape[0]
    odd = lax.rem(n, 2)
    shift = Q_BLOCK // CMP_STRIDE
    start = pl.multiple_of(n_cmp - shift * (n + odd), 2 * SUBLANES)
    s_c = masked_logits(kc, cmask_ref[odd, pl.ds(start, n_cmp), :], q_t)
    q_b = pad_q(_stack_heads_t(qb_ref, n_groups * SWA_REP))
    k_b = jnp.concatenate([kbp_ref[0], kbc_ref[0]], axis=0)
    s_b = masked_logits(k_b, bmask_ref[jnp.minimum(n, bmask_ref.shape[0] - 1)], q_b)
    n_win_tiles = NSA_WINDOW // Q_BLOCK + 1
    tiles = [jnp.maximum(n - (n_win_tiles - 1) + i, 0) for i in range(n_win_tiles)]
    k_w = jnp.concatenate([kw_ref[0, pl.ds(pl.multiple_of(j * Q_BLOCK, Q_BLOCK), Q_BLOCK), :] for j in tiles], axis=0)
    s_w = masked_logits(k_w, wmask_ref[jnp.minimum(n, wmask_ref.shape[0] - 1)], q_t)
    s_d = masked_logits(ksd_ref[0], dmask_ref[...], q_t)

    m = jnp.max(s_c, axis=0, keepdims=True)
    e_c = jnp.exp2(s_c - m).astype(BF16)
    vct = vct_ref[0, 0]
    oc_imp = per_group(lambda i: _dot(
        jnp.concatenate([with_ones(vct[i * HEAD_DIM:(i + 1) * HEAD_DIM]), selmap_ref[...]], axis=0),
        e_c[:, i * gcols:(i + 1) * gcols]))
    den = oc_imp[HEAD_DIM:HEAD_DIM + 1]
    seen = m > 0.5 * NEG_INF
    oc_imp = jnp.where(seen, oc_imp * (1.0 / jnp.where(seen, den, 1.0)), 0.0)
    o_c = oc_imp[:HEAD_DIM]
    imp = oc_imp[HEAD_DIM + SUBLANES:]
    imp = per_group(lambda i: sum(imp[:, (i * rep + r) * Q_BLOCK:(i * rep + r + 1) * Q_BLOCK]
                                  for r in range(rep)))

    sink_row = jnp.concatenate([jnp.full((1, Q_BLOCK), sink_ref[h] * LOG2_E, F32)
                                for h in range(n_groups * SWA_REP)], axis=1)
    v_tiles_b = [vbp_ref[0, 0, 0], vbc_ref[0, 0, 0]]
    v_tiles_w = [vw_ref[0, 0, j] for j in tiles]
    v_tiles_d = [vsd_ref[0, 0, 0]]
    maxes, partial = {}, {"b": [], "w": [], "d": []}

    def col_max(name, s, extra=None):
        def run():
            m = jnp.max(s, axis=0, keepdims=True)
            maxes[name] = m if extra is None else jnp.maximum(m, extra)
        return run

    def pv_tile(name, s, v_tiles, i):
        def run():
            rows = slice(i * Q_BLOCK, (i + 1) * Q_BLOCK)
            partial[name].append(apply_values(v_tiles[i], jnp.exp2(s[rows] - maxes[name]).astype(BF16)))
        return run

    pieces = ([col_max("w", s_w), col_max("b", s_b, sink_row), col_max("d", s_d)]
              + [pv_tile("w", s_w, v_tiles_w, i) for i in range(len(v_tiles_w))]
              + [pv_tile("b", s_b, v_tiles_b, i) for i in range(len(v_tiles_b))]
              + [pv_tile("d", s_d, v_tiles_d, 0)])

    ns_pad = imp.shape[0]
    blk = lax.broadcasted_iota(jnp.int32, (ns_pad, n_groups * Q_BLOCK), 0)
    blk_f = blk.astype(F32)
    tq_row = _tile_lanes(tq_row, n_groups)
    cur = jnp.right_shift(tq_row, SEL_SHIFT)
    forced = (blk == 0) | (blk == cur) | (blk == cur - 1)
    valid = jnp.left_shift(blk, SEL_SHIFT) <= tq_row
    score = jnp.where(valid, imp, NEG_INF)
    score = jnp.where(forced | (blk >= n_sel_blocks), BELOW_NEG_INF, score)
    sel = forced
    for _ in range(min(SEL_TOPK, n_sel_blocks) - 3):
        if pieces:
            pieces.pop(0)()
        top = _col_tree(score, jnp.maximum, jnp.max)
        first_hit = _col_tree(jnp.where(score == top, blk_f, float(ns_pad)), jnp.minimum, jnp.min)
        pick = blk_f == first_hit
        sel = sel | pick
        score = jnp.where(pick, BELOW_NEG_INF, score)
    n_diag_blk = n * (Q_BLOCK // SEL_LEN)
    unselected = jnp.where(sel & (blk < n_diag_blk), 0.0, 1.0).astype(BF16)

    while pieces:
        pieces.pop(0)()

    def finish(name, sink=None):
        acc = sum(partial[name][1:], partial[name][0])
        den = acc[HEAD_DIM:HEAD_DIM + 1]
        if sink is not None:
            den = den + jnp.exp2(sink - maxes[name])
        return acc[:HEAD_DIM] * (1.0 / den)

    m_sc[...] = maxes["d"]
    acc_sc[...] = partial["d"][0]

    q_aug = jnp.concatenate([per_group(lambda i: _tile_lanes(unselected[:, i * Q_BLOCK:(i + 1) * Q_BLOCK], rep)),
                             q_t], axis=0)
    tk = s_sc.shape[1]
    n_tiles = ks_ref.shape[1] // tk

    def logits(kt):
        keys = pl.ds(pl.multiple_of(kt * tk, tk), tk)
        return _dot(jnp.concatenate([mask_ref[keys, :], ks_ref[0, keys, :]], axis=1), q_aug)

    def absorb(kt, s):
        m_old = m_sc[...]
        m_new = jnp.maximum(m_old, jnp.max(s, axis=0, keepdims=True))
        pv = apply_values(vbig_ref[0, kt], jnp.exp2(s - m_new).astype(BF16))
        acc_sc[...] = jnp.exp2(m_old - m_new) * acc_sc[...] + pv
        m_sc[...] = m_new

    s_sc[0] = logits(0)
    ob_ref[0] = _unstack_heads_t(finish("b", sink_row), n_groups * SWA_REP).astype(ob_ref.dtype)
    o_w = finish("w")

    def tile_pair(j, carry):
        s_sc[1] = logits(2 * j + 1)
        absorb(2 * j, s_sc[0])
        s_sc[0] = logits(jnp.minimum(2 * j + 2, n_tiles - 1))
        absorb(2 * j + 1, s_sc[1])
        return carry

    n_needed = lax.div(n * Q_BLOCK + tk - 1, tk)
    lax.fori_loop(0, lax.div(n_needed, 2), tile_pair, 0)

    @pl.when(lax.rem(n_needed, 2) == 1)
    def _():
        absorb(n_needed - 1, s_sc[0])

    acc = acc_sc[...]
    o_s = acc[:HEAD_DIM] * (1.0 / acc[HEAD_DIM:HEAD_DIM + 1])

    gates = _sigmoid(gl_ref[0, 0] + gb_ref[...])
    gate = lambda c: jnp.concatenate([gates[GATE_ROWS * i + 3 * r + c:GATE_ROWS * i + 3 * r + c + 1]
                                      for i in groups for r in range(rep)], axis=1)
    o = gate(0) * o_c + gate(1) * o_s + gate(2) * o_w
    oa_ref[0] = _unstack_heads_t(o, n_groups * rep).astype(oa_ref.dtype)


def _selection_map(t, nc_pad, ns_pad):
    nc = (t - CMP_LEN) // CMP_STRIDE + 1
    ns = t // SEL_LEN
    cs = np.arange(nc, dtype=np.int64)[:, None] * CMP_STRIDE
    ss = np.arange(ns, dtype=np.int64)[None, :] * SEL_LEN
    out = np.zeros((nc_pad, ns_pad), np.float32)
    out[:nc, :ns] = (cs < ss + SEL_LEN) & (cs + CMP_LEN > ss)
    return out


def _block_mask_cols(t, ns_pad):
    hit = np.arange(ns_pad, dtype=np.int64)[None, :] == (np.arange(t, dtype=np.int64) // SEL_LEN)[:, None]
    return np.where(hit, NEG_INF, 0.0).astype(np.float32)


def _band_masks(n_rows, window):
    n_front = n_rows // Q_BLOCK - 1
    r = np.arange(n_rows)[None, :, None]
    tok = np.arange(Q_BLOCK)[None, None, :]
    v = np.arange(n_front + 1)[:, None, None]
    dist = n_front * Q_BLOCK + tok - r
    ok = (dist >= 0) & (dist < window) & (r >= (n_front - v) * Q_BLOCK)
    return np.where(ok, 0.0, NEG_INF).astype(np.float32)


def _compressed_masks(n_cmp):
    shift = Q_BLOCK // CMP_STRIDE
    u = np.arange(2 * n_cmp)[None, :, None] + shift * np.arange(2)[:, None, None]
    tok = np.arange(Q_BLOCK)[None, None, :]
    ok = CMP_STRIDE * (u - n_cmp) + CMP_LEN - 1 <= tok
    return np.where(ok, 0.0, NEG_INF).astype(np.float32)


def _attention(q, k_all, v_big, v_small, gate_t, gate_bias, cmp_rows, cmp_cols, sel_map_t, mask_cols, sinks,
               n_sel_blocks):
    b, t = k_all.shape[:2]
    gd = k_all.shape[2] // 3
    g = gd // HEAD_DIM
    tk = v_big.shape[3]
    width = g * NSA_REP * HEAD_DIM
    cols = g * NSA_REP * Q_BLOCK
    gate_cols = gate_t.shape[3] // Q_BLOCK
    prev = lambda n: jnp.maximum(n - 1, 0)
    cur = lambda n: n
    k_tile = lambda blk, at: pl.BlockSpec((1, Q_BLOCK, gd), lambda i, n: (i, at(n), blk))
    k_all_rows = lambda blk: pl.BlockSpec((1, t, gd), lambda i, n: (i, 0, blk))
    v_tile = lambda a, at: pl.BlockSpec((1, 1, 1, gd, Q_BLOCK), lambda i, n: (a, i, at(n), 0, 0))
    out_spec = pl.BlockSpec((1, Q_BLOCK, width), lambda i, n: (i, n, 0))
    out_shape = jax.ShapeDtypeStruct((b, t, width), BF16)
    tri = np.where(np.arange(Q_BLOCK)[:, None] <= np.arange(Q_BLOCK)[None, :], 0.0, NEG_INF)
    consts = [jnp.asarray(c, BF16) for c in (
        np.tile(np.eye(Q_BLOCK, dtype=np.float32), (1, g * NSA_REP)),
        _compressed_masks(cmp_rows.shape[2]), tri,
        _band_masks(NSA_WINDOW + Q_BLOCK, NSA_WINDOW), _band_masks(SWA_WINDOW + Q_BLOCK, SWA_WINDOW))]
    return pl.pallas_call(
        functools.partial(_attn_kernel, n_sel_blocks),
        grid=(b, t // Q_BLOCK),
        in_specs=[
            pl.BlockSpec(memory_space=pltpu.SMEM),
            pl.BlockSpec((1, 1, width, Q_BLOCK), lambda i, n: (i, n // gate_cols, 0, n % gate_cols)),
            pl.BlockSpec((1, 1, width, Q_BLOCK), lambda i, n: (i, n // gate_cols, 1, n % gate_cols)),
            pl.BlockSpec((1, 1, GATE_ROWS * g, Q_BLOCK), lambda i, n: (i, n // gate_cols, 0, n % gate_cols)),
            _resident(gate_bias.shape),
            pl.BlockSpec((1, 1) + cmp_rows.shape[2:], lambda i, n: (0, i, 0, 0)),
            pl.BlockSpec((1, 1) + cmp_cols.shape[2:], lambda i, n: (1, i, 0, 0)),
            _resident(sel_map_t.shape),
            k_tile(0, cur), k_all_rows(0), _resident(mask_cols.shape),
            v_tile(0, cur), pl.BlockSpec((1,) + v_big.shape[1:], lambda i, n: (i, 0, 0, 0)),
            k_all_rows(1), pl.BlockSpec((1, 1) + v_small.shape[2:], lambda i, n: (1, i, 0, 0, 0)),
            k_tile(2, prev), k_tile(2, cur), v_tile(2, prev), v_tile(2, cur),
        ] + [_resident(c.shape) for c in consts],
        out_specs=[out_spec, out_spec],
        out_shape=[out_shape, out_shape],
        scratch_shapes=[
            pltpu.VMEM((1, cols), F32),
            pltpu.VMEM((HEAD_DIM + SUBLANES, cols), F32),
            pltpu.VMEM((2, tk, cols), F32),
        ],
        compiler_params=_cparams(2),
        name="nsa_swa_attention",
    )(sinks, q, q, gate_t, gate_bias, cmp_rows, cmp_cols, sel_map_t, k_all, k_all, mask_cols, v_small, v_big,
      k_all, v_small, k_all, k_all, v_small, v_small, *consts)


def _conv_silu_kernel(n_q_cols, x_ref, prev_ref, shift_ref, w_ref, b_ref, o_ref, buf):
    i = pl.program_id(1)
    rows = x_ref.shape[1]
    halo = prev_ref.shape[1]
    taps = CONV_WIDTH - 1
    buf[:halo, :] = jnp.where(i > 0, prev_ref[0], jnp.zeros_like(prev_ref[0]))
    buf[halo:, :] = x_ref[0]
    col = lax.broadcasted_iota(jnp.int32, (Q_BLOCK, x_ref.shape[2]), 1)
    for r in range(rows // Q_BLOCK):
        win = buf[r * Q_BLOCK:(r + 1) * Q_BLOCK + halo, :]
        back = _dot(shift_ref[...], win)
        y = b_ref[...] + w_ref[taps:taps + 1, :] * win[halo:].astype(F32)
        for j in range(taps):
            y = y + w_ref[j:j + 1, :] * back[j * Q_BLOCK:(j + 1) * Q_BLOCK]
        y = _silu(y)
        y = jnp.where(col >= n_q_cols, y * (MLSTM_QK_DIM ** -0.5), y)
        o_ref[0, r * Q_BLOCK:(r + 1) * Q_BLOCK, :] = y.astype(o_ref.dtype)


def _conv_silu(x, w, bias):
    b, t, c = x.shape
    halo = 2 * SUBLANES
    ratio = ROW_TILE // halo
    taps = CONV_WIDTH - 1
    shift = np.zeros((taps * Q_BLOCK, Q_BLOCK + halo), np.float32)
    for j in range(taps):
        shift[j * Q_BLOCK + np.arange(Q_BLOCK), np.arange(Q_BLOCK) + halo - taps + j] = 1.0
    return pl.pallas_call(
        functools.partial(_conv_silu_kernel, c // 2),
        grid=(b, t // ROW_TILE),
        in_specs=[
            pl.BlockSpec((1, ROW_TILE, c), lambda i, j: (i, j, 0)),
            pl.BlockSpec((1, halo, c), lambda i, j: (i, jnp.maximum(j * ratio - 1, 0), 0)),
            _resident(shift.shape),
            _resident(w.shape),
            _resident((1, c)),
        ],
        out_specs=pl.BlockSpec((1, ROW_TILE, c), lambda i, j: (i, j, 0)),
        out_shape=jax.ShapeDtypeStruct((b, t, c), BF16),
        scratch_shapes=[pltpu.VMEM((ROW_TILE + halo, c), BF16)],
        compiler_params=_cparams(2),
        name="conv_silu",
    )(x, x, jnp.asarray(shift, BF16), w, bias.reshape(1, c))


def _mlstm_kernel(bias_ref, q_ref, k_ref, v_ref, g_ref, o_ref, state, m_state):
    c = pl.program_id(1)
    L = SCAN_CHUNK
    dqk, dv = MLSTM_QK_DIM, MLSTM_V_DIM

    @pl.when(c == 0)
    def _():
        state[...] = jnp.zeros(state.shape, F32)
        m_state[...] = jnp.zeros(m_state.shape, F32)

    t_idx = lax.broadcasted_iota(jnp.int32, (L, L), 0)
    s_idx = lax.broadcasted_iota(jnp.int32, (L, L), 1)
    causal = s_idx <= t_idx
    eye = s_idx == t_idx
    ones_col = (lax.broadcasted_iota(jnp.int32, (L, LANES), 1) == 0).astype(BF16)

    def to_col(row):
        return jnp.sum(jnp.where(eye, row, 0.0), axis=1, keepdims=True)

    heads = range(MLSTM_HEADS)
    for j in range(SCAN_ROWS // SCAN_CHUNK):
        rows = slice(j * L, (j + 1) * L)
        qc = [q_ref[0, rows, h * dqk:(h + 1) * dqk] for h in heads]
        kc = [k_ref[0, rows, h * dqk:(h + 1) * dqk] for h in heads]
        va = [jnp.concatenate([v_ref[0, rows, h * dv:(h + 1) * dv], ones_col], axis=1) for h in heads]
        qk = [lax.dot_general(qc[h], kc[h], (((1,), (1,)), ((), ())), preferred_element_type=F32) for h in heads]
        i_row, b_col, b_row = [], [], []
        for h in heads:
            i_row.append(g_ref[0, 0, h:h + 1, rows] + bias_ref[0, h])
            f_pre = g_ref[0, 0, MLSTM_HEADS + h:MLSTM_HEADS + h + 1, rows] + bias_ref[1, h]
            f_row = jnp.minimum(f_pre, 0.0) - jnp.log(1.0 + jnp.exp(-jnp.abs(f_pre)))
            f_col = to_col(f_row)
            b_col.append(jnp.sum(jnp.where(causal, f_row, 0.0), axis=1, keepdims=True))
            b_row.append(jnp.sum(jnp.where(t_idx <= s_idx, f_col, 0.0), axis=0, keepdims=True))
        m_old = [m_state[h:h + 1, 0:1] for h in heads]
        st = [state[h] for h in heads]
        mt, w_inter, sm = [], [], []
        for h in heads:
            a = b_col[h] + m_old[h]
            d = jnp.where(causal, b_col[h] - b_row[h] + i_row[h], -jnp.inf)
            mt.append(jnp.maximum(a, jnp.max(d, axis=1, keepdims=True)))
            w_inter.append(jnp.exp(a - mt[h]))
            sm.append((qk[h] * jnp.exp(d - mt[h])).astype(BF16))
        for h in heads:
            nd = w_inter[h] * _dot(qc[h], st[h].astype(BF16)) + _dot(sm[h], va[h])
            hc = nd[:, :dv] * (1.0 / jnp.maximum(jnp.abs(nd[:, dv:dv + 1]), jnp.exp(-mt[h])))
            o_ref[0, rows, h * dv:(h + 1) * dv] = hc.astype(o_ref.dtype)
        for h in heads:
            b_last = b_row[h][:, L - 1:L]
            wl = b_last - b_row[h] + i_row[h]
            m_new = jnp.maximum(b_last + m_old[h], jnp.max(wl, axis=1, keepdims=True))
            decay = jnp.exp(b_last + m_old[h] - m_new)
            w_col = to_col(jnp.exp(wl - m_new))
            kw = (kc[h].astype(F32) * w_col).astype(BF16)
            upd = lax.dot_general(kw, va[h], (((0,), (0,)), ((), ())), preferred_element_type=F32)
            state[h] = decay * st[h] + upd
            m_state[h:h + 1, 0:1] = m_new


def _mlstm(qk, v, gates_t, gate_bias):
    b, t = qk.shape[:2]
    nh = MLSTM_HEADS
    rows = SCAN_ROWS
    assert gates_t.shape[3] == rows
    return pl.pallas_call(
        _mlstm_kernel,
        grid=(b, t // rows),
        in_specs=[
            pl.BlockSpec(memory_space=pltpu.SMEM),
            pl.BlockSpec((1, rows, nh * MLSTM_QK_DIM), lambda i, c: (i, c, 0)),
            pl.BlockSpec((1, rows, nh * MLSTM_QK_DIM), lambda i, c: (i, c, 1)),
            pl.BlockSpec((1, rows, nh * MLSTM_V_DIM), lambda i, c: (i, c, 0)),
            pl.BlockSpec((1, 1) + gates_t.shape[2:], lambda i, c: (i, c, 0, 0)),
        ],
        out_specs=pl.BlockSpec((1, rows, nh * MLSTM_V_DIM), lambda i, c: (i, c, 0)),
        out_shape=jax.ShapeDtypeStruct((b, t, nh * MLSTM_V_DIM), BF16),
        scratch_shapes=[
            pltpu.VMEM((nh, MLSTM_QK_DIM, MLSTM_V_DIM + LANES), F32),
            pltpu.VMEM((SUBLANES, LANES), F32),
        ],
        compiler_params=_cparams(2),
        name="mlstm_scan",
    )(gate_bias, qk, qk, v, gates_t)


def _block_diag(w, n):
    zero = jnp.zeros_like(w)
    return jnp.concatenate([jnp.concatenate([w if i == j else zero for i in range(n)], axis=-1)
                            for j in range(n)], axis=-2)


def _nsa_swa_layer(h, norm_g, w_in, gate_bias, pos_k, pos_v, ck_w1, ck_w2, cv_w1, cv_w2, sinks, w_out):
    b, t, d = h.shape
    m = b * t
    ng = NSA_KV_HEADS
    gk = ng * HEAD_DIM
    nq_a, nq_b = NSA_HEADS * HEAD_DIM, SWA_HEADS * HEAD_DIM
    ngate = NSA_HEADS * 3
    o_kc, o_vc, o_ks, o_vs, o_kw, o_vw = (nq_a + i * gk for i in range(6))
    o_g = nq_a + 6 * gk
    o_qb = o_g + ngate
    o_kb, o_vb = o_qb + nq_b, o_qb + nq_b + gk
    cols = lambda o, n: w_in[:, o:o + n]
    q_scale = HEAD_DIM ** -0.5 * LOG2_E
    gate_pad = jnp.zeros((d, GATE_ROWS - NSA_REP * 3), w_in.dtype)
    w_gate = jnp.concatenate([p for i in range(ng) for p in (cols(o_g + i * NSA_REP * 3, NSA_REP * 3), gate_pad)]
                             + [jnp.zeros((d, LANES - GATE_ROWS * ng), w_in.dtype)], axis=1)
    w_all = jnp.concatenate([cols(0, nq_a) * q_scale, cols(o_qb, nq_b) * q_scale,
                             cols(o_ks, gk), cols(o_kw, gk), cols(o_kb, gk),
                             cols(o_vs, gk), cols(o_vw, gk), cols(o_vb, gk),
                             cols(o_kc, gk), cols(o_vc, gk), w_gate], axis=1).astype(BF16)
    q_all, k_all, cmp_in, v_big, v_small, gate_t = _proj0(
        h.reshape(m, d), norm_g, w_all, [nq_a + nq_b, 3 * gk, 3 * gk, 2 * gk + LANES], 2 * gk, GATE_ROWS * ng)
    tk = ROW_TILE
    q_all = q_all.reshape(b, t // ROW_TILE, nq_a + nq_b, ROW_TILE)
    k_all = k_all.reshape(b, t, 3 * gk)
    v_big = v_big.reshape(b, t // tk, gk, tk)
    v_small = v_small.reshape(3, b, t // Q_BLOCK, gk, Q_BLOCK)
    gate_t = gate_t.reshape(b, t // ROW_TILE, GATE_ROWS * ng, ROW_TILE)
    gb = jnp.pad(gate_bias.reshape(ng, NSA_REP * 3),
                 ((0, 0), (0, GATE_ROWS - NSA_REP * 3))).reshape(ng * GATE_ROWS, 1)

    tile2 = lambda p: jnp.tile(p[:, None, :], (1, 1, ng))
    pos = jnp.stack([tile2(pos_k), tile2(pos_v)])
    per_pos = lambda w: w.reshape(CMP_LEN, HEAD_DIM, HEAD_DIM)
    w1 = _block_diag(jnp.stack([per_pos(ck_w1), per_pos(cv_w1)]), ng).astype(BF16)
    w2 = _block_diag(jnp.stack([ck_w2, cv_w2]), ng).astype(BF16)
    cmp_rows, cmp_cols = _compress(cmp_in.reshape(b, t, 2 * gk), pos, w1, w2)

    n_sel = t // SEL_LEN
    ns_pad = -(-n_sel // LANES) * LANES
    sel_map_t = jnp.asarray(_selection_map(t, t // CMP_STRIDE, ns_pad).T, BF16)
    mask_cols = jnp.asarray(_block_mask_cols(t, ns_pad), BF16)
    o_a, o_b = _attention(q_all, k_all, v_big, v_small, gate_t, gb, cmp_rows, cmp_cols, sel_map_t, mask_cols,
                          sinks, n_sel)

    w_o = w_out.astype(BF16)
    return [o_a.reshape(m, nq_a), o_b.reshape(m, nq_b)], [w_o[:nq_a], w_o[nq_a:]], None


def _mlstm_layer(h, norm_g, w_in, conv_w, conv_b, igate_bias, fgate_bias, w_out):
    b, t, d = h.shape
    m = b * t
    nqk = 2 * MLSTM_HEADS * MLSTM_QK_DIM
    nv = MLSTM_HEADS * MLSTM_V_DIM
    w = jnp.concatenate([w_in, jnp.zeros((d, LANES - 2 * MLSTM_HEADS), w_in.dtype)], axis=1).astype(BF16)
    qk_raw, v, og, gates_t = _norm_proj(h.reshape(m, d), norm_g, w, [nqk, nv, nv, LANES], [BF16, BF16, BF16],
                                        2 * MLSTM_HEADS)
    qk = _conv_silu(qk_raw.reshape(b, t, nqk), conv_w, conv_b)
    gates_t = gates_t.reshape(b, t // ROW_TILE, 2 * MLSTM_HEADS, ROW_TILE)
    hh = _mlstm(qk, v.reshape(b, t, nv), gates_t, jnp.stack([igate_bias, fgate_bias]))
    return [hh.reshape(m, nv)], [w_out.astype(BF16)], og


def kernel(x, norm_mix, norm_ffn, ffn_w_gate, ffn_w_up, ffn_w_down, ab_w_in, ab_gate_bias, nsa_pos_k, nsa_pos_v, nsa_cmp_k_w1, nsa_cmp_k_w2, nsa_cmp_v_w1, nsa_cmp_v_w2, swa_sinks, ab_w_out, c_w_in, c_conv_w, c_conv_b, c_igate_bias, c_fgate_bias, c_w_out, final_norm):
    depth = norm_mix.shape[0]
    b, t, d = x.shape
    h = x
    ffn_w = [w.astype(BF16) for w in (ffn_w_gate, ffn_w_up, ffn_w_down)]
    for layer in range(depth):
        j = layer // 2
        if layer % 2 == 0:
            mixed = _nsa_swa_layer(h, norm_mix[layer], ab_w_in[j], ab_gate_bias[j], nsa_pos_k[j], nsa_pos_v[j],
                               nsa_cmp_k_w1[j], nsa_cmp_k_w2[j], nsa_cmp_v_w1[j], nsa_cmp_v_w2[j],
                               swa_sinks[j], ab_w_out[j])
        else:
            mixed = _mlstm_layer(h, norm_mix[layer], c_w_in[j], c_conv_w[j], c_conv_b[j], c_igate_bias[j],
                             c_fgate_bias[j], c_w_out[j])
        last = layer == depth - 1
        h = _mix_ffn(h.reshape(b * t, d), *mixed, norm_ffn[layer], layer, *ffn_w,
                     final_norm if last else None).reshape(b, t, d)
    if depth == 0:
        raise ValueError("depth must be positive")
    return h
```
